```python
import math
import jax, jax.numpy as jnp
from jax import lax
import numpy as np


D_MODEL = 1024
BATCH = 8
SEQ = 2048
DEPTH = 2
DEC_BATCH = 128
DEC_SEQ = 8
PAST_LEN = 16384
PAGE_SIZE = 128

MIX_WIDTH = D_MODEL // 2
HEAD_DIM = 128
N_HEADS = MIX_WIDTH // HEAD_DIM
N_BRANCH = 3
CONV_W = 4
CHUNK = 32
ROPE_BASE = 10000.0
N_GROUPS = 4
EXPERTS_PER_GROUP = 8
N_EXPERTS = N_GROUPS * EXPERTS_PER_GROUP
TOP_K_IN_GROUP = 2
D_EXPERT = D_MODEL // 2
MOE_BLOCK = 128
NORM_EPS = 1e-6
HEAD_NORM_EPS = 1e-5
IN_COLS = 10 * MIX_WIDTH + N_BRANCH * D_MODEL

kernel_name = 'hybrid_hgrn2_retention_mlstm_hmoe_step'

F32 = jnp.float32


def rmsnorm(x, w):
    xf = x.astype(F32)
    y = xf * lax.rsqrt(jnp.mean(xf * xf, -1, keepdims=True) + NORM_EPS)
    return (y * w.astype(F32)).astype(x.dtype)


def head_rmsnorm(o, w):
    B, T, H, d = o.shape
    y = o * lax.rsqrt(jnp.mean(o * o, -1, keepdims=True) + HEAD_NORM_EPS)
    return y.reshape(B, T, H * d) * w.astype(F32)


def head_layernorm(o, w):
    B, T, H, d = o.shape
    c = o - jnp.mean(o, -1, keepdims=True)
    y = c * lax.rsqrt(jnp.mean(c * c, -1, keepdims=True) + HEAD_NORM_EPS)
    return y.reshape(B, T, H * d) * w.astype(F32)


def rotary(a, pos):
    half = a.shape[-1] // 2
    inv = ROPE_BASE ** (-jnp.arange(half, dtype=F32) / half)
    ang = pos.astype(F32)[:, None] * inv[None, :]
    cos = jnp.cos(ang)[None, :, None, :]
    sin = jnp.sin(ang)[None, :, None, :]
    a1, a2 = a[..., :half], a[..., half:]
    return jnp.concatenate([a1 * cos - a2 * sin, a1 * sin + a2 * cos], -1)


def to_chunks(a, L):
    B, T, H = a.shape[:3]
    a = a.reshape((B, T // L, L, H) + a.shape[3:])
    return jnp.moveaxis(a, (1, 3), (0, 2))


def from_chunks(a):
    NC, B, H, L, d = a.shape
    return jnp.moveaxis(a, (0, 2), (1, 3)).reshape(B, NC * L, H, d)


def hgrn2_scan(q, k, v, logf, S0):
    T = q.shape[1]
    L = math.gcd(T, CHUNK)
    causal = jnp.tril(jnp.ones((L, L), bool))[None, None, :, :, None]

    def step(S, inp):
        qc, kc, vc, gc = inp
        G = jnp.cumsum(gc, axis=2)
        diff = G[:, :, :, None, :] - G[:, :, None, :, :]
        dec = jnp.where(causal, jnp.exp(jnp.where(causal, diff, 0.0)), 0.0)
        A = jnp.einsum('bhtk,bhsk,bhtsk->bhts', qc, kc, dec)
        o = jnp.einsum('bhts,bhsv->bhtv', A, vc) + jnp.einsum('bhtk,bhkv->bhtv', qc * jnp.exp(G), S)
        GL = G[:, :, -1:, :]
        S = jnp.exp(GL[:, :, 0, :])[..., None] * S + jnp.einsum('bhsk,bhsv->bhkv', kc * jnp.exp(GL - G), vc)
        return S, o

    S, o = lax.scan(step, S0, (to_chunks(q, L), to_chunks(k, L), to_chunks(v, L), to_chunks(logf, L)))
    return from_chunks(o), S


def retention_scan(q, k, v, S0, log_gamma):
    T = q.shape[1]
    L = math.gcd(T, CHUNK)
    idx = jnp.arange(L, dtype=F32)
    rel = idx[:, None] - idx[None, :]
    causal = rel >= 0
    lg = log_gamma[:, None, None]
    Dm = jnp.where(causal[None], jnp.exp(jnp.where(causal, rel, 0.0)[None] * lg), 0.0)
    q_in = jnp.exp((idx + 1.0)[None, :] * log_gamma[:, None])
    k_out = jnp.exp((L - 1.0 - idx)[None, :] * log_gamma[:, None])
    g_chunk = jnp.exp(L * log_gamma)

    def step(S, inp):
        qc, kc, vc = inp
        A = jnp.einsum('bhtd,bhsd->bhts', qc, kc) * Dm
        o = jnp.einsum('bhts,bhsv->bhtv', A, vc) + q_in[None, :, :, None] * jnp.einsum('bhtd,bhdv->bhtv', qc, S)
        S = g_chunk[None, :, None, None] * S + jnp.einsum('bhsd,bhsv->bhdv', kc * k_out[None, :, :, None], vc)
        return S, o

    S, o = lax.scan(step, S0, (to_chunks(q, L), to_chunks(k, L), to_chunks(v, L)))
    return from_chunks(o), S


def mlstm_scan(q, k, v, logi, logf, C0, n0, m0):
    T = q.shape[1]
    L = math.gcd(T, CHUNK)
    causal = jnp.tril(jnp.ones((L, L), bool))

    def step(carry, inp):
        C, n, m_prev = carry
        qc, kc, vc, ic, fc = inp
        F = jnp.cumsum(fc, axis=-1)
        m = F + jnp.maximum(m_prev[..., None], lax.cummax(ic - F, axis=2))
        logD = F[..., :, None] - F[..., None, :] + ic[..., None, :] - m[..., :, None]
        Dm = jnp.where(causal, jnp.exp(jnp.where(causal, logD, 0.0)), 0.0)
        Sc = jnp.einsum('bhtd,bhsd->bhts', qc, kc) * Dm
        inter = jnp.exp(F + m_prev[..., None] - m)
        num = jnp.einsum('bhts,bhse->bhte', Sc, vc) + inter[..., None] * jnp.einsum('bhtd,bhde->bhte', qc, C)
        den = Sc.sum(-1) + inter * jnp.einsum('bhtd,bhd->bht', qc, n)
        hout = num / jnp.maximum(jnp.abs(den), jnp.exp(-m))[..., None]
        mL = m[..., -1]
        FL = F[..., -1]
        w = jnp.exp(FL[..., None] - F + ic - mL[..., None])
        decay = jnp.exp(FL + m_prev - mL)
        C = decay[..., None, None] * C + jnp.einsum('bhsd,bhs,bhse->bhde', kc, w, vc)
        n = decay[..., None] * n + jnp.einsum('bhsd,bhs->bhd', kc, w)
        return (C, n, mL), hout

    (C, n, m), h = lax.scan(step, (C0, n0, m0),
                            (to_chunks(q, L), to_chunks(k, L), to_chunks(v, L), to_chunks(logi, L), to_chunks(logf, L)))
    return from_chunks(h), C, n, m


def mixer_block(h, pos, l, S_h, S_r, C0, n0, m0, buf, p):
    B, T, _ = h.shape
    W, H, d = MIX_WIDTH, N_HEADS, HEAD_DIM
    z = jnp.matmul(h, p['w_in'][l]).astype(F32)
    col = lambda j: z[..., j * W:(j + 1) * W]
    heads = lambda a: a.reshape(B, T, H, d)

    a_q, a_f, a_i, a_g = col(0), col(1), col(2), col(3)
    lb = p['lb'][l]
    logf_a = jnp.logaddexp(jnp.log(lb), jnp.log1p(-lb) + jax.nn.log_sigmoid(a_f))
    a_k = (1.0 - lb) * jax.nn.sigmoid(-a_f)
    oA_h, S_h_new = hgrn2_scan(heads(a_q), heads(a_k), heads(a_i), heads(logf_a), S_h.astype(F32))
    oA = head_rmsnorm(oA_h, p['hgrn_norm'][l]) * jax.nn.silu(a_g)

    log_gamma = jnp.log(1.0 - 2.0 ** (-5.0 - jnp.arange(H, dtype=F32)))
    r_q = rotary(heads(col(4)), pos)
    r_k = rotary(heads(col(5)), pos) * (d ** -0.5)
    r_v, r_g = heads(col(6)), col(7)
    oB_h, S_r_new = retention_scan(r_q, r_k, r_v, S_r.astype(F32), log_gamma)
    oB = head_layernorm(oB_h, p['ret_norm'][l]) * jax.nn.silu(r_g)

    u, mz = col(8), col(9)
    ext = jnp.concatenate([buf.astype(F32), u], axis=1)
    conv = lax.conv_general_dilated(ext, p['conv_w'][l].astype(F32)[:, None, :], window_strides=(1,), padding='VALID',
                                    dimension_numbers=('NWC', 'WIO', 'NWC'), feature_group_count=W)
    uc = jax.nn.silu(conv + p['conv_b'][l])
    m_q = jnp.einsum('bthd,hde->bthe', heads(uc), p['mq'][l])
    m_k = jnp.einsum('bthd,hde->bthe', heads(uc), p['mk'][l])
    m_v = jnp.einsum('bthd,hde->bthe', heads(u), p['mv'][l])
    qkv = jnp.concatenate([m_q.reshape(B, T, W), m_k.reshape(B, T, W), m_v.reshape(B, T, W)], -1)
    gpre = jnp.matmul(qkv, p['m_wg'][l]) + p['m_bg'][l]
    logi_c = gpre[..., :H]
    logf_c = jax.nn.log_sigmoid(gpre[..., H:])
    hC, C_new, n_new, m_new = mlstm_scan(m_q, m_k * (d ** -0.5), m_v, logi_c, logf_c,
                                         C0.astype(F32), n0.astype(F32), m0.astype(F32))
    oC = (head_layernorm(hC, p['m_norm'][l]) + p['m_skip'][l] * uc) * jax.nn.silu(mz)
    buf_new = ext[:, ext.shape[1] - (CONV_W - 1):]

    gates = jax.nn.sigmoid(z[..., 10 * W:].reshape(B, T, N_BRANCH, D_MODEL))
    branches = jnp.stack([oA, oB, oC], axis=2)
    proj = jnp.einsum('btnw,nwd->btnd', branches, p['w_branch'][l])
    merged = jnp.sum(gates * proj, axis=2).astype(h.dtype)
    out = jnp.matmul(merged, p['w_out'][l])
    return out, (S_h_new, S_r_new, C_new, n_new, m_new, buf_new)


def grouped_experts(xt, e_id, e_w, w_g, w_u, w_d):
    N, D = xt.shape
    P = N * TOP_K_IN_GROUP
    flat_e = e_id.reshape(-1).astype(jnp.int32)
    order = jnp.argsort(flat_e)
    sorted_e = flat_e[order]
    tok = order // TOP_K_IN_GROUP
    counts = jnp.bincount(flat_e, length=N_EXPERTS)
    padded = (counts + MOE_BLOCK - 1) // MOE_BLOCK * MOE_BLOCK
    pad_end = jnp.cumsum(padded)
    pad_start = pad_end - padded
    start = jnp.cumsum(counts) - counts
    dest = pad_start[sorted_e] + jnp.arange(P, dtype=jnp.int32) - start[sorted_e]
    n_blocks = -(-P // MOE_BLOCK) + N_EXPERTS
    cap = n_blocks * MOE_BLOCK
    row_tok = jnp.zeros((cap,), jnp.int32).at[dest].set(tok)
    block_e = jnp.minimum(jnp.searchsorted(pad_end, jnp.arange(n_blocks, dtype=jnp.int32) * MOE_BLOCK, side='right'),
                          N_EXPERTS - 1)
    xb = xt[row_tok].reshape(n_blocks, MOE_BLOCK, D)

    def run_block(args):
        xblk, e = args
        return jnp.matmul(jax.nn.silu(jnp.matmul(xblk, w_g[e])) * jnp.matmul(xblk, w_u[e]), w_d[e])

    yb = lax.map(run_block, (xb, block_e)).reshape(cap, D)
    y_pairs = yb[dest].astype(F32) * e_w.reshape(-1)[order][:, None]
    return jax.ops.segment_sum(y_pairs, tok, num_segments=N)


def hier_moe(h, l, p):
    B, T, D = h.shape
    xt = h.reshape(B * T, D)
    xf = xt.astype(F32)
    g_logits = jnp.matmul(xf, p['w_rg'][l]) + p['b_rg'][l]
    g_prob = jax.nn.softmax(g_logits, axis=-1)
    _, g_top = lax.top_k(g_logits, 1)
    g_w = jnp.take_along_axis(g_prob, g_top, axis=1)
    e_logits = (jnp.matmul(xf, p['w_re'][l]) + p['b_re'][l]).reshape(-1, N_GROUPS, EXPERTS_PER_GROUP)
    e_logits = jnp.take_along_axis(e_logits, g_top[:, :, None], axis=1)[:, 0]
    top_v, top_i = lax.top_k(e_logits, TOP_K_IN_GROUP)
    e_w = jax.nn.softmax(top_v, axis=-1) * g_w
    e_id = g_top * EXPERTS_PER_GROUP + top_i
    y = grouped_experts(xt, e_id, e_w, p['w_eg'][l], p['w_eu'][l], p['w_ed'][l])
    return y.reshape(B, T, D).astype(h.dtype)


def run_trunk(x, pos, states, p):
    st = states
    outs = ([], [], [], [], [], [])
    for l in range(DEPTH):
        hn = rmsnorm(x, p['norm_mix'][l])
        mix, new = mixer_block(hn, pos, l, st[0][l], st[1][l], st[2][l], st[3][l], st[4][l], st[5][l], p)
        x = x + mix.astype(x.dtype)
        hn = rmsnorm(x, p['norm_ffn'][l])
        x = x + hier_moe(hn, l, p)
        for acc, s in zip(outs, new):
            acc.append(s)
    return rmsnorm(x, p['norm_final']), tuple(jnp.stack(a) for a in outs)


def setup_inputs(seed: int = 0) -> dict:
    key = jax.random.key(seed)
    ks = iter(jax.random.split(key, 64))
    nrm = lambda shape, s: jax.random.normal(next(ks), shape, F32) * s
    W, H, d, D, F = MIX_WIDTH, N_HEADS, HEAD_DIM, D_MODEL, D_EXPERT
    inp = {}
    inp['x_prompt'] = nrm((BATCH, SEQ, D), 1.0)
    inp['x_sample'] = nrm((DEC_BATCH, DEC_SEQ, D), 1.0)
    inp['state_hgrn'] = nrm((DEPTH, DEC_BATCH, H, d, d), 0.5)
    inp['state_ret'] = nrm((DEPTH, DEC_BATCH, H, d, d), 0.5)
    inp['state_mlstm_C'] = nrm((DEPTH, DEC_BATCH, H, d, d), 0.1)
    inp['state_mlstm_n'] = nrm((DEPTH, DEC_BATCH, H, d), 0.1)
    inp['state_mlstm_m'] = nrm((DEPTH, DEC_BATCH, H), 1.0)
    inp['state_mlstm_conv'] = nrm((DEPTH, DEC_BATCH, CONV_W - 1, W), 1.0)
    inp['norm_mix'] = 1.0 + nrm((DEPTH, D), 0.01)
    inp['norm_ffn'] = 1.0 + nrm((DEPTH, D), 0.01)
    inp['norm_final'] = 1.0 + nrm((D,), 0.01)
    inp['w_in'] = nrm((DEPTH, D, IN_COLS), D ** -0.5)
    inp['hgrn_lb'] = nrm((DEPTH, W), 0.5)
    inp['hgrn_norm'] = 1.0 + nrm((DEPTH, W), 0.01)
    inp['ret_norm'] = 1.0 + nrm((DEPTH, W), 0.01)
    inp['mlstm_conv_w'] = nrm((DEPTH, CONV_W, W), CONV_W ** -0.5)
    inp['mlstm_conv_b'] = nrm((DEPTH, W), 0.01)
    inp['mlstm_wq'] = nrm((DEPTH, H, d, d), d ** -0.5)
    inp['mlstm_wk'] = nrm((DEPTH, H, d, d), d ** -0.5)
    inp['mlstm_wv'] = nrm((DEPTH, H, d, d), d ** -0.5)
    inp['mlstm_w_gates'] = nrm((DEPTH, 3 * W, 2 * H), (3 * W) ** -0.5)
    b_i = nrm((DEPTH, H), 0.1)
    b_f = jnp.linspace(3.0, 6.0, H, dtype=F32)[None, :] + nrm((DEPTH, H), 0.1)
    inp['mlstm_b_gates'] = jnp.concatenate([b_i, b_f], axis=-1)
    inp['mlstm_norm'] = 1.0 + nrm((DEPTH, W), 0.01)
    inp['mlstm_skip'] = 1.0 + nrm((DEPTH, W), 0.01)
    inp['w_branch'] = nrm((DEPTH, N_BRANCH, W, D), W ** -0.5)
    inp['w_out'] = nrm((DEPTH, D, D), D ** -0.5)
    inp['w_router_group'] = nrm((DEPTH, D, N_GROUPS), D ** -0.5)
    inp['b_router_group'] = nrm((DEPTH, N_GROUPS), 0.01)
    inp['w_router_expert'] = nrm((DEPTH, D, N_EXPERTS), D ** -0.5)
    inp['b_router_expert'] = nrm((DEPTH, N_EXPERTS), 0.01)
    inp['w_exp_gate'] = nrm((DEPTH, N_EXPERTS, D, F), D ** -0.5)
    inp['w_exp_up'] = nrm((DEPTH, N_EXPERTS, D, F), D ** -0.5)
    inp['w_exp_down'] = nrm((DEPTH, N_EXPERTS, F, D), F ** -0.5)
    return inp


def reference(x_prompt, x_sample, state_hgrn, state_ret, state_mlstm_C, state_mlstm_n, state_mlstm_m, state_mlstm_conv,
              norm_mix, norm_ffn, norm_final, w_in, hgrn_lb, hgrn_norm, ret_norm, mlstm_conv_w, mlstm_conv_b,
              mlstm_wq, mlstm_wk, mlstm_wv, mlstm_w_gates, mlstm_b_gates, mlstm_norm, mlstm_skip, w_branch, w_out,
              w_router_group, b_router_group, w_router_expert, b_router_expert, w_exp_gate, w_exp_up, w_exp_down):
    lb_all = jnp.cumsum(jax.nn.softmax(hgrn_lb.astype(F32), axis=0), axis=0)
    lb_all = lb_all - lb_all[0:1]
    p = {'norm_mix': norm_mix, 'norm_ffn': norm_ffn, 'norm_final': norm_final, 'w_in': w_in, 'lb': lb_all,
         'hgrn_norm': hgrn_norm, 'ret_norm': ret_norm, 'conv_w': mlstm_conv_w, 'conv_b': mlstm_conv_b,
         'mq': mlstm_wq, 'mk': mlstm_wk, 'mv': mlstm_wv, 'm_wg': mlstm_w_gates, 'm_bg': mlstm_b_gates,
         'm_norm': mlstm_norm, 'm_skip': mlstm_skip, 'w_branch': w_branch, 'w_out': w_out,
         'w_rg': w_router_group, 'b_rg': b_router_group, 'w_re': w_router_expert, 'b_re': b_router_expert,
         'w_eg': w_exp_gate, 'w_eu': w_exp_up, 'w_ed': w_exp_down}

    Bp, Tp = x_prompt.shape[0], x_prompt.shape[1]
    zero_states = (jnp.zeros((DEPTH, Bp, N_HEADS, HEAD_DIM, HEAD_DIM), F32),
                   jnp.zeros((DEPTH, Bp, N_HEADS, HEAD_DIM, HEAD_DIM), F32),
                   jnp.zeros((DEPTH, Bp, N_HEADS, HEAD_DIM, HEAD_DIM), F32),
                   jnp.zeros((DEPTH, Bp, N_HEADS, HEAD_DIM), F32),
                   jnp.zeros((DEPTH, Bp, N_HEADS), F32),
                   jnp.zeros((DEPTH, Bp, CONV_W - 1, MIX_WIDTH), F32))
    pos_prompt = jnp.arange(Tp, dtype=jnp.int32)
    pos_sample = PAST_LEN + jnp.arange(x_sample.shape[1], dtype=jnp.int32)

    y_prompt, (p_hgrn, p_ret, p_C, p_n, p_m, p_conv) = run_trunk(x_prompt, pos_prompt, zero_states, p)
    sample_states = (state_hgrn, state_ret, state_mlstm_C, state_mlstm_n, state_mlstm_m, state_mlstm_conv)
    y_sample, (s_hgrn, s_ret, s_C, s_n, s_m, s_conv) = run_trunk(x_sample, pos_sample, sample_states, p)
    return (y_prompt, y_sample, p_hgrn, p_ret, p_C, p_n, p_m, p_conv, s_hgrn, s_ret, s_C, s_n, s_m, s_conv)
```

```python
import functools
import math

import jax
import jax.numpy as jnp
from jax import lax
from jax.experimental import pallas as pl
from jax.experimental.pallas import tpu as pltpu

F32 = jnp.float32
BF16 = jnp.bfloat16
HIGHEST = lax.Precision.HIGHEST

HEAD_DIM = 128
N_HEADS = 4
MIX_WIDTH = HEAD_DIM * N_HEADS
N_BRANCH = 3
CONV_W = 4
ROPE_BASE = 10000.0
N_GROUPS = 4
EXPERTS_PER_GROUP = 8
N_EXPERTS = N_GROUPS * EXPERTS_PER_GROUP
TOP_K = 2
PAST_LEN = 16384
NORM_EPS = 1e-6
HEAD_NORM_EPS = 1e-5
SCAN_COLS = 10 * MIX_WIDTH

LANES = 128
SUBLANES = 8
VMEM_LIMIT = 48 * 1024 * 1024
MOE_ROWS = 128
HGRN_DIAG = 8
NEG_INF = float("-inf")


def _cparams(sem):
    return pltpu.CompilerParams(dimension_semantics=sem, vmem_limit_bytes=VMEM_LIMIT)


def _pick(n, cands):
    for c in cands:
        if n % c == 0:
            return c
    return n


def _rms(x, w):
    return x * lax.rsqrt(jnp.mean(x * x, -1, keepdims=True) + NORM_EPS) * w


def _sigmoid(x):
    return 1.0 / (1.0 + jnp.exp(-x))


def _silu(x):
    return x * _sigmoid(x)


def _log_sigmoid(x):
    return jnp.minimum(x, 0.0) - jnp.log1p(jnp.exp(-jnp.abs(x)))


def _dot(a, b):
    return jnp.dot(a.astype(BF16), b.astype(BF16), preferred_element_type=F32)


def _dot_nt(a, b):
    return lax.dot_general(a.astype(BF16), b.astype(BF16), (((1,), (1,)), ((), ())), preferred_element_type=F32)


def _dot_tn(a, b):
    return jnp.dot(a.T.astype(BF16), b.astype(BF16), preferred_element_type=F32)


def _dot_f32(a, b):
    return jnp.dot(a, b, preferred_element_type=F32, precision=HIGHEST)


def _inproj_kernel(x_ref, nw_ref, w_ref, z_ref):
    h = _rms(x_ref[...], nw_ref[...])
    z_ref[...] = jnp.dot(h.astype(BF16), w_ref[...], preferred_element_type=F32)


def _inproj(x, nw, w):
    n, d = x.shape
    cols = w.shape[1]
    tm = _pick(n, (512, 256, 128))
    tn = _pick(cols, (1280, 1024, 512, 256, 128))
    return pl.pallas_call(
        _inproj_kernel,
        grid=(cols // tn, n // tm),
        in_specs=[pl.BlockSpec((tm, d), lambda j, i: (i, 0)),
                  pl.BlockSpec((1, d), lambda j, i: (0, 0)),
                  pl.BlockSpec((d, tn), lambda j, i: (0, j))],
        out_specs=pl.BlockSpec((tm, tn), lambda j, i: (i, j)),
        out_shape=jax.ShapeDtypeStruct((n, cols), F32),
        name="inproj",
        compiler_params=_cparams(("parallel", "parallel")),
    )(x, nw, w)


def _head_rmsnorm(o, w):
    return o * lax.rsqrt(jnp.mean(o * o, -1, keepdims=True) + HEAD_NORM_EPS) * w


def _head_layernorm(o, w):
    c = o - jnp.mean(o, -1, keepdims=True)
    return c * lax.rsqrt(jnp.mean(c * c, -1, keepdims=True) + HEAD_NORM_EPS) * w


def _mixer_kernel(z_ref, cos_ref, sin_ref, sh0_ref, sr0_ref, c0_ref, n0_ref, m0_ref, cb0_ref,
                  prm_ref, convw_ref, mq_ref, mk_ref, mv_ref, wg_ref, wgt_ref, bg_ref, bgt_ref,
                  o_ref, sh_ref, sr_ref, c_ref, n_ref, m_ref, cb_ref, ext_ref,
                  *, tt, ch, n_t):
    ti = pl.program_id(1)
    d = HEAD_DIM
    W = MIX_WIDTH

    @pl.when(ti == 0)
    def _():
        for h in range(N_HEADS):
            sh_ref[0, h] = sh0_ref[0, h].T
        sr_ref[...] = sr0_ref[...]
        c_ref[...] = c0_ref[...]
        n_ref[...] = n0_ref[...]
        m_ref[...] = m0_ref[...]
        cb_ref[...] = cb0_ref[...]

    def prm(i, h):
        return prm_ref[i:i + 1, h * d:(h + 1) * d]

    n_blk = ch // HGRN_DIAG
    tri_ch = (lax.broadcasted_iota(jnp.int32, (ch, ch), 0) >= lax.broadcasted_iota(jnp.int32, (ch, ch), 1)).astype(F32)
    row_id = lax.broadcasted_iota(jnp.int32, (HGRN_DIAG, d), 0)

    def hgrn_chunk(c, carry):
        r0 = pl.multiple_of(c * ch, ch)
        rows = pl.ds(r0, ch)
        for h in range(N_HEADS):
            q = z_ref[0, rows, h * d:(h + 1) * d]
            fpre = z_ref[0, rows, W + h * d:W + (h + 1) * d]
            v = z_ref[0, rows, 2 * W + h * d:2 * W + (h + 1) * d]
            gate = z_ref[0, rows, 3 * W + h * d:3 * W + (h + 1) * d]
            a = prm(0, h)
            b = prm(1, h) + _log_sigmoid(fpre)
            logf = jnp.maximum(a, b) + jnp.log1p(jnp.exp(-jnp.abs(a - b)))
            k = prm(2, h) * _sigmoid(-fpre)
            G = _dot_f32(tri_ch, logf)
            st = sh_ref[0, h]
            o_inter = _dot_nt(q * jnp.exp(G), st)
            outs = []
            for blk in range(n_blk):
                lo = blk * HGRN_DIAG
                hi = lo + HGRN_DIAG
                Gi, qi, ki, vi = G[lo:hi], q[lo:hi], k[lo:hi], v[lo:hi]
                acc = o_inter[lo:hi]
                if blk > 0:
                    gi = G[lo - 1:lo]
                    qh = qi * jnp.exp(Gi - gi)
                    kt = k[:lo] * jnp.exp(gi - G[:lo])
                    acc = acc + _dot(_dot_nt(qh, kt), v[:lo])
                for s in range(HGRN_DIAG):
                    keep = row_id >= s
                    dec = jnp.where(keep, jnp.exp(jnp.where(keep, Gi - Gi[s:s + 1], 0.0)), 0.0)
                    a_s = jnp.sum(qi * dec * ki[s:s + 1], -1, keepdims=True)
                    acc = acc + a_s * vi[s:s + 1]
                outs.append(acc)
            o_h = jnp.concatenate(outs, axis=0) if n_blk > 1 else outs[0]
            GL = G[ch - 1:ch]
            sh_ref[0, h] = jnp.exp(GL) * st + _dot(v.T, k * jnp.exp(GL - G))
            o_ref[0, rows, h * d:(h + 1) * d] = _head_rmsnorm(o_h, prm(3, h)) * _silu(gate)
        return carry

    lax.fori_loop(0, tt // ch, hgrn_chunk, 0)

    t_i = lax.broadcasted_iota(jnp.int32, (tt, tt), 0)
    s_i = lax.broadcasted_iota(jnp.int32, (tt, tt), 1)
    causal = t_i >= s_i
    rel = jnp.where(causal, (t_i - s_i).astype(F32), 0.0)
    idx_full = lax.broadcasted_iota(jnp.int32, (tt, d), 0).astype(F32)
    cos = cos_ref[...]
    sin = sin_ref[...]

    def rotary(a_):
        return a_ * cos + pltpu.roll(a_, d // 2, 1) * sin

    for h in range(N_HEADS):
        lg = math.log(1.0 - 2.0 ** (-5.0 - h))
        q = rotary(z_ref[0, :, 4 * W + h * d:4 * W + (h + 1) * d])
        k = rotary(z_ref[0, :, 5 * W + h * d:5 * W + (h + 1) * d]) * (d ** -0.5)
        v = z_ref[0, :, 6 * W + h * d:6 * W + (h + 1) * d]
        gate = z_ref[0, :, 7 * W + h * d:7 * W + (h + 1) * d]
        dm = jnp.where(causal, jnp.exp(rel * lg), 0.0)
        A = _dot_nt(q, k) * dm
        S = sr_ref[0, h]
        o_h = _dot(A, v) + jnp.exp((idx_full + 1.0) * lg) * _dot(q, S)
        k_out = jnp.exp((tt - 1.0 - idx_full) * lg)
        sr_ref[0, h] = math.exp(tt * lg) * S + _dot_tn(k * k_out, v)
        o_ref[0, :, W + h * d:W + (h + 1) * d] = _head_layernorm(o_h, prm(4, h)) * _silu(gate)

    u = z_ref[0, :, 8 * W:9 * W]
    ext_ref[SUBLANES - (CONV_W - 1):SUBLANES, :] = cb_ref[0]
    ext_ref[SUBLANES:SUBLANES + tt, :] = u
    conv = jnp.zeros((tt, W), F32)
    for j in range(CONV_W):
        off = SUBLANES - (CONV_W - 1) + j
        conv = conv + convw_ref[j:j + 1, :] * ext_ref[off:off + tt, :]
    cb_ref[0] = ext_ref[SUBLANES + tt - (CONV_W - 1):SUBLANES + tt, :]
    uc = _silu(conv + prm_ref[7:8, :])

    qs, ks, vs = [], [], []
    for h in range(N_HEADS):
        uch = uc[:, h * d:(h + 1) * d].astype(BF16)
        qs.append(jnp.dot(uch, mq_ref[h], preferred_element_type=F32))
        ks.append(jnp.dot(uch, mk_ref[h], preferred_element_type=F32))
        vs.append(jnp.dot(u[:, h * d:(h + 1) * d].astype(BF16), mv_ref[h], preferred_element_type=F32))
    qkv = jnp.concatenate(qs + ks + vs, axis=1)
    g_col = _dot_f32(qkv, wg_ref[...]) + bg_ref[...]
    g_row = lax.dot_general(wgt_ref[...], qkv, (((1,), (1,)), ((), ())),
                            preferred_element_type=F32, precision=HIGHEST) + bgt_ref[...]
    tri_l = causal.astype(F32)
    tri_u = (t_i <= s_i).astype(F32)
    i_cols = g_col[:, :N_HEADS]
    f_cols = _dot_f32(tri_l, _log_sigmoid(g_col[:, N_HEADS:]))
    i_rows = g_row[:N_HEADS]
    f_rows = _dot_f32(_log_sigmoid(g_row[N_HEADS:]), tri_u)
    m_prev_all = m_ref[0]
    for h in range(N_HEADS):
        q, k, v = qs[h], ks[h] * (d ** -0.5), vs[h]
        i_col, F_col = i_cols[:, h:h + 1], f_cols[:, h:h + 1]
        a_row = i_rows[h:h + 1] - f_rows[h:h + 1]
        m_prev = m_prev_all[:, h:h + 1]
        cm = jnp.max(jnp.where(causal, a_row, NEG_INF), -1, keepdims=True)
        m_col = F_col + jnp.maximum(m_prev, cm)
        logd = (F_col - m_col) + a_row
        dm = jnp.where(causal, jnp.exp(jnp.where(causal, logd, 0.0)), 0.0)
        Sc = _dot_nt(q, k) * dm
        inter = jnp.exp(F_col + m_prev - m_col)
        Cst = c_ref[0, h]
        n_row = n_ref[0, h:h + 1, :]
        num = _dot(Sc, v) + inter * _dot(q, Cst)
        den = jnp.sum(Sc, -1, keepdims=True) + inter * jnp.sum(q * n_row, -1, keepdims=True)
        hout = num / jnp.maximum(jnp.abs(den), jnp.exp(-m_col))
        mL = m_col[tt - 1:tt]
        FL = F_col[tt - 1:tt]
        kw = k * jnp.exp(FL - F_col + i_col - mL)
        decay = jnp.exp(FL + m_prev - mL)
        c_ref[0, h] = decay * Cst + _dot_tn(kw, v)
        n_ref[0, h:h + 1, :] = decay * n_row + jnp.sum(kw, 0, keepdims=True)
        m_ref[0, :, h:h + 1] = mL
        y = _head_layernorm(hout, prm(5, h)) + prm(6, h) * uc[:, h * d:(h + 1) * d]
        o_ref[0, :, 2 * W + h * d:2 * W + (h + 1) * d] = y * _silu(z_ref[0, :, 9 * W + h * d:9 * W + (h + 1) * d])

    @pl.when(ti == n_t - 1)
    def _():
        for h in range(N_HEADS):
            sh_ref[0, h] = sh_ref[0, h].T


def _mixers(z, cos, sin, states, prm, convw, mq, mk, mv, wg, wgt, bg, bgt):
    B, T, _ = z.shape
    sh0, sr0, c0, n0, m0, cb0 = states
    tt = _pick(T, (256, 128, 64, 32, 16, 8))
    ch = min(tt, 64)
    n_t = T // tt
    H, d, W = N_HEADS, HEAD_DIM, MIX_WIDTH
    m0 = m0.reshape(B, 1, H)
    st_spec = pl.BlockSpec((1, H, d, d), lambda b, t: (b, 0, 0, 0))
    n_spec = pl.BlockSpec((1, H, d), lambda b, t: (b, 0, 0))
    m_spec = pl.BlockSpec((1, 1, H), lambda b, t: (b, 0, 0))
    cb_spec = pl.BlockSpec((1, CONV_W - 1, W), lambda b, t: (b, 0, 0))

    def full(a):
        nd = a.ndim
        return pl.BlockSpec(a.shape, lambda b, t: (0,) * nd)

    outs = pl.pallas_call(
        functools.partial(_mixer_kernel, tt=tt, ch=ch, n_t=n_t),
        grid=(B, n_t),
        in_specs=[pl.BlockSpec((1, tt, SCAN_COLS), lambda b, t: (b, t, 0)),
                  pl.BlockSpec((tt, d), lambda b, t: (t, 0)),
                  pl.BlockSpec((tt, d), lambda b, t: (t, 0)),
                  st_spec, st_spec, st_spec, n_spec, m_spec, cb_spec,
                  full(prm), full(convw), full(mq), full(mk), full(mv), full(wg), full(wgt), full(bg), full(bgt)],
        out_specs=[pl.BlockSpec((1, tt, N_BRANCH * W), lambda b, t: (b, t, 0)),
                   st_spec, st_spec, st_spec, n_spec, m_spec, cb_spec],
        out_shape=[jax.ShapeDtypeStruct((B, T, N_BRANCH * W), F32),
                   jax.ShapeDtypeStruct((B, H, d, d), F32),
                   jax.ShapeDtypeStruct((B, H, d, d), F32),
                   jax.ShapeDtypeStruct((B, H, d, d), F32),
                   jax.ShapeDtypeStruct((B, H, d), F32),
                   jax.ShapeDtypeStruct((B, 1, H), F32),
                   jax.ShapeDtypeStruct((B, CONV_W - 1, W), F32)],
        scratch_shapes=[pltpu.VMEM((SUBLANES + tt, W), F32)],
        name="mixers",
        compiler_params=_cparams(("parallel", "arbitrary")),
    )(z, cos, sin, sh0, sr0, c0, n0, m0, cb0, prm, convw, mq, mk, mv, wg, wgt, bg, bgt)
    o, sh, sr, c, n, m, cb = outs
    return o, (sh, sr, c, n, m.reshape(B, H), cb)


def _merge_kernel(x_ref, o_ref, nw_ref, wgate_ref, wbr_ref, wout_ref, y_ref):
    x = x_ref[...]
    D = x.shape[1]
    W = MIX_WIDTH
    hn = _rms(x, nw_ref[...]).astype(BF16)
    gz = jnp.dot(hn, wgate_ref[...], preferred_element_type=F32)
    merged = jnp.zeros(x.shape, F32)
    for n in range(N_BRANCH):
        proj = jnp.dot(o_ref[:, n * W:(n + 1) * W].astype(BF16), wbr_ref[n], preferred_element_type=F32)
        merged = merged + _sigmoid(gz[:, n * D:(n + 1) * D]) * proj
    y_ref[...] = x + jnp.dot(merged.astype(BF16), wout_ref[...], preferred_element_type=F32)


def _merge(x, o, nw, wgate, wbr, wout):
    n, d = x.shape
    tm = _pick(n, (512, 256, 128))
    return pl.pallas_call(
        _merge_kernel,
        grid=(n // tm,),
        in_specs=[pl.BlockSpec((tm, d), lambda i: (i, 0)),
                  pl.BlockSpec((tm, N_BRANCH * MIX_WIDTH), lambda i: (i, 0)),
                  pl.BlockSpec((1, d), lambda i: (0, 0)),
                  pl.BlockSpec(wgate.shape, lambda i: (0, 0)),
                  pl.BlockSpec(wbr.shape, lambda i: (0, 0, 0)),
                  pl.BlockSpec(wout.shape, lambda i: (0, 0))],
        out_specs=pl.BlockSpec((tm, d), lambda i: (i, 0)),
        out_shape=jax.ShapeDtypeStruct((n, d), F32),
        name="merge",
        compiler_params=_cparams(("parallel",)),
    )(x, o, nw, wgate, wbr, wout)


def _router_kernel(x_ref, nw_ref, wr_ref, br_ref, hn_ref, mi_ref, mf_ref, cnt_ref, carry_ref):
    i = pl.program_id(0)
    tm = x_ref.shape[0]
    n_sub = x_ref.shape[1] // LANES

    @pl.when(i == 0)
    def _():
        carry_ref[...] = jnp.zeros(carry_ref.shape, F32)

    hn = _rms(x_ref[...], nw_ref[...])
    for s in range(n_sub):
        hn_ref[:, s, :] = hn[:, s * LANES:(s + 1) * LANES]
    logits = _dot_f32(hn, wr_ref[...]) + br_ref[...]
    lane = lax.broadcasted_iota(jnp.int32, logits.shape, 1)
    big = jnp.int32(1 << 20)

    def first_max(mask):
        vmax = jnp.max(jnp.where(mask, logits, NEG_INF), -1, keepdims=True)
        imax = jnp.min(jnp.where(mask, jnp.where(logits == vmax, lane, big), big), -1, keepdims=True)
        return vmax, imax

    in_groups = lane < N_GROUPS
    g_max, g_top = first_max(in_groups)
    g_w = 1.0 / jnp.sum(jnp.where(in_groups, jnp.exp(logits - g_max), 0.0), -1, keepdims=True)
    e_lo = N_GROUPS + EXPERTS_PER_GROUP * g_top
    in_e = jnp.logical_and(lane >= e_lo, lane < e_lo + EXPERTS_PER_GROUP)
    v1, i1 = first_max(in_e)
    v2, i2 = first_max(jnp.logical_and(in_e, lane != i1))
    e2 = jnp.exp(v2 - v1)
    w1 = g_w / (1.0 + e2)
    w2 = g_w * e2 / (1.0 + e2)

    hit1 = lane == i1
    hit2 = lane == i2
    onehot = jnp.where(hit1, 1.0, 0.0) + jnp.where(hit2, 1.0, 0.0)
    strict = (lax.broadcasted_iota(jnp.int32, (tm, tm), 0) > lax.broadcasted_iota(jnp.int32, (tm, tm), 1))
    before = _dot(strict.astype(F32), onehot) + carry_ref[...]
    r1 = jnp.sum(jnp.where(hit1, before, 0.0), -1, keepdims=True).astype(jnp.int32)
    r2 = jnp.sum(jnp.where(hit2, before, 0.0), -1, keepdims=True).astype(jnp.int32)
    carry_ref[...] = carry_ref[...] + jnp.sum(onehot, 0, keepdims=True)
    cnt_ref[...] = carry_ref[...]

    mi_ref[...] = jnp.where(lane == 0, i1 - N_GROUPS,
                            jnp.where(lane == 1, i2 - N_GROUPS,
                                      jnp.where(lane == 2, r1, jnp.where(lane == 3, r2, 0))))
    mf_ref[...] = jnp.where(lane == 0, w1, jnp.where(lane == 1, w2, 0.0))


def _router(x, nw, wr, br):
    n, d = x.shape
    tm = _pick(n, (256, 128))
    return pl.pallas_call(
        _router_kernel,
        grid=(n // tm,),
        in_specs=[pl.BlockSpec((tm, d), lambda i: (i, 0)),
                  pl.BlockSpec((1, d), lambda i: (0, 0)),
                  pl.BlockSpec(wr.shape, lambda i: (0, 0)),
                  pl.BlockSpec((1, LANES), lambda i: (0, 0))],
        out_specs=[pl.BlockSpec((tm, d // LANES, LANES), lambda i: (i, 0, 0)),
                   pl.BlockSpec((tm, LANES), lambda i: (i, 0)),
                   pl.BlockSpec((tm, LANES), lambda i: (i, 0)),
                   pl.BlockSpec((1, LANES), lambda i: (0, 0))],
        out_shape=[jax.ShapeDtypeStruct((n, d // LANES, LANES), F32),
                   jax.ShapeDtypeStruct((n, LANES), jnp.int32),
                   jax.ShapeDtypeStruct((n, LANES), F32),
                   jax.ShapeDtypeStruct((1, LANES), F32)],
        scratch_shapes=[pltpu.VMEM((1, LANES), F32)],
        name="router",
        compiler_params=_cparams(("arbitrary",)),
    )(x, nw, wr, br)


DISPATCH_WINDOW = 16


def _dispatch_kernel(d0_ref, d1_ref, hn_ref, xs_in_ref, xs_ref, sem):
    del xs_in_ref
    n = hn_ref.shape[0]

    def copies(t):
        return (pltpu.make_async_copy(hn_ref.at[t], xs_ref.at[d0_ref[t]], sem),
                pltpu.make_async_copy(hn_ref.at[t], xs_ref.at[d1_ref[t]], sem))

    def body(t, carry):
        @pl.when(t >= DISPATCH_WINDOW)
        def _():
            for cp in copies(t - DISPATCH_WINDOW):
                cp.wait()
        for cp in copies(t):
            cp.start()
        return carry

    lax.fori_loop(0, n, body, 0)

    def drain(t, carry):
        for cp in copies(t):
            cp.wait()
        return carry

    lax.fori_loop(max(n - DISPATCH_WINDOW, 0), n, drain, 0)


def _dispatch(d0, d1, hn3, cap):
    n, s, l = hn3.shape
    xs0 = jnp.zeros((cap, s, l), F32)
    return pl.pallas_call(
        _dispatch_kernel,
        grid_spec=pltpu.PrefetchScalarGridSpec(
            num_scalar_prefetch=2,
            grid=(1,),
            in_specs=[pl.BlockSpec(memory_space=pl.ANY), pl.BlockSpec(memory_space=pl.ANY)],
            out_specs=pl.BlockSpec(memory_space=pl.ANY),
            scratch_shapes=[pltpu.SemaphoreType.DMA(())]),
        out_shape=jax.ShapeDtypeStruct((cap, s, l), F32),
        input_output_aliases={3: 0},
        name="dispatch",
        compiler_params=pltpu.CompilerParams(dimension_semantics=("arbitrary",), has_side_effects=True),
    )(d0, d1, hn3, xs0)


def _experts_kernel(be_ref, nu_ref, xs_ref, wg_ref, wu_ref, wd_ref, ys_ref, wg_bf, wu_bf, wd_bf):
    i = pl.program_id(0)
    n_sub = xs_ref.shape[1]

    @pl.when(i < nu_ref[0])
    def _():
        changed = jnp.logical_or(i == 0, be_ref[i] != be_ref[jnp.maximum(i - 1, 0)])

        @pl.when(changed)
        def _():
            wg_bf[...] = wg_ref[0].astype(BF16)
            wu_bf[...] = wu_ref[0].astype(BF16)
            wd_bf[...] = wd_ref[0].astype(BF16)

        x = jnp.concatenate([xs_ref[:, s, :] for s in range(n_sub)], axis=1).astype(BF16)
        g = jnp.dot(x, wg_bf[...], preferred_element_type=F32)
        u = jnp.dot(x, wu_bf[...], preferred_element_type=F32)
        y = jnp.dot((_silu(g) * u).astype(BF16), wd_bf[...], preferred_element_type=F32)
        for s in range(n_sub):
            ys_ref[:, s, :] = y[:, s * LANES:(s + 1) * LANES]

    @pl.when(i >= nu_ref[0])
    def _():
        ys_ref[...] = jnp.zeros(ys_ref.shape, F32)


def _experts(block_e, n_used, xs, w_g, w_u, w_d):
    cap, s, l = xs.shape
    _, d, f = w_g.shape
    n_blocks = cap // MOE_ROWS
    return pl.pallas_call(
        _experts_kernel,
        grid_spec=pltpu.PrefetchScalarGridSpec(
            num_scalar_prefetch=2,
            grid=(n_blocks,),
            in_specs=[pl.BlockSpec((MOE_ROWS, s, l), lambda i, be, nu: (i, 0, 0)),
                      pl.BlockSpec((1, d, f), lambda i, be, nu: (be[i], 0, 0)),
                      pl.BlockSpec((1, d, f), lambda i, be, nu: (be[i], 0, 0)),
                      pl.BlockSpec((1, f, d), lambda i, be, nu: (be[i], 0, 0))],
            out_specs=pl.BlockSpec((MOE_ROWS, s, l), lambda i, be, nu: (i, 0, 0)),
            scratch_shapes=[pltpu.VMEM((d, f), BF16), pltpu.VMEM((d, f), BF16), pltpu.VMEM((f, d), BF16)]),
        out_shape=jax.ShapeDtypeStruct((cap, s, l), F32),
        name="experts",
        compiler_params=_cparams(("arbitrary",)),
    )(block_e, n_used, xs, w_g, w_u, w_d)


def _combine_kernel(d0_ref, d1_ref, x_ref, mf_ref, nw_ref, ys_ref, y_ref, buf, sem, *, final_norm):
    i = pl.program_id(0)
    tm = x_ref.shape[0]
    n_sub = buf.shape[2]
    base = i * tm

    def copies(r):
        return (pltpu.make_async_copy(ys_ref.at[d0_ref[base + r]], buf.at[0, r], sem),
                pltpu.make_async_copy(ys_ref.at[d1_ref[base + r]], buf.at[1, r], sem))

    def start(r, carry):
        for cp in copies(r):
            cp.start()
        return carry

    def wait(r, carry):
        for cp in copies(r):
            cp.wait()
        return carry

    lax.fori_loop(0, tm, start, 0)
    lax.fori_loop(0, tm, wait, 0)

    w = mf_ref[...]
    y0 = jnp.concatenate([buf[0, :, s, :] for s in range(n_sub)], axis=1)
    y1 = jnp.concatenate([buf[1, :, s, :] for s in range(n_sub)], axis=1)
    out = x_ref[...] + (y0 * w[:, 0:1] + y1 * w[:, 1:2])
    if final_norm:
        out = _rms(out, nw_ref[...])
    y_ref[...] = out


def _combine(d0, d1, x, mf, nw, ys, final_norm):
    n, d = x.shape
    _, s, l = ys.shape
    tm = _pick(n, (256, 128))
    return pl.pallas_call(
        functools.partial(_combine_kernel, final_norm=final_norm),
        grid_spec=pltpu.PrefetchScalarGridSpec(
            num_scalar_prefetch=2,
            grid=(n // tm,),
            in_specs=[pl.BlockSpec((tm, d), lambda i, a, b: (i, 0)),
                      pl.BlockSpec((tm, LANES), lambda i, a, b: (i, 0)),
                      pl.BlockSpec((1, d), lambda i, a, b: (0, 0)),
                      pl.BlockSpec(memory_space=pl.ANY)],
            out_specs=pl.BlockSpec((tm, d), lambda i, a, b: (i, 0)),
            scratch_shapes=[pltpu.VMEM((TOP_K, tm, s, l), F32), pltpu.SemaphoreType.DMA(())]),
        out_shape=jax.ShapeDtypeStruct((n, d), F32),
        name="combine",
        compiler_params=_cparams(("arbitrary",)),
    )(d0, d1, x, mf, nw, ys)


def _moe(x, l, p, final_nw):
    n, d = x.shape
    hn3, mi, mf, cnt = _router(x, p['norm_ffn'][l][None], p['w_r'][l], p['b_r'][l])
    counts = cnt[0, N_GROUPS:N_GROUPS + N_EXPERTS].astype(jnp.int32)
    padded = (counts + MOE_ROWS - 1) // MOE_ROWS * MOE_ROWS
    pad_end = jnp.cumsum(padded)
    pad_start = pad_end - padded
    d0 = pad_start[mi[:, 0]] + mi[:, 2]
    d1 = pad_start[mi[:, 1]] + mi[:, 3]
    n_blocks = -(-(n * TOP_K) // MOE_ROWS) + N_EXPERTS
    block_e = jnp.minimum(jnp.searchsorted(pad_end, jnp.arange(n_blocks, dtype=jnp.int32) * MOE_ROWS, side='right'),
                          N_EXPERTS - 1).astype(jnp.int32)
    n_used = (pad_end[-1:] // MOE_ROWS).astype(jnp.int32)
    xs = _dispatch(d0, d1, hn3, n_blocks * MOE_ROWS)
    ys = _experts(block_e, n_used, xs, p['w_eg'][l], p['w_eu'][l], p['w_ed'][l])
    nw = (final_nw if final_nw is not None else p['norm_ffn'][l])[None]
    return _combine(d0, d1, x, mf, nw, ys, final_nw is not None)


def _trunk(x, pos, states, p):
    B, T, D = x.shape
    depth = p['w_in_scan'].shape[0]
    half = HEAD_DIM // 2
    inv = ROPE_BASE ** (-jnp.arange(half, dtype=F32) / half)
    ang = pos.astype(F32)[:, None] * inv[None, :]
    cos = jnp.concatenate([jnp.cos(ang), jnp.cos(ang)], -1)
    sin = jnp.concatenate([-jnp.sin(ang), jnp.sin(ang)], -1)
    xf = x.reshape(B * T, D)
    new_states = []
    for l in range(depth):
        z = _inproj(xf, p['norm_mix'][l][None], p['w_in_scan'][l])
        o, st = _mixers(z.reshape(B, T, SCAN_COLS), cos, sin, tuple(s[l] for s in states), p['prm'][l],
                        p['conv_w'][l], p['mq'][l], p['mk'][l], p['mv'][l],
                        p['m_wg'][l], p['m_wg'][l].T, p['m_bg'][l][None], p['m_bg'][l][:, None])
        new_states.append(st)
        xf = _merge(xf, o.reshape(B * T, N_BRANCH * MIX_WIDTH), p['norm_mix'][l][None],
                    p['w_in_gate'][l], p['w_branch'][l], p['w_out'][l])
        xf = _moe(xf, l, p, p['norm_final'] if l == depth - 1 else None)
    return xf.reshape(B, T, D), tuple(jnp.stack([s[i] for s in new_states]) for i in range(6))


def kernel(x_prompt, x_sample, state_hgrn, state_ret, state_mlstm_C, state_mlstm_n, state_mlstm_m, state_mlstm_conv,
           norm_mix, norm_ffn, norm_final, w_in, hgrn_lb, hgrn_norm, ret_norm, mlstm_conv_w, mlstm_conv_b,
           mlstm_wq, mlstm_wk, mlstm_wv, mlstm_w_gates, mlstm_b_gates, mlstm_norm, mlstm_skip, w_branch, w_out,
           w_router_group, b_router_group, w_router_expert, b_router_expert, w_exp_gate, w_exp_up, w_exp_down):
    depth, D = norm_mix.shape
    H, d, W = N_HEADS, HEAD_DIM, MIX_WIDTH
    lb = jnp.cumsum(jax.nn.softmax(hgrn_lb.astype(F32), axis=0), axis=0)
    lb = lb - lb[0:1]
    prm = jnp.stack([jnp.log(lb), jnp.log1p(-lb), 1.0 - lb, hgrn_norm.astype(F32), ret_norm.astype(F32),
                     mlstm_norm.astype(F32), mlstm_skip.astype(F32), mlstm_conv_b.astype(F32)], axis=1)
    pad = LANES - N_GROUPS - N_EXPERTS
    w_r = jnp.concatenate([w_router_group, w_router_expert, jnp.zeros((depth, D, pad), F32)], -1)
    b_r = jnp.concatenate([b_router_group, b_router_expert, jnp.zeros((depth, pad), F32)], -1)[:, None, :]
    p = {'norm_mix': norm_mix, 'norm_ffn': norm_ffn, 'norm_final': norm_final,
         'w_in_scan': w_in[:, :, :SCAN_COLS].astype(BF16), 'w_in_gate': w_in[:, :, SCAN_COLS:].astype(BF16),
         'prm': prm, 'conv_w': mlstm_conv_w.astype(F32),
         'mq': mlstm_wq.astype(BF16), 'mk': mlstm_wk.astype(BF16), 'mv': mlstm_wv.astype(BF16),
         'm_wg': mlstm_w_gates, 'm_bg': mlstm_b_gates,
         'w_branch': w_branch.astype(BF16), 'w_out': w_out.astype(BF16),
         'w_r': w_r, 'b_r': b_r, 'w_eg': w_exp_gate, 'w_eu': w_exp_up, 'w_ed': w_exp_down}

    Bp, Tp = x_prompt.shape[0], x_prompt.shape[1]
    zero_states = (jnp.zeros((depth, Bp, H, d, d), F32), jnp.zeros((depth, Bp, H, d, d), F32),
                   jnp.zeros((depth, Bp, H, d, d), F32), jnp.zeros((depth, Bp, H, d), F32),
                   jnp.zeros((depth, Bp, H), F32), jnp.zeros((depth, Bp, CONV_W - 1, W), F32))
    pos_prompt = jnp.arange(Tp, dtype=jnp.int32)
    pos_sample = PAST_LEN + jnp.arange(x_sample.shape[1], dtype=jnp.int32)
    y_prompt, ps = _trunk(x_prompt, pos_prompt, zero_states, p)
    sample_states = (state_hgrn, state_ret, state_mlstm_C, state_mlstm_n, state_mlstm_m, state_mlstm_conv)
    y_sample, ss = _trunk(x_sample, pos_sample, sample_states, p)
    return (y_prompt, y_sample) + ps + ss
```

```python
import functools
import math

import jax
import jax.numpy as jnp
from jax import lax
from jax.experimental import pallas as pl
from jax.experimental.pallas import tpu as pltpu

F32 = jnp.float32
BF16 = jnp.bfloat16
HIGHEST = lax.Precision.HIGHEST

HEAD_DIM = 128
N_HEADS = 4
MIX_WIDTH = HEAD_DIM * N_HEADS
N_BRANCH = 3
CONV_W = 4
ROPE_BASE = 10000.0
N_GROUPS = 4
EXPERTS_PER_GROUP = 8
N_EXPERTS = N_GROUPS * EXPERTS_PER_GROUP
TOP_K = 2
PAST_LEN = 16384
NORM_EPS = 1e-6
HEAD_NORM_EPS = 1e-5
SCAN_COLS = 10 * MIX_WIDTH

LANES = 128
SUBLANES = 8
VMEM_LIMIT = 48 * 1024 * 1024
MOE_ROWS = 128
HGRN_DIAG = 8
NEG_INF = float("-inf")


def _cparams(sem):
    return pltpu.CompilerParams(dimension_semantics=sem, vmem_limit_bytes=VMEM_LIMIT)


def _pick(n, cands):
    for c in cands:
        if n % c == 0:
            return c
    return n


def _rms(x, w):
    return x * lax.rsqrt(jnp.mean(x * x, -1, keepdims=True) + NORM_EPS) * w


def _sigmoid(x):
    return 1.0 / (1.0 + jnp.exp(-x))


def _silu(x):
    return x * _sigmoid(x)


def _log_sigmoid(x):
    return jnp.minimum(x, 0.0) - jnp.log1p(jnp.exp(-jnp.abs(x)))


def _dot(a, b):
    return jnp.dot(a.astype(BF16), b.astype(BF16), preferred_element_type=F32)


def _dot_nt(a, b):
    return lax.dot_general(a.astype(BF16), b.astype(BF16), (((1,), (1,)), ((), ())), preferred_element_type=F32)


def _dot_tn(a, b):
    return jnp.dot(a.T.astype(BF16), b.astype(BF16), preferred_element_type=F32)


def _split3(x):
    hi = x.astype(BF16)
    r = x - hi.astype(F32)
    mid = r.astype(BF16)
    lo = (r - mid.astype(F32)).astype(BF16)
    return hi, mid, lo


def _mask_dot(mask_bf, x):
    return sum(jnp.dot(mask_bf, part, preferred_element_type=F32) for part in _split3(x))


def _dot_mask(x, mask_bf):
    return sum(jnp.dot(part, mask_bf, preferred_element_type=F32) for part in _split3(x))


def _inproj_kernel(x_ref, nw_ref, w_ref, z_ref):
    h = _rms(x_ref[...], nw_ref[...])
    z_ref[...] = jnp.dot(h.astype(BF16), w_ref[...], preferred_element_type=F32)


def _inproj(x, nw, w_in, l):
    n, d = x.shape
    cols = SCAN_COLS
    tm = _pick(n, (512, 256, 128))
    tn = _pick(cols, (1280, 1024, 512, 256, 128))
    return pl.pallas_call(
        _inproj_kernel,
        grid=(cols // tn, n // tm),
        in_specs=[pl.BlockSpec((tm, d), lambda j, i: (i, 0)),
                  pl.BlockSpec((None, 1, d), lambda j, i: (l, 0, 0)),
                  pl.BlockSpec((None, d, tn), lambda j, i: (l, 0, j))],
        out_specs=pl.BlockSpec((tm, tn), lambda j, i: (i, j)),
        out_shape=jax.ShapeDtypeStruct((n, cols), F32),
        name="inproj",
        compiler_params=_cparams(("parallel", "parallel")),
    )(x, nw, w_in)


def _head_rmsnorm(o, w):
    return o * lax.rsqrt(jnp.mean(o * o, -1, keepdims=True) + HEAD_NORM_EPS) * w


def _head_layernorm(o, w):
    c = o - jnp.mean(o, -1, keepdims=True)
    return c * lax.rsqrt(jnp.mean(c * c, -1, keepdims=True) + HEAD_NORM_EPS) * w


def _mixer_kernel(*refs, tt, ch, n_t, n_alias):
    (z_ref, cos_ref, sin_ref, sh0_ref, sr0_ref, c0_ref, n0_ref, m0_ref, cb0_ref,
     prm_ref, convw_ref, mq_ref, mk_ref, mv_ref, wg_ref, wgt_ref, bg_ref, bgt_ref) = refs[:18]
    o_ref, sh_ref, sr_ref, c_ref, n_ref, m_ref, cb_ref, ext_ref = refs[18 + n_alias:]
    ti = pl.program_id(1)
    d = HEAD_DIM
    W = MIX_WIDTH

    @pl.when(ti == 0)
    def _():
        for h in range(N_HEADS):
            sh_ref[h] = sh0_ref[h].T
        sr_ref[...] = sr0_ref[...]
        c_ref[...] = c0_ref[...]
        n_ref[...] = n0_ref[...]
        m_ref[...] = m0_ref[...]
        cb_ref[...] = cb0_ref[...]

    def prm(i, h):
        return prm_ref[i:i + 1, h * d:(h + 1) * d]

    n_blk = ch // HGRN_DIAG
    tri_ch = (lax.broadcasted_iota(jnp.int32, (ch, ch), 0) >= lax.broadcasted_iota(jnp.int32, (ch, ch), 1)).astype(BF16)
    row_id = lax.broadcasted_iota(jnp.int32, (HGRN_DIAG, d), 0)

    def hgrn_chunk(c, carry):
        r0 = pl.multiple_of(c * ch, ch)
        rows = pl.ds(r0, ch)
        for h in range(N_HEADS):
            q = z_ref[rows, h * d:(h + 1) * d]
            fpre = z_ref[rows, W + h * d:W + (h + 1) * d]
            v = z_ref[rows, 2 * W + h * d:2 * W + (h + 1) * d]
            gate = z_ref[rows, 3 * W + h * d:3 * W + (h + 1) * d]
            a = prm(0, h)
            b = prm(1, h) + _log_sigmoid(fpre)
            logf = jnp.maximum(a, b) + jnp.log1p(jnp.exp(-jnp.abs(a - b)))
            k = prm(2, h) * _sigmoid(-fpre)
            G = _mask_dot(tri_ch, logf)
            st = sh_ref[h]
            o_inter = _dot_nt(q * jnp.exp(G), st)
            outs = []
            for blk in range(n_blk):
                lo = blk * HGRN_DIAG
                hi = lo + HGRN_DIAG
                Gi, qi, ki, vi = G[lo:hi], q[lo:hi], k[lo:hi], v[lo:hi]
                acc = o_inter[lo:hi]
                if blk > 0:
                    gi = G[lo - 1:lo]
                    qh = qi * jnp.exp(Gi - gi)
                    kt = k[:lo] * jnp.exp(gi - G[:lo])
                    acc = acc + _dot(_dot_nt(qh, kt), v[:lo])
                for s in range(HGRN_DIAG):
                    keep = row_id >= s
                    dec = jnp.where(keep, jnp.exp(jnp.where(keep, Gi - Gi[s:s + 1], 0.0)), 0.0)
                    a_s = jnp.sum(qi * dec * ki[s:s + 1], -1, keepdims=True)
                    acc = acc + a_s * vi[s:s + 1]
                outs.append(acc)
            o_h = jnp.concatenate(outs, axis=0) if n_blk > 1 else outs[0]
            GL = G[ch - 1:ch]
            sh_ref[h] = jnp.exp(GL) * st + _dot(v.T, k * jnp.exp(GL - G))
            o_ref[rows, h * d:(h + 1) * d] = _head_rmsnorm(o_h, prm(3, h)) * _silu(gate)
        return carry

    lax.fori_loop(0, tt // ch, hgrn_chunk, 0)

    t_i = lax.broadcasted_iota(jnp.int32, (tt, tt), 0)
    s_i = lax.broadcasted_iota(jnp.int32, (tt, tt), 1)
    causal = t_i >= s_i
    rel = jnp.where(causal, (t_i - s_i).astype(F32), 0.0)
    idx_full = lax.broadcasted_iota(jnp.int32, (tt, d), 0).astype(F32)
    cos = cos_ref[...]
    sin = sin_ref[...]

    def rotary(a_):
        return a_ * cos + pltpu.roll(a_, d // 2, 1) * sin

    for h in range(N_HEADS):
        lg = math.log(1.0 - 2.0 ** (-5.0 - h))
        q = rotary(z_ref[:, 4 * W + h * d:4 * W + (h + 1) * d])
        k = rotary(z_ref[:, 5 * W + h * d:5 * W + (h + 1) * d]) * (d ** -0.5)
        v = z_ref[:, 6 * W + h * d:6 * W + (h + 1) * d]
        gate = z_ref[:, 7 * W + h * d:7 * W + (h + 1) * d]
        dm = jnp.where(causal, jnp.exp(rel * lg), 0.0)
        A = _dot_nt(q, k) * dm
        S = sr_ref[h]
        o_h = _dot(A, v) + jnp.exp((idx_full + 1.0) * lg) * _dot(q, S)
        k_out = jnp.exp((tt - 1.0 - idx_full) * lg)
        sr_ref[h] = math.exp(tt * lg) * S + _dot_tn(k * k_out, v)
        o_ref[:, W + h * d:W + (h + 1) * d] = _head_layernorm(o_h, prm(4, h)) * _silu(gate)

    u = z_ref[:, 8 * W:9 * W]
    ext_ref[SUBLANES - (CONV_W - 1):SUBLANES, :] = cb_ref[...]
    ext_ref[SUBLANES:SUBLANES + tt, :] = u
    conv = jnp.zeros((tt, W), F32)
    for j in range(CONV_W):
        off = SUBLANES - (CONV_W - 1) + j
        conv = conv + convw_ref[j:j + 1, :] * ext_ref[off:off + tt, :]
    cb_ref[...] = ext_ref[SUBLANES + tt - (CONV_W - 1):SUBLANES + tt, :]
    uc = _silu(conv + prm_ref[7:8, :])

    qs, ks, vs = [], [], []
    for h in range(N_HEADS):
        uch = uc[:, h * d:(h + 1) * d].astype(BF16)
        qs.append(jnp.dot(uch, mq_ref[h], preferred_element_type=F32))
        ks.append(jnp.dot(uch, mk_ref[h], preferred_element_type=F32))
        vs.append(jnp.dot(u[:, h * d:(h + 1) * d].astype(BF16), mv_ref[h], preferred_element_type=F32))
    qkv = jnp.concatenate(qs + ks + vs, axis=1).astype(BF16)
    g_col = jnp.dot(qkv, wg_ref[...], preferred_element_type=F32) + bg_ref[...]
    g_row = lax.dot_general(wgt_ref[...], qkv, (((1,), (1,)), ((), ())),
                            preferred_element_type=F32) + bgt_ref[...]
    tri_l = causal.astype(BF16)
    tri_u = (t_i <= s_i).astype(BF16)
    i_cols = g_col[:, :N_HEADS]
    f_cols = _mask_dot(tri_l, _log_sigmoid(g_col[:, N_HEADS:]))
    i_rows = g_row[:N_HEADS]
    f_rows = _dot_mask(_log_sigmoid(g_row[N_HEADS:]), tri_u)
    m_prev_all = m_ref[...]
    for h in range(N_HEADS):
        q, k, v = qs[h], ks[h] * (d ** -0.5), vs[h]
        i_col, F_col = i_cols[:, h:h + 1], f_cols[:, h:h + 1]
        a_row = i_rows[h:h + 1] - f_rows[h:h + 1]
        m_prev = m_prev_all[:, h:h + 1]
        cm = jnp.max(jnp.where(causal, a_row, NEG_INF), -1, keepdims=True)
        m_col = F_col + jnp.maximum(m_prev, cm)
        logd = (F_col - m_col) + a_row
        dm = jnp.where(causal, jnp.exp(jnp.where(causal, logd, 0.0)), 0.0)
        Sc = _dot_nt(q, k) * dm
        inter = jnp.exp(F_col + m_prev - m_col)
        Cst = c_ref[h]
        n_row = n_ref[h:h + 1, :]
        num = _dot(Sc, v) + inter * _dot(q, Cst)
        den = jnp.sum(Sc, -1, keepdims=True) + inter * jnp.sum(q * n_row, -1, keepdims=True)
        hout = num / jnp.maximum(jnp.abs(den), jnp.exp(-m_col))
        mL = m_col[tt - 1:tt]
        FL = F_col[tt - 1:tt]
        kw = k * jnp.exp(FL - F_col + i_col - mL)
        decay = jnp.exp(FL + m_prev - mL)
        c_ref[h] = decay * Cst + _dot_tn(kw, v)
        n_ref[h:h + 1, :] = decay * n_row + jnp.sum(kw, 0, keepdims=True)
        m_ref[:, h:h + 1] = mL
        y = _head_layernorm(hout, prm(5, h)) + prm(6, h) * uc[:, h * d:(h + 1) * d]
        o_ref[:, 2 * W + h * d:2 * W + (h + 1) * d] = y * _silu(z_ref[:, 9 * W + h * d:9 * W + (h + 1) * d])

    @pl.when(ti == n_t - 1)
    def _():
        for h in range(N_HEADS):
            sh_ref[h] = sh_ref[h].T


def _mixers(z, cos, sin, states_in, l, prev_out, p):
    B, T, _ = z.shape
    depth = states_in[0].shape[0]
    tt = _pick(T, (256, 128, 64, 32, 16, 8))
    ch = min(tt, 64)
    n_t = T // tt
    H, d, W = N_HEADS, HEAD_DIM, MIX_WIDTH
    st_spec = pl.BlockSpec((None, None, H, d, d), lambda b, t: (l, b, 0, 0, 0))
    n_spec = pl.BlockSpec((None, None, H, d), lambda b, t: (l, b, 0, 0))
    m_spec = pl.BlockSpec((None, None, 1, H), lambda b, t: (l, b, 0, 0))
    cb_spec = pl.BlockSpec((None, None, CONV_W - 1, W), lambda b, t: (l, b, 0, 0))
    state_specs = [st_spec, st_spec, st_spec, n_spec, m_spec, cb_spec]

    def layer(a):
        nd = a.ndim - 1
        return pl.BlockSpec((None,) + a.shape[1:], lambda b, t: (l,) + (0,) * nd)

    weights = [p['prm'], p['conv_w'], p['mq'], p['mk'], p['mv'], p['m_wg'], p['m_wgt'], p['m_bg'], p['m_bgt']]
    alias_in = list(prev_out) if prev_out is not None else []
    n_fixed = 3 + 6 + len(weights)
    state_shapes = [jax.ShapeDtypeStruct((depth, B, H, d, d), F32)] * 3 + [
        jax.ShapeDtypeStruct((depth, B, H, d), F32),
        jax.ShapeDtypeStruct((depth, B, 1, H), F32),
        jax.ShapeDtypeStruct((depth, B, CONV_W - 1, W), F32)]
    outs = pl.pallas_call(
        functools.partial(_mixer_kernel, tt=tt, ch=ch, n_t=n_t, n_alias=len(alias_in)),
        grid=(B, n_t),
        in_specs=[pl.BlockSpec((None, tt, SCAN_COLS), lambda b, t: (b, t, 0)),
                  pl.BlockSpec((tt, d), lambda b, t: (t, 0)),
                  pl.BlockSpec((tt, d), lambda b, t: (t, 0))]
                 + state_specs + [layer(w) for w in weights]
                 + [pl.BlockSpec(memory_space=pl.ANY)] * len(alias_in),
        out_specs=[pl.BlockSpec((None, tt, N_BRANCH * W), lambda b, t: (b, t, 0))] + state_specs,
        out_shape=[jax.ShapeDtypeStruct((B, T, N_BRANCH * W), F32)] + state_shapes,
        input_output_aliases={n_fixed + i: 1 + i for i in range(len(alias_in))},
        scratch_shapes=[pltpu.VMEM((SUBLANES + tt, W), F32)],
        name="mixers",
        compiler_params=_cparams(("parallel", "arbitrary")),
    )(z, cos, sin, *states_in, *weights, *alias_in)
    return outs[0], tuple(outs[1:])


def _merge_kernel(x_ref, o_ref, nw_ref, wg0_ref, wg1_ref, wg2_ref, wbr_ref, wout_ref, y_ref):
    x = x_ref[...]
    W = MIX_WIDTH
    hn = _rms(x, nw_ref[...]).astype(BF16)
    merged = jnp.zeros(x.shape, F32)
    for n, wg_ref in enumerate((wg0_ref, wg1_ref, wg2_ref)):
        gz = jnp.dot(hn, wg_ref[...], preferred_element_type=F32)
        proj = jnp.dot(o_ref[:, n * W:(n + 1) * W].astype(BF16), wbr_ref[n], preferred_element_type=F32)
        merged = merged + _sigmoid(gz) * proj
    y_ref[...] = x + jnp.dot(merged.astype(BF16), wout_ref[...], preferred_element_type=F32)


def _merge(x, o, nw, w_in, wbr, wout, l):
    n, d = x.shape
    tm = _pick(n, (512, 256, 128))
    g0 = SCAN_COLS // d

    def gate_spec(k):
        return pl.BlockSpec((None, d, d), lambda i: (l, 0, g0 + k))

    return pl.pallas_call(
        _merge_kernel,
        grid=(n // tm,),
        in_specs=[pl.BlockSpec((tm, d), lambda i: (i, 0)),
                  pl.BlockSpec((tm, N_BRANCH * MIX_WIDTH), lambda i: (i, 0)),
                  pl.BlockSpec((None, 1, d), lambda i: (l, 0, 0)),
                  gate_spec(0), gate_spec(1), gate_spec(2),
                  pl.BlockSpec((None,) + wbr.shape[1:], lambda i: (l, 0, 0, 0)),
                  pl.BlockSpec((None,) + wout.shape[1:], lambda i: (l, 0, 0))],
        out_specs=pl.BlockSpec((tm, d), lambda i: (i, 0)),
        out_shape=jax.ShapeDtypeStruct((n, d), F32),
        name="merge",
        compiler_params=_cparams(("parallel",)),
    )(x, o, nw, w_in, w_in, w_in, wbr, wout)


def _router_kernel(x_ref, nw_ref, wr_ref, br_ref, mi_ref, mf_ref, cnt_ref, carry_ref):
    i = pl.program_id(0)
    tm = x_ref.shape[0]

    @pl.when(i == 0)
    def _():
        carry_ref[...] = jnp.zeros(carry_ref.shape, F32)

    hn = _rms(x_ref[...], nw_ref[...])
    logits = jnp.dot(hn, wr_ref[...], preferred_element_type=F32, precision=HIGHEST) + br_ref[...]
    lane = lax.broadcasted_iota(jnp.int32, logits.shape, 1)
    big = jnp.int32(1 << 20)

    def first_max(mask):
        vmax = jnp.max(jnp.where(mask, logits, NEG_INF), -1, keepdims=True)
        imax = jnp.min(jnp.where(mask, jnp.where(logits == vmax, lane, big), big), -1, keepdims=True)
        return vmax, imax

    in_groups = lane < N_GROUPS
    g_max, g_top = first_max(in_groups)
    g_w = 1.0 / jnp.sum(jnp.where(in_groups, jnp.exp(logits - g_max), 0.0), -1, keepdims=True)
    e_lo = N_GROUPS + EXPERTS_PER_GROUP * g_top
    in_e = jnp.logical_and(lane >= e_lo, lane < e_lo + EXPERTS_PER_GROUP)
    v1, i1 = first_max(in_e)
    v2, i2 = first_max(jnp.logical_and(in_e, lane != i1))
    e2 = jnp.exp(v2 - v1)
    w1 = g_w / (1.0 + e2)
    w2 = g_w * e2 / (1.0 + e2)

    hit1 = lane == i1
    hit2 = lane == i2
    onehot = jnp.where(hit1, 1.0, 0.0) + jnp.where(hit2, 1.0, 0.0)
    strict = (lax.broadcasted_iota(jnp.int32, (tm, tm), 0) > lax.broadcasted_iota(jnp.int32, (tm, tm), 1))
    before = _dot(strict.astype(F32), onehot) + carry_ref[...]
    r1 = jnp.sum(jnp.where(hit1, before, 0.0), -1, keepdims=True).astype(jnp.int32)
    r2 = jnp.sum(jnp.where(hit2, before, 0.0), -1, keepdims=True).astype(jnp.int32)
    carry_ref[...] = carry_ref[...] + jnp.sum(onehot, 0, keepdims=True)
    cnt_ref[...] = carry_ref[...]

    meta = jnp.where(lane == 0, i1 - N_GROUPS,
                     jnp.where(lane == 1, i2 - N_GROUPS,
                               jnp.where(lane == 2, r1, jnp.where(lane == 3, r2, 0))))
    mi_ref[...] = meta.T[:SUBLANES]
    mf_ref[...] = jnp.where(lane == 0, w1, jnp.where(lane == 1, w2, 0.0))


def _router(x, nw, wr, br, l):
    n, d = x.shape
    tm = _pick(n, (256, 128))
    return pl.pallas_call(
        _router_kernel,
        grid=(n // tm,),
        in_specs=[pl.BlockSpec((tm, d), lambda i: (i, 0)),
                  pl.BlockSpec((None, 1, d), lambda i: (l, 0, 0)),
                  pl.BlockSpec((None, d, LANES), lambda i: (l, 0, 0)),
                  pl.BlockSpec((None, 1, LANES), lambda i: (l, 0, 0))],
        out_specs=[pl.BlockSpec((SUBLANES, tm), lambda i: (0, i)),
                   pl.BlockSpec((tm, LANES), lambda i: (i, 0)),
                   pl.BlockSpec((1, LANES), lambda i: (0, 0))],
        out_shape=[jax.ShapeDtypeStruct((SUBLANES, n), jnp.int32),
                   jax.ShapeDtypeStruct((n, LANES), F32),
                   jax.ShapeDtypeStruct((1, LANES), F32)],
        scratch_shapes=[pltpu.VMEM((1, LANES), F32)],
        name="router",
        compiler_params=_cparams(("arbitrary",)),
    )(x, nw, wr, br)


def _dispatch_kernel(d0_ref, d1_ref, x_ref, nw_ref, xs_in_ref, xs_ref, buf, sem):
    del xs_in_ref
    i = pl.program_id(0)
    n_i = pl.num_programs(0)
    tm = x_ref.shape[0]
    n_sub = x_ref.shape[1] // LANES
    slot = i % 2

    def copies(step, sl, r):
        t = step * tm + r
        src = buf.at[sl, pl.ds(pl.multiple_of(r * n_sub, n_sub), n_sub)]
        return (pltpu.make_async_copy(src, xs_ref.at[d0_ref[t]], sem.at[sl]),
                pltpu.make_async_copy(src, xs_ref.at[d1_ref[t]], sem.at[sl]))

    def wait_step(step, sl):
        def body(r, carry):
            for cp in copies(step, sl, r):
                cp.wait()
            return carry
        lax.fori_loop(0, tm, body, 0)

    @pl.when(i >= 2)
    def _():
        wait_step(i - 2, slot)

    hn = _rms(x_ref[...], nw_ref[...])
    for s in range(n_sub):
        buf[slot, pl.ds(s, tm, stride=n_sub), :] = hn[:, s * LANES:(s + 1) * LANES]

    def start(r, carry):
        for cp in copies(i, slot, r):
            cp.start()
        return carry

    lax.fori_loop(0, tm, start, 0)

    @pl.when(i == n_i - 1)
    def _():
        @pl.when(i >= 1)
        def _():
            wait_step(i - 1, 1 - slot)
        wait_step(i, slot)


def _dispatch(d0, d1, x, nw, l, cap):
    n, d = x.shape
    n_sub = d // LANES
    tm = _pick(n, (256, 128))
    xs0 = jnp.zeros((cap, n_sub, LANES), F32)
    return pl.pallas_call(
        _dispatch_kernel,
        grid_spec=pltpu.PrefetchScalarGridSpec(
            num_scalar_prefetch=2,
            grid=(n // tm,),
            in_specs=[pl.BlockSpec((tm, d), lambda i, a, b: (i, 0)),
                      pl.BlockSpec((None, 1, d), lambda i, a, b: (l, 0, 0)),
                      pl.BlockSpec(memory_space=pl.ANY)],
            out_specs=pl.BlockSpec(memory_space=pl.ANY),
            scratch_shapes=[pltpu.VMEM((2, tm * n_sub, LANES), F32), pltpu.SemaphoreType.DMA((2,))]),
        out_shape=jax.ShapeDtypeStruct((cap, n_sub, LANES), F32),
        input_output_aliases={4: 0},
        name="dispatch",
        compiler_params=pltpu.CompilerParams(dimension_semantics=("arbitrary",), vmem_limit_bytes=VMEM_LIMIT,
                                             has_side_effects=True),
    )(d0, d1, x, nw, xs0)


def _experts_kernel(be_ref, nu_ref, xs_ref, wg_ref, wu_ref, wd_ref, ys_ref, wg_bf, wu_bf, wd_bf, *, n_sub):
    i = pl.program_id(0)
    rows = xs_ref.shape[0] // n_sub

    @pl.when(i < nu_ref[0])
    def _():
        changed = jnp.logical_or(i == 0, be_ref[i] != be_ref[jnp.maximum(i - 1, 0)])

        @pl.when(changed)
        def _():
            wg_bf[...] = wg_ref[...].astype(BF16)
            wu_bf[...] = wu_ref[...].astype(BF16)
            wd_bf[...] = wd_ref[...].astype(BF16)

        x = jnp.concatenate([xs_ref[pl.ds(s, rows, stride=n_sub), :] for s in range(n_sub)], axis=1).astype(BF16)
        g = jnp.dot(x, wg_bf[...], preferred_element_type=F32)
        u = jnp.dot(x, wu_bf[...], preferred_element_type=F32)
        y = jnp.dot((_silu(g) * u).astype(BF16), wd_bf[...], preferred_element_type=F32)
        for s in range(n_sub):
            ys_ref[pl.ds(s, rows, stride=n_sub), :] = y[:, s * LANES:(s + 1) * LANES]

    @pl.when(i >= nu_ref[0])
    def _():
        ys_ref[...] = jnp.zeros(ys_ref.shape, F32)


def _experts(block_e, n_used, xs, w_g, w_u, w_d, l):
    cap, n_sub, _ = xs.shape
    _, _, d, f = w_g.shape
    n_blocks = cap // MOE_ROWS
    blk = MOE_ROWS * n_sub
    ys = pl.pallas_call(
        functools.partial(_experts_kernel, n_sub=n_sub),
        grid_spec=pltpu.PrefetchScalarGridSpec(
            num_scalar_prefetch=2,
            grid=(n_blocks,),
            in_specs=[pl.BlockSpec((blk, LANES), lambda i, be, nu: (i, 0)),
                      pl.BlockSpec((None, None, d, f), lambda i, be, nu: (l, be[i], 0, 0)),
                      pl.BlockSpec((None, None, d, f), lambda i, be, nu: (l, be[i], 0, 0)),
                      pl.BlockSpec((None, None, f, d), lambda i, be, nu: (l, be[i], 0, 0))],
            out_specs=pl.BlockSpec((blk, LANES), lambda i, be, nu: (i, 0)),
            scratch_shapes=[pltpu.VMEM((d, f), BF16), pltpu.VMEM((d, f), BF16), pltpu.VMEM((f, d), BF16)]),
        out_shape=jax.ShapeDtypeStruct((cap * n_sub, LANES), F32),
        name="experts",
        compiler_params=_cparams(("arbitrary",)),
    )(block_e, n_used, xs.reshape(cap * n_sub, LANES), w_g, w_u, w_d)
    return ys.reshape(cap, n_sub, LANES)


def _combine_kernel(d0_ref, d1_ref, x_ref, mf_ref, nw_ref, ys_ref, y_ref, buf, sem, *, final_norm):
    i = pl.program_id(0)
    tm = x_ref.shape[0]
    n_sub = x_ref.shape[1] // LANES
    base = i * tm

    def copies(r):
        dst = pl.ds(pl.multiple_of(r * n_sub, n_sub), n_sub)
        return (pltpu.make_async_copy(ys_ref.at[d0_ref[base + r]], buf.at[0, dst], sem),
                pltpu.make_async_copy(ys_ref.at[d1_ref[base + r]], buf.at[1, dst], sem))

    def start(r, carry):
        for cp in copies(r):
            cp.start()
        return carry

    def wait(r, carry):
        for cp in copies(r):
            cp.wait()
        return carry

    lax.fori_loop(0, tm, start, 0)
    lax.fori_loop(0, tm, wait, 0)

    w = mf_ref[...]
    y0 = jnp.concatenate([buf[0, pl.ds(s, tm, stride=n_sub), :] for s in range(n_sub)], axis=1)
    y1 = jnp.concatenate([buf[1, pl.ds(s, tm, stride=n_sub), :] for s in range(n_sub)], axis=1)
    out = x_ref[...] + (y0 * w[:, 0:1] + y1 * w[:, 1:2])
    if final_norm:
        out = _rms(out, nw_ref[...])
    y_ref[...] = out


def _combine(d0, d1, x, mf, nw, ys, final_norm):
    n, d = x.shape
    n_sub = d // LANES
    tm = _pick(n, (256, 128))
    return pl.pallas_call(
        functools.partial(_combine_kernel, final_norm=final_norm),
        grid_spec=pltpu.PrefetchScalarGridSpec(
            num_scalar_prefetch=2,
            grid=(n // tm,),
            in_specs=[pl.BlockSpec((tm, d), lambda i, a, b: (i, 0)),
                      pl.BlockSpec((tm, LANES), lambda i, a, b: (i, 0)),
                      pl.BlockSpec((1, d), lambda i, a, b: (0, 0)),
                      pl.BlockSpec(memory_space=pl.ANY)],
            out_specs=pl.BlockSpec((tm, d), lambda i, a, b: (i, 0)),
            scratch_shapes=[pltpu.VMEM((TOP_K, tm * n_sub, LANES), F32), pltpu.SemaphoreType.DMA(())]),
        out_shape=jax.ShapeDtypeStruct((n, d), F32),
        name="combine",
        compiler_params=_cparams(("arbitrary",)),
    )(d0, d1, x, mf, nw, ys)


def _moe(x, l, p, final_nw):
    n, d = x.shape
    mi, mf, cnt = _router(x, p['norm_ffn'], p['w_r'], p['b_r'], l)
    counts = cnt[0, N_GROUPS:N_GROUPS + N_EXPERTS].astype(jnp.int32)
    padded = (counts + MOE_ROWS - 1) // MOE_ROWS * MOE_ROWS
    pad_end = jnp.cumsum(padded)
    pad_start = pad_end - padded
    experts = jnp.arange(N_EXPERTS, dtype=jnp.int32)

    def dest(e, r):
        return r + jnp.sum(jnp.where(e[:, None] == experts[None, :], pad_start[None, :], 0), axis=1)

    d0 = dest(mi[0], mi[2])
    d1 = dest(mi[1], mi[3])
    n_blocks = -(-(n * TOP_K) // MOE_ROWS) + N_EXPERTS
    block_row = jnp.arange(n_blocks, dtype=jnp.int32) * MOE_ROWS
    block_e = jnp.minimum(jnp.sum((pad_end[None, :] <= block_row[:, None]).astype(jnp.int32), axis=1), N_EXPERTS - 1)
    n_used = (pad_end[-1:] // MOE_ROWS).astype(jnp.int32)
    xs = _dispatch(d0, d1, x, p['norm_ffn'], l, n_blocks * MOE_ROWS)
    ys = _experts(block_e, n_used, xs, p['w_eg'], p['w_eu'], p['w_ed'], l)
    nw = (final_nw if final_nw is not None else p['norm_ffn'][l, 0])[None]
    return _combine(d0, d1, x, mf, nw, ys, final_nw is not None)


def _trunk(x, pos, states, p):
    B, T, D = x.shape
    depth = p['w_in'].shape[0]
    half = HEAD_DIM // 2
    inv = ROPE_BASE ** (-jnp.arange(half, dtype=F32) / half)
    ang = pos.astype(F32)[:, None] * inv[None, :]
    cos = jnp.concatenate([jnp.cos(ang), jnp.cos(ang)], -1)
    sin = jnp.concatenate([-jnp.sin(ang), jnp.sin(ang)], -1)
    xf = x.reshape(B * T, D)
    new_states = None
    for l in range(depth):
        z = _inproj(xf, p['norm_mix'], p['w_in'], l)
        o, new_states = _mixers(z.reshape(B, T, SCAN_COLS), cos, sin, states, l, new_states, p)
        xf = _merge(xf, o.reshape(B * T, N_BRANCH * MIX_WIDTH), p['norm_mix'], p['w_in'], p['w_branch'], p['w_out'], l)
        xf = _moe(xf, l, p, p['norm_final'] if l == depth - 1 else None)
    sh, sr, c, n, m, cb = new_states
    return xf.reshape(B, T, D), (sh, sr, c, n, m.reshape(depth, B, N_HEADS), cb)


def kernel(x_prompt, x_sample, state_hgrn, state_ret, state_mlstm_C, state_mlstm_n, state_mlstm_m, state_mlstm_conv,
           norm_mix, norm_ffn, norm_final, w_in, hgrn_lb, hgrn_norm, ret_norm, mlstm_conv_w, mlstm_conv_b,
           mlstm_wq, mlstm_wk, mlstm_wv, mlstm_w_gates, mlstm_b_gates, mlstm_norm, mlstm_skip, w_branch, w_out,
           w_router_group, b_router_group, w_router_expert, b_router_expert, w_exp_gate, w_exp_up, w_exp_down):
    depth, D = norm_mix.shape
    H, d, W = N_HEADS, HEAD_DIM, MIX_WIDTH
    lb = jnp.cumsum(jax.nn.softmax(hgrn_lb.astype(F32), axis=0), axis=0)
    lb = lb - lb[0:1]
    prm = jnp.stack([jnp.log(lb), jnp.log1p(-lb), 1.0 - lb, hgrn_norm.astype(F32), ret_norm.astype(F32),
                     mlstm_norm.astype(F32), mlstm_skip.astype(F32), mlstm_conv_b.astype(F32)], axis=1)
    pad = LANES - N_GROUPS - N_EXPERTS
    w_r = jnp.concatenate([w_router_group, w_router_expert, jnp.zeros((depth, D, pad), F32)], -1)
    b_r = jnp.concatenate([b_router_group, b_router_expert, jnp.zeros((depth, pad), F32)], -1)[:, None, :]
    wg_bf = mlstm_w_gates.astype(BF16)
    p = {'norm_mix': norm_mix[:, None, :], 'norm_ffn': norm_ffn[:, None, :], 'norm_final': norm_final,
         'w_in': w_in.astype(BF16), 'prm': prm, 'conv_w': mlstm_conv_w.astype(F32),
         'mq': mlstm_wq.astype(BF16), 'mk': mlstm_wk.astype(BF16), 'mv': mlstm_wv.astype(BF16),
         'm_wg': wg_bf, 'm_wgt': jnp.swapaxes(wg_bf, 1, 2),
         'm_bg': mlstm_b_gates[:, None, :], 'm_bgt': mlstm_b_gates[:, :, None],
         'w_branch': w_branch.astype(BF16), 'w_out': w_out.astype(BF16),
         'w_r': w_r, 'b_r': b_r, 'w_eg': w_exp_gate, 'w_eu': w_exp_up, 'w_ed': w_exp_down}

    Bp, Tp = x_prompt.shape[0], x_prompt.shape[1]
    Bs = x_sample.shape[0]
    zero_states = (jnp.zeros((depth, Bp, H, d, d), F32), jnp.zeros((depth, Bp, H, d, d), F32),
                   jnp.zeros((depth, Bp, H, d, d), F32), jnp.zeros((depth, Bp, H, d), F32),
                   jnp.zeros((depth, Bp, 1, H), F32), jnp.zeros((depth, Bp, CONV_W - 1, W), F32))
    pos_prompt = jnp.arange(Tp, dtype=jnp.int32)
    pos_sample = PAST_LEN + jnp.arange(x_sample.shape[1], dtype=jnp.int32)
    y_prompt, ps = _trunk(x_prompt, pos_prompt, zero_states, p)
    sample_states = (state_hgrn, state_ret, state_mlstm_C, state_mlstm_n,
                     state_mlstm_m.reshape(depth, Bs, 1, H), state_mlstm_conv)
    y_sample, ss = _trunk(x_sample, pos_sample, sample_states, p)
    return (y_prompt, y_sample) + ps + ss
```

```python
import functools

import jax
import jax.numpy as jnp
import numpy as np
from jax import lax
from jax.experimental import pallas as pl
from jax.experimental.pallas import tpu as pltpu

F32 = jnp.float32
BF16 = jnp.bfloat16
HIGHEST = lax.Precision.HIGHEST

HEAD_DIM = 128
N_HEADS = 4
MIX_WIDTH = HEAD_DIM * N_HEADS
N_BRANCH = 3
CONV_W = 4
ROPE_BASE = 10000.0
N_GROUPS = 4
EXPERTS_PER_GROUP = 8
N_EXPERTS = N_GROUPS * EXPERTS_PER_GROUP
TOP_K = 2
PAST_LEN = 16384
NORM_EPS = 1e-6
HEAD_NORM_EPS = 1e-5
SCAN_COLS = 10 * MIX_WIDTH

LANES = 128
SUBLANES = 8
VMEM_LIMIT = 48 * 1024 * 1024
MOE_ROWS = 128
NEG_INF = float("-inf")


def _cparams(sem):
    return pltpu.CompilerParams(dimension_semantics=sem, vmem_limit_bytes=VMEM_LIMIT)


def _pick(n, cands):
    for c in cands:
        if n % c == 0:
            return c
    return n


def _rms(x, w):
    return x * lax.rsqrt(jnp.mean(x * x, -1, keepdims=True) + NORM_EPS) * w


def _sigmoid(x):
    return 1.0 / (1.0 + jnp.exp(-x))


def _silu(x):
    return x * _sigmoid(x)


def _log_sigmoid(x):
    return jnp.minimum(x, 0.0) - jnp.log1p(jnp.exp(-jnp.abs(x)))


def _dot(a, b):
    return jnp.dot(a.astype(BF16), b.astype(BF16), preferred_element_type=F32)


def _dot_nt(a, b):
    return lax.dot_general(a.astype(BF16), b.astype(BF16), (((1,), (1,)), ((), ())), preferred_element_type=F32)


def _dot_tn(a, b):
    return jnp.dot(a.T.astype(BF16), b.astype(BF16), preferred_element_type=F32)


def _split3(x):
    hi = x.astype(BF16)
    r = x - hi.astype(F32)
    mid = r.astype(BF16)
    lo = (r - mid.astype(F32)).astype(BF16)
    return hi, mid, lo


def _mask_dot(mask_bf, x):
    return sum(jnp.dot(mask_bf, part, preferred_element_type=F32) for part in _split3(x))


def _dot_mask(x, mask_bf):
    return sum(jnp.dot(part, mask_bf, preferred_element_type=F32) for part in _split3(x))


def _head_rmsnorm(o, w):
    return o * lax.rsqrt(jnp.mean(o * o, -1, keepdims=True) + HEAD_NORM_EPS) * w


def _head_layernorm(o, w):
    c = o - jnp.mean(o, -1, keepdims=True)
    return c * lax.rsqrt(jnp.mean(c * c, -1, keepdims=True) + HEAD_NORM_EPS) * w


def _level_masks(c):
    t = np.arange(c)[:, None]
    s = np.arange(c)[None, :]
    masks = [t == s]
    blk = 2
    while blk <= c:
        masks.append((t // blk == s // blk) & (t % blk >= blk // 2) & (s % blk < blk // 2))
        blk *= 2
    return np.stack(masks).astype(np.float32)


def _retention_tables(c):
    idx = np.arange(c, dtype=np.float64)
    lg = np.log(1.0 - 2.0 ** (-5.0 - np.arange(N_HEADS, dtype=np.float64)))[:, None, None]
    rel = idx[:, None] - idx[None, :]
    dm = np.where(rel >= 0, np.exp(np.maximum(rel, 0.0)[None] * lg), 0.0)
    q_in = np.broadcast_to(np.exp((idx + 1.0)[None, :, None] * lg), (N_HEADS, c, HEAD_DIM))
    k_out = np.broadcast_to(np.exp((c - 1.0 - idx)[None, :, None] * lg), (N_HEADS, c, HEAD_DIM))
    g_chunk = tuple(float(g) for g in np.exp(c * lg[:, 0, 0]))
    return dm.astype(np.float32), q_in.astype(np.float32), k_out.astype(np.float32), g_chunk


def _level_ref(G, blk, row):
    c, d = G.shape
    if blk == 2:
        return jnp.where((row & 1) == 1, pltpu.roll(G, 1, 0), G)
    if blk == 4:
        r = row & 3
        return jnp.where(r == 0, pltpu.roll(G, c - 1, 0),
                         jnp.where(r == 1, G, jnp.where(r == 2, pltpu.roll(G, 1, 0), pltpu.roll(G, 2, 0))))
    mid = blk // 2 - 1
    G3 = G.reshape(c // blk, blk, d)
    return jnp.broadcast_to(G3[:, mid:mid + 1, :], (c // blk, blk, d)).reshape(c, d)


def _mixer_kernel(*refs, tt, bb, n_t, n_alias, g_chunk):
    (x_ref, nw_ref, w_ref, cos_ref, sin_ref, lvl_ref, dm_ref, qin_ref, kout_ref,
     sh0_ref, sr0_ref, c0_ref, n0_ref, m0_ref, cb0_ref,
     prm_ref, convw_ref, mq_ref, mk_ref, mv_ref, wg_ref, wgt_ref, bg_ref, bgt_ref) = refs[:24]
    o_ref, sh_ref, sr_ref, c_ref, n_ref, m_ref, cb_ref, z_ref, ext_ref = refs[24 + n_alias:]
    ti = pl.program_id(1)
    d, W, H = HEAD_DIM, MIX_WIDTH, N_HEADS
    n_lvl = lvl_ref.shape[0]

    x = x_ref[...].reshape(bb * tt, x_ref.shape[-1])
    z_ref[...] = jnp.dot(_rms(x, nw_ref[...]).astype(BF16), w_ref[...], preferred_element_type=F32)

    @pl.when(ti == 0)
    def _():
        def init(bi, carry):
            for h in range(H):
                sh_ref[bi, h] = sh0_ref[bi, h].T
            return carry
        lax.fori_loop(0, bb, init, 0)
        sr_ref[...] = sr0_ref[...]
        c_ref[...] = c0_ref[...]
        n_ref[...] = n0_ref[...]
        m_ref[...] = m0_ref[...]
        cb_ref[...] = cb0_ref[...]

    def prm(i, h):
        return prm_ref[i:i + 1, h * d:(h + 1) * d]

    t_i = lax.broadcasted_iota(jnp.int32, (tt, tt), 0)
    s_i = lax.broadcasted_iota(jnp.int32, (tt, tt), 1)
    causal = t_i >= s_i
    tri_l = causal.astype(BF16)
    tri_u = (t_i <= s_i).astype(BF16)
    row = lax.broadcasted_iota(jnp.int32, (tt, d), 0)
    cos = cos_ref[...]
    sin = sin_ref[...]

    def rotary(a_):
        return a_ * cos + pltpu.roll(a_, d // 2, 1) * sin

    def per_batch(bi, carry):
        rows = pl.ds(pl.multiple_of(bi * tt, tt), tt)

        def zc(j, h):
            return z_ref[rows, j * W + h * d:j * W + (h + 1) * d]

        fpre = z_ref[rows, W:2 * W]
        e = jnp.exp(-jnp.abs(fpre))
        a = prm_ref[0:1, :]
        b = prm_ref[1:2, :] + (jnp.minimum(fpre, 0.0) - jnp.log1p(e))
        logf = jnp.maximum(a, b) + jnp.log1p(jnp.exp(-jnp.abs(a - b)))
        k_all = prm_ref[2:3, :] * (jnp.where(fpre >= 0.0, e, 1.0) / (1.0 + e))
        G_all = _mask_dot(tri_l, logf)
        for h in range(H):
            hs = slice(h * d, (h + 1) * d)
            q, v, gate = zc(0, h), zc(2, h), zc(3, h)
            k, G = k_all[:, hs], G_all[:, hs]
            A = lvl_ref[0] * _dot_nt(q, k)
            for j in range(1, n_lvl):
                E = jnp.exp(-jnp.abs(G - _level_ref(G, 1 << j, row)))
                A = A + lvl_ref[j] * _dot_nt(q * E, k * E)
            st = sh_ref[bi, h]
            o_h = _dot(A, v) + _dot_nt(q * jnp.exp(G), st)
            GL = G[tt - 1:tt]
            sh_ref[bi, h] = jnp.exp(GL) * st + _dot(v.T, k * jnp.exp(GL - G))
            o_ref[bi, :, hs] = _head_rmsnorm(o_h, prm(3, h)) * _silu(gate)

        for h in range(H):
            q = rotary(zc(4, h))
            k = rotary(zc(5, h)) * (d ** -0.5)
            v, gate = zc(6, h), zc(7, h)
            A = _dot_nt(q, k) * dm_ref[h]
            S = sr_ref[bi, h]
            o_h = _dot(A, v) + qin_ref[h] * _dot(q, S)
            sr_ref[bi, h] = g_chunk[h] * S + _dot_tn(k * kout_ref[h], v)
            o_ref[bi, :, W + h * d:W + (h + 1) * d] = _head_layernorm(o_h, prm(4, h)) * _silu(gate)

        u = z_ref[rows, 8 * W:9 * W]
        ext_ref[SUBLANES - (CONV_W - 1):SUBLANES, :] = cb_ref[bi]
        ext_ref[SUBLANES:SUBLANES + tt, :] = u
        conv = jnp.zeros((tt, W), F32)
        for j in range(CONV_W):
            off = SUBLANES - (CONV_W - 1) + j
            conv = conv + convw_ref[j:j + 1, :] * ext_ref[off:off + tt, :]
        cb_ref[bi] = ext_ref[SUBLANES + tt - (CONV_W - 1):SUBLANES + tt, :]
        uc = _silu(conv + prm_ref[7:8, :])

        qs, ks, vs = [], [], []
        for h in range(H):
            uch = uc[:, h * d:(h + 1) * d].astype(BF16)
            qs.append(jnp.dot(uch, mq_ref[h], preferred_element_type=F32))
            ks.append(jnp.dot(uch, mk_ref[h], preferred_element_type=F32))
            vs.append(jnp.dot(u[:, h * d:(h + 1) * d].astype(BF16), mv_ref[h], preferred_element_type=F32))
        qkv = jnp.concatenate(qs + ks + vs, axis=1).astype(BF16)
        g_col = jnp.dot(qkv, wg_ref[...], preferred_element_type=F32) + bg_ref[...]
        g_row = lax.dot_general(wgt_ref[...], qkv, (((1,), (1,)), ((), ())),
                                preferred_element_type=F32) + bgt_ref[...]
        i_cols = g_col[:, :H]
        f_cols = _mask_dot(tri_l, _log_sigmoid(g_col[:, H:]))
        i_rows = g_row[:H]
        f_rows = _dot_mask(_log_sigmoid(g_row[H:]), tri_u)
        m_prev_all = m_ref[bi]
        for h in range(H):
            q, k, v = qs[h], ks[h] * (d ** -0.5), vs[h]
            i_col, F_col = i_cols[:, h:h + 1], f_cols[:, h:h + 1]
            a_row = i_rows[h:h + 1] - f_rows[h:h + 1]
            m_prev = m_prev_all[:, h:h + 1]
            cm = jnp.max(jnp.where(causal, a_row, NEG_INF), -1, keepdims=True)
            m_col = F_col + jnp.maximum(m_prev, cm)
            logd = (F_col - m_col) + a_row
            dmat = jnp.where(causal, jnp.exp(jnp.where(causal, logd, 0.0)), 0.0)
            Sc = _dot_nt(q, k) * dmat
            inter = jnp.exp(F_col + m_prev - m_col)
            Cst = c_ref[bi, h]
            n_row = n_ref[bi, h:h + 1, :]
            num = _dot(Sc, v) + inter * _dot(q, Cst)
            den = jnp.sum(Sc, -1, keepdims=True) + inter * jnp.sum(q * n_row, -1, keepdims=True)
            hout = num / jnp.maximum(jnp.abs(den), jnp.exp(-m_col))
            mL = m_col[tt - 1:tt]
            FL = F_col[tt - 1:tt]
            kw = k * jnp.exp(FL - F_col + i_col - mL)
            decay = jnp.exp(FL + m_prev - mL)
            c_ref[bi, h] = decay * Cst + _dot_tn(kw, v)
            n_ref[bi, h:h + 1, :] = decay * n_row + jnp.sum(kw, 0, keepdims=True)
            m_ref[bi, :, h:h + 1] = mL
            y = _head_layernorm(hout, prm(5, h)) + prm(6, h) * uc[:, h * d:(h + 1) * d]
            o_ref[bi, :, 2 * W + h * d:2 * W + (h + 1) * d] = y * _silu(zc(9, h))
        return carry

    lax.fori_loop(0, bb, per_batch, 0)

    @pl.when(ti == n_t - 1)
    def _():
        def fin(bi, carry):
            for h in range(H):
                sh_ref[bi, h] = sh_ref[bi, h].T
            return carry
        lax.fori_loop(0, bb, fin, 0)


def _mixers(x, cos, sin, states_in, l, prev_out, p):
    B, T, D = x.shape
    depth = states_in[0].shape[0]
    tt = _pick(T, (128, 64, 32, 16, 8))
    bb = _pick(B, tuple(c for c in (8, 4, 2) if c * tt <= 128) + (1,))
    n_t = T // tt
    H, d, W = N_HEADS, HEAD_DIM, MIX_WIDTH
    dm, q_in, k_out, g_chunk = _retention_tables(tt)
    consts = [jnp.asarray(_level_masks(tt)), jnp.asarray(dm), jnp.asarray(q_in), jnp.asarray(k_out)]
    st_spec = pl.BlockSpec((None, bb, H, d, d), lambda b, t: (l, b, 0, 0, 0))
    n_spec = pl.BlockSpec((None, bb, H, d), lambda b, t: (l, b, 0, 0))
    m_spec = pl.BlockSpec((None, bb, 1, H), lambda b, t: (l, b, 0, 0))
    cb_spec = pl.BlockSpec((None, bb, CONV_W - 1, W), lambda b, t: (l, b, 0, 0))
    state_specs = [st_spec, st_spec, st_spec, n_spec, m_spec, cb_spec]

    def full(a):
        nd = a.ndim
        return pl.BlockSpec(a.shape, lambda b, t: (0,) * nd)

    def layer(a):
        nd = a.ndim - 1
        return pl.BlockSpec((None,) + a.shape[1:], lambda b, t: (l,) + (0,) * nd)

    weights = [p['prm'], p['conv_w'], p['mq'], p['mk'], p['mv'], p['m_wg'], p['m_wgt'], p['m_bg'], p['m_bgt']]
    alias_in = list(prev_out) if prev_out is not None else []
    n_fixed = 5 + len(consts) + 6 + len(weights)
    state_shapes = [jax.ShapeDtypeStruct((depth, B, H, d, d), F32)] * 3 + [
        jax.ShapeDtypeStruct((depth, B, H, d), F32),
        jax.ShapeDtypeStruct((depth, B, 1, H), F32),
        jax.ShapeDtypeStruct((depth, B, CONV_W - 1, W), F32)]
    outs = pl.pallas_call(
        functools.partial(_mixer_kernel, tt=tt, bb=bb, n_t=n_t, n_alias=len(alias_in), g_chunk=g_chunk),
        grid=(B // bb, n_t),
        in_specs=[pl.BlockSpec((bb, tt, D), lambda b, t: (b, t, 0)),
                  pl.BlockSpec((None, 1, D), lambda b, t: (l, 0, 0)),
                  pl.BlockSpec((None, D, SCAN_COLS), lambda b, t: (l, 0, 0), pipeline_mode=pl.Buffered(1)),
                  pl.BlockSpec((tt, d), lambda b, t: (t, 0)),
                  pl.BlockSpec((tt, d), lambda b, t: (t, 0))]
                 + [full(c) for c in consts] + state_specs + [layer(w) for w in weights]
                 + [pl.BlockSpec(memory_space=pl.ANY)] * len(alias_in),
        out_specs=[pl.BlockSpec((bb, tt, N_BRANCH * W), lambda b, t: (b, t, 0))] + state_specs,
        out_shape=[jax.ShapeDtypeStruct((B, T, N_BRANCH * W), F32)] + state_shapes,
        input_output_aliases={n_fixed + i: 1 + i for i in range(len(alias_in))},
        scratch_shapes=[pltpu.VMEM((bb * tt, SCAN_COLS), F32), pltpu.VMEM((SUBLANES + tt, W), F32)],
        name="mixers",
        compiler_params=_cparams(("parallel", "arbitrary")),
    )(x, p['norm_mix'], p['w_in'], cos, sin, *consts, *states_in, *weights, *alias_in)
    return outs[0], tuple(outs[1:])


def _merge_kernel(x_ref, o_ref, nw_ref, wg0_ref, wg1_ref, wg2_ref, wbr_ref, wout_ref, y_ref):
    x = x_ref[...]
    W = MIX_WIDTH
    hn = _rms(x, nw_ref[...]).astype(BF16)
    merged = jnp.zeros(x.shape, F32)
    for n, wg_ref in enumerate((wg0_ref, wg1_ref, wg2_ref)):
        gz = jnp.dot(hn, wg_ref[...], preferred_element_type=F32)
        proj = jnp.dot(o_ref[:, n * W:(n + 1) * W].astype(BF16), wbr_ref[n], preferred_element_type=F32)
        merged = merged + _sigmoid(gz) * proj
    y_ref[...] = x + jnp.dot(merged.astype(BF16), wout_ref[...], preferred_element_type=F32)


def _merge(x, o, nw, w_in, wbr, wout, l):
    n, d = x.shape
    tm = _pick(n, (512, 256, 128))
    g0 = SCAN_COLS // d

    def gate_spec(k):
        return pl.BlockSpec((None, d, d), lambda i: (l, 0, g0 + k))

    return pl.pallas_call(
        _merge_kernel,
        grid=(n // tm,),
        in_specs=[pl.BlockSpec((tm, d), lambda i: (i, 0)),
                  pl.BlockSpec((tm, N_BRANCH * MIX_WIDTH), lambda i: (i, 0)),
                  pl.BlockSpec((None, 1, d), lambda i: (l, 0, 0)),
                  gate_spec(0), gate_spec(1), gate_spec(2),
                  pl.BlockSpec((None,) + wbr.shape[1:], lambda i: (l, 0, 0, 0)),
                  pl.BlockSpec((None,) + wout.shape[1:], lambda i: (l, 0, 0))],
        out_specs=pl.BlockSpec((tm, d), lambda i: (i, 0)),
        out_shape=jax.ShapeDtypeStruct((n, d), F32),
        name="merge",
        compiler_params=_cparams(("parallel",)),
    )(x, o, nw, w_in, w_in, w_in, wbr, wout)


def _router_kernel(x_ref, nw_ref, wr_ref, br_ref, mi_ref, mf_ref, cnt_ref, carry_ref):
    i = pl.program_id(0)
    tm = x_ref.shape[0]

    @pl.when(i == 0)
    def _():
        carry_ref[...] = jnp.zeros(carry_ref.shape, F32)

    hn = _rms(x_ref[...], nw_ref[...])
    logits = _dot(hn, wr_ref[...]) + br_ref[...]
    lane = lax.broadcasted_iota(jnp.int32, logits.shape, 1)
    big = jnp.int32(1 << 20)

    def first_max(mask):
        vmax = jnp.max(jnp.where(mask, logits, NEG_INF), -1, keepdims=True)
        imax = jnp.min(jnp.where(mask, jnp.where(logits == vmax, lane, big), big), -1, keepdims=True)
        return vmax, imax

    in_groups = lane < N_GROUPS
    g_max, g_top = first_max(in_groups)
    g_w = 1.0 / jnp.sum(jnp.where(in_groups, jnp.exp(logits - g_max), 0.0), -1, keepdims=True)
    e_lo = N_GROUPS + EXPERTS_PER_GROUP * g_top
    in_e = jnp.logical_and(lane >= e_lo, lane < e_lo + EXPERTS_PER_GROUP)
    v1, i1 = first_max(in_e)
    v2, i2 = first_max(jnp.logical_and(in_e, lane != i1))
    e2 = jnp.exp(v2 - v1)
    w1 = g_w / (1.0 + e2)
    w2 = g_w * e2 / (1.0 + e2)

    hit1 = lane == i1
    hit2 = lane == i2
    onehot = jnp.where(hit1, 1.0, 0.0) + jnp.where(hit2, 1.0, 0.0)
    strict = (lax.broadcasted_iota(jnp.int32, (tm, tm), 0) > lax.broadcasted_iota(jnp.int32, (tm, tm), 1))
    before = _dot(strict.astype(F32), onehot) + carry_ref[...]
    r1 = jnp.sum(jnp.where(hit1, before, 0.0), -1, keepdims=True).astype(jnp.int32)
    r2 = jnp.sum(jnp.where(hit2, before, 0.0), -1, keepdims=True).astype(jnp.int32)
    carry_ref[...] = carry_ref[...] + jnp.sum(onehot, 0, keepdims=True)
    cnt_ref[...] = carry_ref[...]

    meta = jnp.where(lane == 0, i1 - N_GROUPS,
                     jnp.where(lane == 1, i2 - N_GROUPS,
                               jnp.where(lane == 2, r1, jnp.where(lane == 3, r2, 0))))
    mi_ref[...] = meta.T[:SUBLANES]
    mf_ref[...] = jnp.where(lane == 0, w1, jnp.where(lane == 1, w2, 0.0))


def _router(x, nw, wr, br, l):
    n, d = x.shape
    tm = _pick(n, (256, 128))
    return pl.pallas_call(
        _router_kernel,
        grid=(n // tm,),
        in_specs=[pl.BlockSpec((tm, d), lambda i: (i, 0)),
                  pl.BlockSpec((None, 1, d), lambda i: (l, 0, 0)),
                  pl.BlockSpec((None, d, LANES), lambda i: (l, 0, 0)),
                  pl.BlockSpec((None, 1, LANES), lambda i: (l, 0, 0))],
        out_specs=[pl.BlockSpec((SUBLANES, tm), lambda i: (0, i)),
                   pl.BlockSpec((tm, LANES), lambda i: (i, 0)),
                   pl.BlockSpec((1, LANES), lambda i: (0, 0))],
        out_shape=[jax.ShapeDtypeStruct((SUBLANES, n), jnp.int32),
                   jax.ShapeDtypeStruct((n, LANES), F32),
                   jax.ShapeDtypeStruct((1, LANES), F32)],
        scratch_shapes=[pltpu.VMEM((1, LANES), F32)],
        name="router",
        compiler_params=_cparams(("arbitrary",)),
    )(x, nw, wr, br)


def _dispatch_kernel(d0_ref, d1_ref, x_ref, nw_ref, xs_in_ref, xs_ref, buf, sem):
    del xs_in_ref
    i = pl.program_id(0)
    n_i = pl.num_programs(0)
    tm = x_ref.shape[0]
    n_sub = x_ref.shape[1] // LANES
    slot = i % 2

    def copies(step, sl, r):
        t = step * tm + r
        src = buf.at[sl, pl.ds(pl.multiple_of(r * n_sub, n_sub), n_sub)]
        return (pltpu.make_async_copy(src, xs_ref.at[d0_ref[t]], sem.at[sl]),
                pltpu.make_async_copy(src, xs_ref.at[d1_ref[t]], sem.at[sl]))

    def wait_step(step, sl):
        def body(r, carry):
            for cp in copies(step, sl, r):
                cp.wait()
            return carry
        lax.fori_loop(0, tm, body, 0)

    @pl.when(i >= 2)
    def _():
        wait_step(i - 2, slot)

    hn = _rms(x_ref[...], nw_ref[...])
    for s in range(n_sub):
        buf[slot, pl.ds(s, tm, stride=n_sub), :] = hn[:, s * LANES:(s + 1) * LANES]

    def start(r, carry):
        for cp in copies(i, slot, r):
            cp.start()
        return carry

    lax.fori_loop(0, tm, start, 0)

    @pl.when(i == n_i - 1)
    def _():
        @pl.when(i >= 1)
        def _():
            wait_step(i - 1, 1 - slot)
        wait_step(i, slot)


def _dispatch(d0, d1, x, nw, l, cap):
    n, d = x.shape
    n_sub = d // LANES
    tm = _pick(n, (256, 128))
    xs0 = jnp.zeros((cap, n_sub, LANES), F32)
    return pl.pallas_call(
        _dispatch_kernel,
        grid_spec=pltpu.PrefetchScalarGridSpec(
            num_scalar_prefetch=2,
            grid=(n // tm,),
            in_specs=[pl.BlockSpec((tm, d), lambda i, a, b: (i, 0)),
                      pl.BlockSpec((None, 1, d), lambda i, a, b: (l, 0, 0)),
                      pl.BlockSpec(memory_space=pl.ANY)],
            out_specs=pl.BlockSpec(memory_space=pl.ANY),
            scratch_shapes=[pltpu.VMEM((2, tm * n_sub, LANES), F32), pltpu.SemaphoreType.DMA((2,))]),
        out_shape=jax.ShapeDtypeStruct((cap, n_sub, LANES), F32),
        input_output_aliases={4: 0},
        name="dispatch",
        compiler_params=pltpu.CompilerParams(dimension_semantics=("arbitrary",), vmem_limit_bytes=VMEM_LIMIT,
                                             has_side_effects=True),
    )(d0, d1, x, nw, xs0)


def _experts_kernel(be_ref, nu_ref, xs_ref, wg_ref, wu_ref, wd_ref, ys_ref, wg_bf, wu_bf, wd_bf, *, n_sub):
    i = pl.program_id(0)
    rows = xs_ref.shape[0] // n_sub

    @pl.when(i < nu_ref[0])
    def _():
        changed = jnp.logical_or(i == 0, be_ref[i] != be_ref[jnp.maximum(i - 1, 0)])

        @pl.when(changed)
        def _():
            wg_bf[...] = wg_ref[...].astype(BF16)
            wu_bf[...] = wu_ref[...].astype(BF16)
            wd_bf[...] = wd_ref[...].astype(BF16)

        x = jnp.concatenate([xs_ref[pl.ds(s, rows, stride=n_sub), :] for s in range(n_sub)], axis=1).astype(BF16)
        g = jnp.dot(x, wg_bf[...], preferred_element_type=F32)
        u = jnp.dot(x, wu_bf[...], preferred_element_type=F32)
        y = jnp.dot((_silu(g) * u).astype(BF16), wd_bf[...], preferred_element_type=F32)
        for s in range(n_sub):
            ys_ref[pl.ds(s, rows, stride=n_sub), :] = y[:, s * LANES:(s + 1) * LANES]

    @pl.when(i >= nu_ref[0])
    def _():
        ys_ref[...] = jnp.zeros(ys_ref.shape, F32)


def _experts(block_e, n_used, xs, w_g, w_u, w_d, l):
    cap, n_sub, _ = xs.shape
    _, _, d, f = w_g.shape
    n_blocks = cap // MOE_ROWS
    blk = MOE_ROWS * n_sub
    ys = pl.pallas_call(
        functools.partial(_experts_kernel, n_sub=n_sub),
        grid_spec=pltpu.PrefetchScalarGridSpec(
            num_scalar_prefetch=2,
            grid=(n_blocks,),
            in_specs=[pl.BlockSpec((blk, LANES), lambda i, be, nu: (i, 0)),
                      pl.BlockSpec((None, None, d, f), lambda i, be, nu: (l, be[i], 0, 0)),
                      pl.BlockSpec((None, None, d, f), lambda i, be, nu: (l, be[i], 0, 0)),
                      pl.BlockSpec((None, None, f, d), lambda i, be, nu: (l, be[i], 0, 0))],
            out_specs=pl.BlockSpec((blk, LANES), lambda i, be, nu: (i, 0)),
            scratch_shapes=[pltpu.VMEM((d, f), BF16), pltpu.VMEM((d, f), BF16), pltpu.VMEM((f, d), BF16)]),
        out_shape=jax.ShapeDtypeStruct((cap * n_sub, LANES), F32),
        name="experts",
        compiler_params=_cparams(("arbitrary",)),
    )(block_e, n_used, xs.reshape(cap * n_sub, LANES), w_g, w_u, w_d)
    return ys.reshape(cap, n_sub, LANES)


def _combine_kernel(d0_ref, d1_ref, x_ref, mf_ref, nw_ref, ys_ref, y_ref, buf, sem, *, final_norm):
    i = pl.program_id(0)
    tm = x_ref.shape[0]
    n_sub = x_ref.shape[1] // LANES
    base = i * tm

    def copies(r):
        dst = pl.ds(pl.multiple_of(r * n_sub, n_sub), n_sub)
        return (pltpu.make_async_copy(ys_ref.at[d0_ref[base + r]], buf.at[0, dst], sem),
                pltpu.make_async_copy(ys_ref.at[d1_ref[base + r]], buf.at[1, dst], sem))

    def start(r, carry):
        for cp in copies(r):
            cp.start()
        return carry

    def wait(r, carry):
        for cp in copies(r):
            cp.wait()
        return carry

    lax.fori_loop(0, tm, start, 0)
    lax.fori_loop(0, tm, wait, 0)

    w = mf_ref[...]
    y0 = jnp.concatenate([buf[0, pl.ds(s, tm, stride=n_sub), :] for s in range(n_sub)], axis=1)
    y1 = jnp.concatenate([buf[1, pl.ds(s, tm, stride=n_sub), :] for s in range(n_sub)], axis=1)
    out = x_ref[...] + (y0 * w[:, 0:1] + y1 * w[:, 1:2])
    if final_norm:
        out = _rms(out, nw_ref[...])
    y_ref[...] = out


def _combine(d0, d1, x, mf, nw, ys, final_norm):
    n, d = x.shape
    n_sub = d // LANES
    tm = _pick(n, (256, 128))
    return pl.pallas_call(
        functools.partial(_combine_kernel, final_norm=final_norm),
        grid_spec=pltpu.PrefetchScalarGridSpec(
            num_scalar_prefetch=2,
            grid=(n // tm,),
            in_specs=[pl.BlockSpec((tm, d), lambda i, a, b: (i, 0)),
                      pl.BlockSpec((tm, LANES), lambda i, a, b: (i, 0)),
                      pl.BlockSpec((1, d), lambda i, a, b: (0, 0)),
                      pl.BlockSpec(memory_space=pl.ANY)],
            out_specs=pl.BlockSpec((tm, d), lambda i, a, b: (i, 0)),
            scratch_shapes=[pltpu.VMEM((TOP_K, tm * n_sub, LANES), F32), pltpu.SemaphoreType.DMA(())]),
        out_shape=jax.ShapeDtypeStruct((n, d), F32),
        name="combine",
        compiler_params=_cparams(("arbitrary",)),
    )(d0, d1, x, mf, nw, ys)


def _moe(x, l, p, final_nw):
    n, d = x.shape
    mi, mf, cnt = _router(x, p['norm_ffn'], p['w_r'], p['b_r'], l)
    counts = cnt[0, N_GROUPS:N_GROUPS + N_EXPERTS].astype(jnp.int32)
    padded = (counts + MOE_ROWS - 1) // MOE_ROWS * MOE_ROWS
    pad_end = jnp.cumsum(padded)
    pad_start = pad_end - padded
    experts = jnp.arange(N_EXPERTS, dtype=jnp.int32)

    def dest(e, r):
        return r + jnp.sum(jnp.where(e[:, None] == experts[None, :], pad_start[None, :], 0), axis=1)

    d0 = dest(mi[0], mi[2])
    d1 = dest(mi[1], mi[3])
    n_blocks = -(-(n * TOP_K) // MOE_ROWS) + N_EXPERTS
    block_row = jnp.arange(n_blocks, dtype=jnp.int32) * MOE_ROWS
    block_e = jnp.minimum(jnp.sum((pad_end[None, :] <= block_row[:, None]).astype(jnp.int32), axis=1), N_EXPERTS - 1)
    n_used = (pad_end[-1:] // MOE_ROWS).astype(jnp.int32)
    xs = _dispatch(d0, d1, x, p['norm_ffn'], l, n_blocks * MOE_ROWS)
    ys = _experts(block_e, n_used, xs, p['w_eg'], p['w_eu'], p['w_ed'], l)
    nw = (final_nw if final_nw is not None else p['norm_ffn'][l, 0])[None]
    return _combine(d0, d1, x, mf, nw, ys, final_nw is not None)


def _trunk(x, pos, states, p):
    B, T, D = x.shape
    depth = p['w_in'].shape[0]
    half = HEAD_DIM // 2
    inv = ROPE_BASE ** (-jnp.arange(half, dtype=F32) / half)
    ang = pos.astype(F32)[:, None] * inv[None, :]
    cos = jnp.concatenate([jnp.cos(ang), jnp.cos(ang)], -1)
    sin = jnp.concatenate([-jnp.sin(ang), jnp.sin(ang)], -1)
    xf = x.reshape(B * T, D)
    new_states = None
    for l in range(depth):
        o, new_states = _mixers(xf.reshape(B, T, D), cos, sin, states, l, new_states, p)
        xf = _merge(xf, o.reshape(B * T, N_BRANCH * MIX_WIDTH), p['norm_mix'], p['w_in'], p['w_branch'], p['w_out'], l)
        xf = _moe(xf, l, p, p['norm_final'] if l == depth - 1 else None)
    sh, sr, c, n, m, cb = new_states
    return xf.reshape(B, T, D), (sh, sr, c, n, m.reshape(depth, B, N_HEADS), cb)


def kernel(x_prompt, x_sample, state_hgrn, state_ret, state_mlstm_C, state_mlstm_n, state_mlstm_m, state_mlstm_conv,
           norm_mix, norm_ffn, norm_final, w_in, hgrn_lb, hgrn_norm, ret_norm, mlstm_conv_w, mlstm_conv_b,
           mlstm_wq, mlstm_wk, mlstm_wv, mlstm_w_gates, mlstm_b_gates, mlstm_norm, mlstm_skip, w_branch, w_out,
           w_router_group, b_router_group, w_router_expert, b_router_expert, w_exp_gate, w_exp_up, w_exp_down):
    depth, D = norm_mix.shape
    H, d, W = N_HEADS, HEAD_DIM, MIX_WIDTH
    lb = jnp.cumsum(jax.nn.softmax(hgrn_lb.astype(F32), axis=0), axis=0)
    lb = lb - lb[0:1]
    prm = jnp.stack([jnp.log(lb), jnp.log1p(-lb), 1.0 - lb, hgrn_norm.astype(F32), ret_norm.astype(F32),
                     mlstm_norm.astype(F32), mlstm_skip.astype(F32), mlstm_conv_b.astype(F32)], axis=1)
    pad = LANES - N_GROUPS - N_EXPERTS
    w_r = jnp.concatenate([w_router_group, w_router_expert, jnp.zeros((depth, D, pad), F32)], -1)
    b_r = jnp.concatenate([b_router_group, b_router_expert, jnp.zeros((depth, pad), F32)], -1)[:, None, :]
    wg_bf = mlstm_w_gates.astype(BF16)
    p = {'norm_mix': norm_mix[:, None, :], 'norm_ffn': norm_ffn[:, None, :], 'norm_final': norm_final,
         'w_in': w_in.astype(BF16), 'prm': prm, 'conv_w': mlstm_conv_w.astype(F32),
         'mq': mlstm_wq.astype(BF16), 'mk': mlstm_wk.astype(BF16), 'mv': mlstm_wv.astype(BF16),
         'm_wg': wg_bf, 'm_wgt': jnp.swapaxes(wg_bf, 1, 2),
         'm_bg': mlstm_b_gates[:, None, :], 'm_bgt': mlstm_b_gates[:, :, None],
         'w_branch': w_branch.astype(BF16), 'w_out': w_out.astype(BF16),
         'w_r': w_r, 'b_r': b_r, 'w_eg': w_exp_gate, 'w_eu': w_exp_up, 'w_ed': w_exp_down}

    Bp, Tp = x_prompt.shape[0], x_prompt.shape[1]
    Bs = x_sample.shape[0]
    zero_states = (jnp.zeros((depth, Bp, H, d, d), F32), jnp.zeros((depth, Bp, H, d, d), F32),
                   jnp.zeros((depth, Bp, H, d, d), F32), jnp.zeros((depth, Bp, H, d), F32),
                   jnp.zeros((depth, Bp, 1, H), F32), jnp.zeros((depth, Bp, CONV_W - 1, W), F32))
    pos_prompt = jnp.arange(Tp, dtype=jnp.int32)
    pos_sample = PAST_LEN + jnp.arange(x_sample.shape[1], dtype=jnp.int32)
    y_prompt, ps = _trunk(x_prompt, pos_prompt, zero_states, p)
    sample_states = (state_hgrn, state_ret, state_mlstm_C, state_mlstm_n,
                     state_mlstm_m.reshape(depth, Bs, 1, H), state_mlstm_conv)
    y_sample, ss = _trunk(x_sample, pos_sample, sample_states, p)
    return (y_prompt, y_sample) + ps + ss
```

```python
import functools

import jax
import jax.numpy as jnp
import numpy as np
from jax import lax
from jax.experimental import pallas as pl
from jax.experimental.pallas import tpu as pltpu

F32 = jnp.float32
BF16 = jnp.bfloat16
HIGHEST = lax.Precision.HIGHEST

HEAD_DIM = 128
N_HEADS = 4
MIX_WIDTH = HEAD_DIM * N_HEADS
N_BRANCH = 3
CONV_W = 4
ROPE_BASE = 10000.0
N_GROUPS = 4
EXPERTS_PER_GROUP = 8
N_EXPERTS = N_GROUPS * EXPERTS_PER_GROUP
TOP_K = 2
PAST_LEN = 16384
NORM_EPS = 1e-6
HEAD_NORM_EPS = 1e-5
SCAN_COLS = 10 * MIX_WIDTH

LANES = 128
SUBLANES = 8
VMEM_LIMIT = 48 * 1024 * 1024
MOE_ROWS = 128
NEG_INF = float("-inf")


def _cparams(sem):
    return pltpu.CompilerParams(dimension_semantics=sem, vmem_limit_bytes=VMEM_LIMIT)


def _pick(n, cands):
    for c in cands:
        if n % c == 0:
            return c
    return n


def _rms(x, w):
    return x * lax.rsqrt(jnp.mean(x * x, -1, keepdims=True) + NORM_EPS) * w


def _sigmoid(x):
    return 1.0 / (1.0 + jnp.exp(-x))


def _silu(x):
    return x * _sigmoid(x)


def _log_sigmoid(x):
    return jnp.minimum(x, 0.0) - jnp.log(1.0 + jnp.exp(-jnp.abs(x)))


def _dot(a, b):
    return jnp.dot(a.astype(BF16), b.astype(BF16), preferred_element_type=F32)


def _dot_nt(a, b):
    return lax.dot_general(a.astype(BF16), b.astype(BF16), (((1,), (1,)), ((), ())), preferred_element_type=F32)


def _dot_tn(a, b):
    return jnp.dot(a.T.astype(BF16), b.astype(BF16), preferred_element_type=F32)


def _split3(x):
    hi = x.astype(BF16)
    r = x - hi.astype(F32)
    mid = r.astype(BF16)
    lo = (r - mid.astype(F32)).astype(BF16)
    return hi, mid, lo


def _mask_dot(mask_bf, x):
    return sum(jnp.dot(mask_bf, part, preferred_element_type=F32) for part in _split3(x))


def _dot_mask(x, mask_bf):
    return sum(jnp.dot(part, mask_bf, preferred_element_type=F32) for part in _split3(x))


def _head_rmsnorm(o, w):
    return o * lax.rsqrt(jnp.mean(o * o, -1, keepdims=True) + HEAD_NORM_EPS) * w


def _head_layernorm(o, w):
    c = o - jnp.mean(o, -1, keepdims=True)
    return c * lax.rsqrt(jnp.mean(c * c, -1, keepdims=True) + HEAD_NORM_EPS) * w


def _level_masks(c):
    t = np.arange(c)[:, None]
    s = np.arange(c)[None, :]
    masks = [t == s]
    blk = 2
    while blk <= c:
        masks.append((t // blk == s // blk) & (t % blk >= blk // 2) & (s % blk < blk // 2))
        blk *= 2
    return np.stack(masks).astype(np.float32)


def _retention_tables(c):
    idx = np.arange(c, dtype=np.float64)
    lg = np.log(1.0 - 2.0 ** (-5.0 - np.arange(N_HEADS, dtype=np.float64)))[:, None, None]
    rel = idx[:, None] - idx[None, :]
    dm = np.where(rel >= 0, np.exp(np.maximum(rel, 0.0)[None] * lg), 0.0)
    q_in = np.broadcast_to(np.exp((idx + 1.0)[None, :, None] * lg), (N_HEADS, c, HEAD_DIM))
    k_out = np.broadcast_to(np.exp((c - 1.0 - idx)[None, :, None] * lg), (N_HEADS, c, HEAD_DIM))
    g_chunk = tuple(float(g) for g in np.exp(c * lg[:, 0, 0]))
    return dm.astype(np.float32), q_in.astype(np.float32), k_out.astype(np.float32), g_chunk


def _level_ref(G, blk, row):
    c, d = G.shape
    if blk == 2:
        return jnp.where((row & 1) == 1, pltpu.roll(G, 1, 0), G)
    if blk == 4:
        r = row & 3
        return jnp.where(r == 0, pltpu.roll(G, c - 1, 0),
                         jnp.where(r == 1, G, jnp.where(r == 2, pltpu.roll(G, 1, 0), pltpu.roll(G, 2, 0))))
    mid = blk // 2 - 1
    G3 = G.reshape(c // blk, blk, d)
    return jnp.broadcast_to(G3[:, mid:mid + 1, :], (c // blk, blk, d)).reshape(c, d)


def _mixer_kernel(*refs, tt, ck, bb, n_t, n_alias, g_chunk):
    (x_ref, nw_ref, w_ref, cos_ref, sin_ref, lvl_ref, dm_ref, qin_ref, kout_ref,
     sh0_ref, sr0_ref, c0_ref, n0_ref, m0_ref, cb0_ref,
     prm_ref, convw_ref, mq_ref, mk_ref, mv_ref, wg_ref, wgt_ref, bg_ref, bgt_ref) = refs[:24]
    o_ref, sh_ref, sr_ref, c_ref, n_ref, m_ref, cb_ref = refs[24 + n_alias:31 + n_alias]
    z_refs = refs[31 + n_alias:-1]
    ext_ref = refs[-1]
    ti = pl.program_id(1)
    d, W, H = HEAD_DIM, MIX_WIDTH, N_HEADS
    n_lvl = lvl_ref.shape[0]

    @pl.when(ti == 0)
    def _():
        def init(bi, carry):
            for h in range(H):
                sh_ref[bi, h] = sh0_ref[bi, h].T
            return carry
        lax.fori_loop(0, bb, init, 0)
        sr_ref[...] = sr0_ref[...]
        c_ref[...] = c0_ref[...]
        n_ref[...] = n0_ref[...]
        m_ref[...] = m0_ref[...]
        cb_ref[...] = cb0_ref[...]

    x = x_ref[...].reshape(bb * tt, x_ref.shape[-1])
    hn = _rms(x, nw_ref[...]).astype(BF16)
    for j, zj_ref in enumerate(z_refs):
        zj_ref[...] = jnp.dot(hn, w_ref[:, j * W:(j + 1) * W], preferred_element_type=F32)

    def prm(i, h):
        return prm_ref[i:i + 1, h * d:(h + 1) * d]

    t_i = lax.broadcasted_iota(jnp.int32, (ck, ck), 0)
    s_i = lax.broadcasted_iota(jnp.int32, (ck, ck), 1)
    causal = t_i >= s_i
    tri_l = causal.astype(BF16)
    tri_u = (t_i <= s_i).astype(BF16)
    row = lax.broadcasted_iota(jnp.int32, (ck, d), 0)

    def chunk(bi, c0):
        if isinstance(bi, int):
            rows = slice(bi * tt + c0, bi * tt + c0 + ck)
        else:
            rows = pl.ds(pl.multiple_of(bi * tt + c0, ck), ck)
        cos = cos_ref[c0:c0 + ck, :]
        sin = sin_ref[c0:c0 + ck, :]

        def rotary(a_):
            return a_ * cos + pltpu.roll(a_, d // 2, 1) * sin

        def zc(j, h):
            return z_refs[j][rows, h * d:(h + 1) * d]

        def put(j, h, val):
            o_ref[bi, c0:c0 + ck, j * W + h * d:j * W + (h + 1) * d] = val

        fpre = z_refs[1][rows, :]
        e = jnp.exp(-jnp.abs(fpre))
        a = prm_ref[0:1, :]
        b = prm_ref[1:2, :] + (jnp.minimum(fpre, 0.0) - jnp.log(1.0 + e))
        logf = jnp.maximum(a, b) + jnp.log(1.0 + jnp.exp(-jnp.abs(a - b)))
        k_all = prm_ref[2:3, :] * (jnp.where(fpre >= 0.0, e, 1.0) / (1.0 + e))
        G_all = _mask_dot(tri_l, logf)
        for h in range(H):
            hs = slice(h * d, (h + 1) * d)
            q, v, gate = zc(0, h), zc(2, h), zc(3, h)
            k, G = k_all[:, hs], G_all[:, hs]
            q_bf, k_bf = q.astype(BF16), k.astype(BF16)
            A = lvl_ref[0] * _dot_nt(q_bf, k_bf)
            for j in range(1, n_lvl):
                E = jnp.exp(-jnp.abs(G - _level_ref(G, 1 << j, row))).astype(BF16)
                A = A + lvl_ref[j] * _dot_nt(q_bf * E, k_bf * E)
            st = sh_ref[bi, h]
            o_h = _dot(A, v) + _dot_nt(q * jnp.exp(G), st)
            GL = G[ck - 1:ck]
            sh_ref[bi, h] = jnp.exp(GL) * st + _dot(v.T, k * jnp.exp(GL - G))
            put(0, h, _head_rmsnorm(o_h, prm(3, h)) * _silu(gate))

        for h in range(H):
            q = rotary(zc(4, h))
            k = rotary(zc(5, h)) * (d ** -0.5)
            v, gate = zc(6, h), zc(7, h)
            A = _dot_nt(q, k) * dm_ref[h]
            S = sr_ref[bi, h]
            o_h = _dot(A, v) + qin_ref[h] * _dot(q, S)
            sr_ref[bi, h] = g_chunk[h] * S + _dot_tn(k * kout_ref[h], v)
            put(1, h, _head_layernorm(o_h, prm(4, h)) * _silu(gate))

        u = z_refs[8][rows, :]
        ext_ref[SUBLANES - (CONV_W - 1):SUBLANES, :] = cb_ref[bi]
        ext_ref[SUBLANES:SUBLANES + ck, :] = u
        conv = jnp.zeros((ck, W), F32)
        for j in range(CONV_W):
            off = SUBLANES - (CONV_W - 1) + j
            conv = conv + convw_ref[j:j + 1, :] * ext_ref[off:off + ck, :]
        cb_ref[bi] = ext_ref[SUBLANES + ck - (CONV_W - 1):SUBLANES + ck, :]
        uc = _silu(conv + prm_ref[7:8, :])

        qs, ks, vs = [], [], []
        for h in range(H):
            uch = uc[:, h * d:(h + 1) * d].astype(BF16)
            qs.append(jnp.dot(uch, mq_ref[h], preferred_element_type=F32))
            ks.append(jnp.dot(uch, mk_ref[h], preferred_element_type=F32))
            vs.append(jnp.dot(u[:, h * d:(h + 1) * d].astype(BF16), mv_ref[h], preferred_element_type=F32))
        qkv = jnp.concatenate(qs + ks + vs, axis=1).astype(BF16)
        g_col = jnp.dot(qkv, wg_ref[...], preferred_element_type=F32) + bg_ref[...]
        g_row = lax.dot_general(wgt_ref[...], qkv, (((1,), (1,)), ((), ())),
                                preferred_element_type=F32) + bgt_ref[...]
        i_cols = g_col[:, :H]
        f_cols = _mask_dot(tri_l, _log_sigmoid(g_col[:, H:]))
        i_rows = g_row[:H]
        f_rows = _dot_mask(_log_sigmoid(g_row[H:]), tri_u)
        m_prev_all = m_ref[bi]
        for h in range(H):
            q, k, v = qs[h], ks[h] * (d ** -0.5), vs[h]
            i_col, F_col = i_cols[:, h:h + 1], f_cols[:, h:h + 1]
            a_row = i_rows[h:h + 1] - f_rows[h:h + 1]
            m_prev = m_prev_all[:, h:h + 1]
            cm = jnp.max(jnp.where(causal, a_row, NEG_INF), -1, keepdims=True)
            m_col = F_col + jnp.maximum(m_prev, cm)
            logd = (F_col - m_col) + a_row
            dmat = jnp.where(causal, jnp.exp(jnp.where(causal, logd, 0.0)), 0.0)
            Sc = _dot_nt(q, k) * dmat
            inter = jnp.exp(F_col + m_prev - m_col)
            Cst = c_ref[bi, h]
            n_row = n_ref[bi, h:h + 1, :]
            num = _dot(Sc, v) + inter * _dot(q, Cst)
            den = jnp.sum(Sc, -1, keepdims=True) + inter * jnp.sum(q * n_row, -1, keepdims=True)
            hout = num / jnp.maximum(jnp.abs(den), jnp.exp(-m_col))
            mL = m_col[ck - 1:ck]
            FL = F_col[ck - 1:ck]
            kw = k * jnp.exp(FL - F_col + i_col - mL)
            decay = jnp.exp(FL + m_prev - mL)
            c_ref[bi, h] = decay * Cst + _dot_tn(kw, v)
            n_ref[bi, h:h + 1, :] = decay * n_row + jnp.sum(kw, 0, keepdims=True)
            m_ref[bi, :, h:h + 1] = mL
            y = _head_layernorm(hout, prm(5, h)) + prm(6, h) * uc[:, h * d:(h + 1) * d]
            put(2, h, y * _silu(zc(9, h)))

    def sequence(bi):
        for c in range(tt // ck):
            chunk(bi, c * ck)

    if bb == 1:
        sequence(0)
    else:
        def per_batch(bi, carry):
            sequence(bi)
            return carry
        lax.fori_loop(0, bb, per_batch, 0, unroll=2)

    @pl.when(ti == n_t - 1)
    def _():
        def fin(bi, carry):
            for h in range(H):
                sh_ref[bi, h] = sh_ref[bi, h].T
            return carry
        lax.fori_loop(0, bb, fin, 0)


def _mixers(x, cos, sin, states_in, l, prev_out, p):
    B, T, D = x.shape
    depth = states_in[0].shape[0]
    tt = _pick(T, (256, 128, 64, 32, 16, 8))
    ck = min(tt, 128)
    bb = _pick(B, tuple(c for c in (8, 4, 2) if c * tt <= 128) + (1,))
    n_t = T // tt
    H, d, W = N_HEADS, HEAD_DIM, MIX_WIDTH
    dm, q_in, k_out, g_chunk = _retention_tables(ck)
    consts = [jnp.asarray(_level_masks(ck)), jnp.asarray(dm), jnp.asarray(q_in), jnp.asarray(k_out)]
    st_spec = pl.BlockSpec((None, bb, H, d, d), lambda b, t: (l, b, 0, 0, 0))
    n_spec = pl.BlockSpec((None, bb, H, d), lambda b, t: (l, b, 0, 0))
    m_spec = pl.BlockSpec((None, bb, 1, H), lambda b, t: (l, b, 0, 0))
    cb_spec = pl.BlockSpec((None, bb, CONV_W - 1, W), lambda b, t: (l, b, 0, 0))
    state_specs = [st_spec, st_spec, st_spec, n_spec, m_spec, cb_spec]

    def full(a):
        nd = a.ndim
        return pl.BlockSpec(a.shape, lambda b, t: (0,) * nd)

    def layer(a):
        nd = a.ndim - 1
        return pl.BlockSpec((None,) + a.shape[1:], lambda b, t: (l,) + (0,) * nd)

    weights = [p['prm'], p['conv_w'], p['mq'], p['mk'], p['mv'], p['m_wg'], p['m_wgt'], p['m_bg'], p['m_bgt']]
    alias_in = list(prev_out) if prev_out is not None else []
    n_fixed = 5 + len(consts) + 6 + len(weights)
    state_shapes = [jax.ShapeDtypeStruct((depth, B, H, d, d), F32)] * 3 + [
        jax.ShapeDtypeStruct((depth, B, H, d), F32),
        jax.ShapeDtypeStruct((depth, B, 1, H), F32),
        jax.ShapeDtypeStruct((depth, B, CONV_W - 1, W), F32)]
    outs = pl.pallas_call(
        functools.partial(_mixer_kernel, tt=tt, ck=ck, bb=bb, n_t=n_t, n_alias=len(alias_in), g_chunk=g_chunk),
        grid=(B // bb, n_t),
        in_specs=[pl.BlockSpec((bb, tt, D), lambda b, t: (b, t, 0)),
                  pl.BlockSpec((None, 1, D), lambda b, t: (l, 0, 0)),
                  pl.BlockSpec((None, D, SCAN_COLS), lambda b, t: (l, 0, 0), pipeline_mode=pl.Buffered(1)),
                  pl.BlockSpec((tt, d), lambda b, t: (t, 0)),
                  pl.BlockSpec((tt, d), lambda b, t: (t, 0))]
                 + [full(c) for c in consts] + state_specs + [layer(w) for w in weights]
                 + [pl.BlockSpec(memory_space=pl.ANY)] * len(alias_in),
        out_specs=[pl.BlockSpec((bb, tt, N_BRANCH * W), lambda b, t: (b, t, 0))] + state_specs,
        out_shape=[jax.ShapeDtypeStruct((B, T, N_BRANCH * W), F32)] + state_shapes,
        input_output_aliases={n_fixed + i: 1 + i for i in range(len(alias_in))},
        scratch_shapes=[pltpu.VMEM((bb * tt, W), F32)] * (SCAN_COLS // W) + [pltpu.VMEM((SUBLANES + ck, W), F32)],
        name="mixers",
        compiler_params=_cparams(("parallel", "arbitrary")),
    )(x, p['norm_mix'], p['w_in'], cos, sin, *consts, *states_in, *weights, *alias_in)
    return outs[0], tuple(outs[1:])


def _merge_kernel(x_ref, o_ref, nw_ref, wg0_ref, wg1_ref, wg2_ref, wbr_ref, wout_ref, y_ref):
    x = x_ref[...]
    W = MIX_WIDTH
    hn = _rms(x, nw_ref[...]).astype(BF16)
    merged = jnp.zeros(x.shape, F32)
    for n, wg_ref in enumerate((wg0_ref, wg1_ref, wg2_ref)):
        gz = jnp.dot(hn, wg_ref[...], preferred_element_type=F32)
        proj = jnp.dot(o_ref[:, n * W:(n + 1) * W].astype(BF16), wbr_ref[n], preferred_element_type=F32)
        merged = merged + _sigmoid(gz) * proj
    y_ref[...] = x + jnp.dot(merged.astype(BF16), wout_ref[...], preferred_element_type=F32)


def _merge(x, o, nw, w_in, wbr, wout, l):
    n, d = x.shape
    tm = _pick(n, (512, 256, 128))
    g0 = SCAN_COLS // d

    def gate_spec(k):
        return pl.BlockSpec((None, d, d), lambda i: (l, 0, g0 + k))

    return pl.pallas_call(
        _merge_kernel,
        grid=(n // tm,),
        in_specs=[pl.BlockSpec((tm, d), lambda i: (i, 0)),
                  pl.BlockSpec((tm, N_BRANCH * MIX_WIDTH), lambda i: (i, 0)),
                  pl.BlockSpec((None, 1, d), lambda i: (l, 0, 0)),
                  gate_spec(0), gate_spec(1), gate_spec(2),
                  pl.BlockSpec((None,) + wbr.shape[1:], lambda i: (l, 0, 0, 0)),
                  pl.BlockSpec((None,) + wout.shape[1:], lambda i: (l, 0, 0))],
        out_specs=pl.BlockSpec((tm, d), lambda i: (i, 0)),
        out_shape=jax.ShapeDtypeStruct((n, d), F32),
        name="merge",
        compiler_params=_cparams(("parallel",)),
    )(x, o, nw, w_in, w_in, w_in, wbr, wout)


def _router_kernel(x_ref, nw_ref, wr_ref, br_ref, mi_ref, mf_ref, cnt_ref, carry_ref):
    i = pl.program_id(0)
    tm = x_ref.shape[0]

    @pl.when(i == 0)
    def _():
        carry_ref[...] = jnp.zeros(carry_ref.shape, F32)

    hn = _rms(x_ref[...], nw_ref[...])
    logits = _dot(hn, wr_ref[...]) + br_ref[...]
    lane = lax.broadcasted_iota(jnp.int32, logits.shape, 1)
    big = jnp.int32(1 << 20)

    def first_max(mask):
        vmax = jnp.max(jnp.where(mask, logits, NEG_INF), -1, keepdims=True)
        imax = jnp.min(jnp.where(mask, jnp.where(logits == vmax, lane, big), big), -1, keepdims=True)
        return vmax, imax

    in_groups = lane < N_GROUPS
    g_max, g_top = first_max(in_groups)
    g_w = 1.0 / jnp.sum(jnp.where(in_groups, jnp.exp(logits - g_max), 0.0), -1, keepdims=True)
    e_lo = N_GROUPS + EXPERTS_PER_GROUP * g_top
    in_e = jnp.logical_and(lane >= e_lo, lane < e_lo + EXPERTS_PER_GROUP)
    v1, i1 = first_max(in_e)
    v2, i2 = first_max(jnp.logical_and(in_e, lane != i1))
    e2 = jnp.exp(v2 - v1)
    w1 = g_w / (1.0 + e2)
    w2 = g_w * e2 / (1.0 + e2)

    hit1 = lane == i1
    hit2 = lane == i2
    onehot = jnp.where(hit1, 1.0, 0.0) + jnp.where(hit2, 1.0, 0.0)
    strict = (lax.broadcasted_iota(jnp.int32, (tm, tm), 0) > lax.broadcasted_iota(jnp.int32, (tm, tm), 1))
    before = _dot(strict.astype(F32), onehot) + carry_ref[...]
    r1 = jnp.sum(jnp.where(hit1, before, 0.0), -1, keepdims=True).astype(jnp.int32)
    r2 = jnp.sum(jnp.where(hit2, before, 0.0), -1, keepdims=True).astype(jnp.int32)
    carry_ref[...] = carry_ref[...] + jnp.sum(onehot, 0, keepdims=True)
    cnt_ref[...] = carry_ref[...]

    meta = jnp.where(lane == 0, i1 - N_GROUPS,
                     jnp.where(lane == 1, i2 - N_GROUPS,
                               jnp.where(lane == 2, r1, jnp.where(lane == 3, r2, 0))))
    mi_ref[...] = meta.T[:SUBLANES]
    mf_ref[...] = jnp.where(lane == 0, w1, jnp.where(lane == 1, w2, 0.0))


def _router(x, nw, wr, br, l):
    n, d = x.shape
    tm = _pick(n, (256, 128))
    return pl.pallas_call(
        _router_kernel,
        grid=(n // tm,),
        in_specs=[pl.BlockSpec((tm, d), lambda i: (i, 0)),
                  pl.BlockSpec((None, 1, d), lambda i: (l, 0, 0)),
                  pl.BlockSpec((None, d, LANES), lambda i: (l, 0, 0)),
                  pl.BlockSpec((None, 1, LANES), lambda i: (l, 0, 0))],
        out_specs=[pl.BlockSpec((SUBLANES, tm), lambda i: (0, i)),
                   pl.BlockSpec((tm, LANES), lambda i: (i, 0)),
                   pl.BlockSpec((1, LANES), lambda i: (0, 0))],
        out_shape=[jax.ShapeDtypeStruct((SUBLANES, n), jnp.int32),
                   jax.ShapeDtypeStruct((n, LANES), F32),
                   jax.ShapeDtypeStruct((1, LANES), F32)],
        scratch_shapes=[pltpu.VMEM((1, LANES), F32)],
        name="router",
        compiler_params=_cparams(("arbitrary",)),
    )(x, nw, wr, br)


def _dispatch_kernel(d0_ref, d1_ref, x_ref, nw_ref, xs_in_ref, xs_ref, buf, sem):
    del xs_in_ref
    i = pl.program_id(0)
    n_i = pl.num_programs(0)
    tm = x_ref.shape[0]
    n_sub = x_ref.shape[1] // LANES
    slot = i % 2

    def copies(step, sl, r):
        t = step * tm + r
        src = buf.at[sl, pl.ds(pl.multiple_of(r * n_sub, n_sub), n_sub)]
        return (pltpu.make_async_copy(src, xs_ref.at[d0_ref[t]], sem.at[sl]),
                pltpu.make_async_copy(src, xs_ref.at[d1_ref[t]], sem.at[sl]))

    def wait_step(step, sl):
        def body(r, carry):
            for cp in copies(step, sl, r):
                cp.wait()
            return carry
        lax.fori_loop(0, tm, body, 0)

    @pl.when(i >= 2)
    def _():
        wait_step(i - 2, slot)

    hn = _rms(x_ref[...], nw_ref[...])
    for s in range(n_sub):
        buf[slot, pl.ds(s, tm, stride=n_sub), :] = hn[:, s * LANES:(s + 1) * LANES]

    def start(r, carry):
        for cp in copies(i, slot, r):
            cp.start()
        return carry

    lax.fori_loop(0, tm, start, 0)

    @pl.when(i == n_i - 1)
    def _():
        @pl.when(i >= 1)
        def _():
            wait_step(i - 1, 1 - slot)
        wait_step(i, slot)


def _dispatch(d0, d1, x, nw, l, cap):
    n, d = x.shape
    n_sub = d // LANES
    tm = _pick(n, (256, 128))
    xs0 = jnp.zeros((cap, n_sub, LANES), F32)
    return pl.pallas_call(
        _dispatch_kernel,
        grid_spec=pltpu.PrefetchScalarGridSpec(
            num_scalar_prefetch=2,
            grid=(n // tm,),
            in_specs=[pl.BlockSpec((tm, d), lambda i, a, b: (i, 0)),
                      pl.BlockSpec((None, 1, d), lambda i, a, b: (l, 0, 0)),
                      pl.BlockSpec(memory_space=pl.ANY)],
            out_specs=pl.BlockSpec(memory_space=pl.ANY),
            scratch_shapes=[pltpu.VMEM((2, tm * n_sub, LANES), F32), pltpu.SemaphoreType.DMA((2,))]),
        out_shape=jax.ShapeDtypeStruct((cap, n_sub, LANES), F32),
        input_output_aliases={4: 0},
        name="dispatch",
        compiler_params=pltpu.CompilerParams(dimension_semantics=("arbitrary",), vmem_limit_bytes=VMEM_LIMIT,
                                             has_side_effects=True),
    )(d0, d1, x, nw, xs0)


def _experts_kernel(be_ref, nu_ref, xs_ref, wg_ref, wu_ref, wd_ref, ys_ref, wg_bf, wu_bf, wd_bf, *, n_sub):
    i = pl.program_id(0)
    rows = xs_ref.shape[0] // n_sub

    @pl.when(i < nu_ref[0])
    def _():
        changed = jnp.logical_or(i == 0, be_ref[i] != be_ref[jnp.maximum(i - 1, 0)])

        @pl.when(changed)
        def _():
            wg_bf[...] = wg_ref[...].astype(BF16)
            wu_bf[...] = wu_ref[...].astype(BF16)
            wd_bf[...] = wd_ref[...].astype(BF16)

        x = jnp.concatenate([xs_ref[pl.ds(s, rows, stride=n_sub), :] for s in range(n_sub)], axis=1).astype(BF16)
        g = jnp.dot(x, wg_bf[...], preferred_element_type=F32)
        u = jnp.dot(x, wu_bf[...], preferred_element_type=F32)
        y = jnp.dot((_silu(g) * u).astype(BF16), wd_bf[...], preferred_element_type=F32)
        for s in range(n_sub):
            ys_ref[pl.ds(s, rows, stride=n_sub), :] = y[:, s * LANES:(s + 1) * LANES]

    @pl.when(i >= nu_ref[0])
    def _():
        ys_ref[...] = jnp.zeros(ys_ref.shape, F32)


def _experts(block_e, n_used, xs, w_g, w_u, w_d, l):
    cap, n_sub, _ = xs.shape
    _, _, d, f = w_g.shape
    n_blocks = cap // MOE_ROWS
    blk = MOE_ROWS * n_sub
    ys = pl.pallas_call(
        functools.partial(_experts_kernel, n_sub=n_sub),
        grid_spec=pltpu.PrefetchScalarGridSpec(
            num_scalar_prefetch=2,
            grid=(n_blocks,),
            in_specs=[pl.BlockSpec((blk, LANES), lambda i, be, nu: (i, 0)),
                      pl.BlockSpec((None, None, d, f), lambda i, be, nu: (l, be[i], 0, 0)),
                      pl.BlockSpec((None, None, d, f), lambda i, be, nu: (l, be[i], 0, 0)),
                      pl.BlockSpec((None, None, f, d), lambda i, be, nu: (l, be[i], 0, 0))],
            out_specs=pl.BlockSpec((blk, LANES), lambda i, be, nu: (i, 0)),
            scratch_shapes=[pltpu.VMEM((d, f), BF16), pltpu.VMEM((d, f), BF16), pltpu.VMEM((f, d), BF16)]),
        out_shape=jax.ShapeDtypeStruct((cap * n_sub, LANES), F32),
        name="experts",
        compiler_params=_cparams(("arbitrary",)),
    )(block_e, n_used, xs.reshape(cap * n_sub, LANES), w_g, w_u, w_d)
    return ys.reshape(cap, n_sub, LANES)


def _combine_kernel(d0_ref, d1_ref, x_ref, mf_ref, nw_ref, ys_ref, y_ref, buf, sem, *, final_norm):
    i = pl.program_id(0)
    tm = x_ref.shape[0]
    n_sub = x_ref.shape[1] // LANES
    base = i * tm

    def copies(r):
        dst = pl.ds(pl.multiple_of(r * n_sub, n_sub), n_sub)
        return (pltpu.make_async_copy(ys_ref.at[d0_ref[base + r]], buf.at[0, dst], sem),
                pltpu.make_async_copy(ys_ref.at[d1_ref[base + r]], buf.at[1, dst], sem))

    def start(r, carry):
        for cp in copies(r):
            cp.start()
        return carry

    def wait(r, carry):
        for cp in copies(r):
            cp.wait()
        return carry

    lax.fori_loop(0, tm, start, 0)
    lax.fori_loop(0, tm, wait, 0)

    w = mf_ref[...]
    y0 = jnp.concatenate([buf[0, pl.ds(s, tm, stride=n_sub), :] for s in range(n_sub)], axis=1)
    y1 = jnp.concatenate([buf[1, pl.ds(s, tm, stride=n_sub), :] for s in range(n_sub)], axis=1)
    out = x_ref[...] + (y0 * w[:, 0:1] + y1 * w[:, 1:2])
    if final_norm:
        out = _rms(out, nw_ref[...])
    y_ref[...] = out


def _combine(d0, d1, x, mf, nw, ys, final_norm):
    n, d = x.shape
    n_sub = d // LANES
    tm = _pick(n, (256, 128))
    return pl.pallas_call(
        functools.partial(_combine_kernel, final_norm=final_norm),
        grid_spec=pltpu.PrefetchScalarGridSpec(
            num_scalar_prefetch=2,
            grid=(n // tm,),
            in_specs=[pl.BlockSpec((tm, d), lambda i, a, b: (i, 0)),
                      pl.BlockSpec((tm, LANES), lambda i, a, b: (i, 0)),
                      pl.BlockSpec((1, d), lambda i, a, b: (0, 0)),
                      pl.BlockSpec(memory_space=pl.ANY)],
            out_specs=pl.BlockSpec((tm, d), lambda i, a, b: (i, 0)),
            scratch_shapes=[pltpu.VMEM((TOP_K, tm * n_sub, LANES), F32), pltpu.SemaphoreType.DMA(())]),
        out_shape=jax.ShapeDtypeStruct((n, d), F32),
        name="combine",
        compiler_params=_cparams(("arbitrary",)),
    )(d0, d1, x, mf, nw, ys)


def _moe(x, l, p, final_nw):
    n, d = x.shape
    mi, mf, cnt = _router(x, p['norm_ffn'], p['w_r'], p['b_r'], l)
    counts = cnt[0, N_GROUPS:N_GROUPS + N_EXPERTS].astype(jnp.int32)
    padded = (counts + MOE_ROWS - 1) // MOE_ROWS * MOE_ROWS
    pad_end = jnp.cumsum(padded)
    pad_start = pad_end - padded
    experts = jnp.arange(N_EXPERTS, dtype=jnp.int32)

    def dest(e, r):
        return r + jnp.sum(jnp.where(e[:, None] == experts[None, :], pad_start[None, :], 0), axis=1)

    d0 = dest(mi[0], mi[2])
    d1 = dest(mi[1], mi[3])
    n_blocks = -(-(n * TOP_K) // MOE_ROWS) + N_EXPERTS
    block_row = jnp.arange(n_blocks, dtype=jnp.int32) * MOE_ROWS
    block_e = jnp.minimum(jnp.sum((pad_end[None, :] <= block_row[:, None]).astype(jnp.int32), axis=1), N_EXPERTS - 1)
    n_used = (pad_end[-1:] // MOE_ROWS).astype(jnp.int32)
    xs = _dispatch(d0, d1, x, p['norm_ffn'], l, n_blocks * MOE_ROWS)
    ys = _experts(block_e, n_used, xs, p['w_eg'], p['w_eu'], p['w_ed'], l)
    nw = (final_nw if final_nw is not None else p['norm_ffn'][l, 0])[None]
    return _combine(d0, d1, x, mf, nw, ys, final_nw is not None)


def _trunk(x, pos, states, p):
    B, T, D = x.shape
    depth = p['w_in'].shape[0]
    half = HEAD_DIM // 2
    inv = ROPE_BASE ** (-jnp.arange(half, dtype=F32) / half)
    ang = pos.astype(F32)[:, None] * inv[None, :]
    cos = jnp.concatenate([jnp.cos(ang), jnp.cos(ang)], -1)
    sin = jnp.concatenate([-jnp.sin(ang), jnp.sin(ang)], -1)
    xf = x.reshape(B * T, D)
    new_states = None
    for l in range(depth):
        o, new_states = _mixers(xf.reshape(B, T, D), cos, sin, states, l, new_states, p)
        xf = _merge(xf, o.reshape(B * T, N_BRANCH * MIX_WIDTH), p['norm_mix'], p['w_in'], p['w_branch'], p['w_out'], l)
        xf = _moe(xf, l, p, p['norm_final'] if l == depth - 1 else None)
    sh, sr, c, n, m, cb = new_states
    return xf.reshape(B, T, D), (sh, sr, c, n, m.reshape(depth, B, N_HEADS), cb)


def kernel(x_prompt, x_sample, state_hgrn, state_ret, state_mlstm_C, state_mlstm_n, state_mlstm_m, state_mlstm_conv,
           norm_mix, norm_ffn, norm_final, w_in, hgrn_lb, hgrn_norm, ret_norm, mlstm_conv_w, mlstm_conv_b,
           mlstm_wq, mlstm_wk, mlstm_wv, mlstm_w_gates, mlstm_b_gates, mlstm_norm, mlstm_skip, w_branch, w_out,
           w_router_group, b_router_group, w_router_expert, b_router_expert, w_exp_gate, w_exp_up, w_exp_down):
    depth, D = norm_mix.shape
    H, d, W = N_HEADS, HEAD_DIM, MIX_WIDTH
    lb = jnp.cumsum(jax.nn.softmax(hgrn_lb.astype(F32), axis=0), axis=0)
    lb = lb - lb[0:1]
    prm = jnp.stack([jnp.log(lb), jnp.log1p(-lb), 1.0 - lb, hgrn_norm.astype(F32), ret_norm.astype(F32),
                     mlstm_norm.astype(F32), mlstm_skip.astype(F32), mlstm_conv_b.astype(F32)], axis=1)
    pad = LANES - N_GROUPS - N_EXPERTS
    w_r = jnp.concatenate([w_router_group, w_router_expert, jnp.zeros((depth, D, pad), F32)], -1)
    b_r = jnp.concatenate([b_router_group, b_router_expert, jnp.zeros((depth, pad), F32)], -1)[:, None, :]
    wg_bf = mlstm_w_gates.astype(BF16)
    p = {'norm_mix': norm_mix[:, None, :], 'norm_ffn': norm_ffn[:, None, :], 'norm_final': norm_final,
         'w_in': w_in.astype(BF16), 'prm': prm, 'conv_w': mlstm_conv_w.astype(F32),
         'mq': mlstm_wq.astype(BF16), 'mk': mlstm_wk.astype(BF16), 'mv': mlstm_wv.astype(BF16),
         'm_wg': wg_bf, 'm_wgt': jnp.swapaxes(wg_bf, 1, 2),
         'm_bg': mlstm_b_gates[:, None, :], 'm_bgt': mlstm_b_gates[:, :, None],
         'w_branch': w_branch.astype(BF16), 'w_out': w_out.astype(BF16),
         'w_r': w_r, 'b_r': b_r, 'w_eg': w_exp_gate, 'w_eu': w_exp_up, 'w_ed': w_exp_down}

    Bp, Tp = x_prompt.shape[0], x_prompt.shape[1]
    Bs = x_sample.shape[0]
    zero_states = (jnp.zeros((depth, Bp, H, d, d), F32), jnp.zeros((depth, Bp, H, d, d), F32),
                   jnp.zeros((depth, Bp, H, d, d), F32), jnp.zeros((depth, Bp, H, d), F32),
                   jnp.zeros((depth, Bp, 1, H), F32), jnp.zeros((depth, Bp, CONV_W - 1, W), F32))
    pos_prompt = jnp.arange(Tp, dtype=jnp.int32)
    pos_sample = PAST_LEN + jnp.arange(x_sample.shape[1], dtype=jnp.int32)
    y_prompt, ps = _trunk(x_prompt, pos_prompt, zero_states, p)
    sample_states = (state_hgrn, state_ret, state_mlstm_C, state_mlstm_n,
                     state_mlstm_m.reshape(depth, Bs, 1, H), state_mlstm_conv)
    y_sample, ss = _trunk(x_sample, pos_sample, sample_states, p)
    return (y_prompt, y_sample) + ps + ss
```

```python
import functools

import jax
import jax.numpy as jnp
import numpy as np
from jax import lax
from jax.experimental import pallas as pl
from jax.experimental.pallas import tpu as pltpu

F32 = jnp.float32
BF16 = jnp.bfloat16
HIGHEST = lax.Precision.HIGHEST

HEAD_DIM = 128
N_HEADS = 4
MIX_WIDTH = HEAD_DIM * N_HEADS
N_BRANCH = 3
CONV_W = 4
ROPE_BASE = 10000.0
N_GROUPS = 4
EXPERTS_PER_GROUP = 8
N_EXPERTS = N_GROUPS * EXPERTS_PER_GROUP
TOP_K = 2
PAST_LEN = 16384
NORM_EPS = 1e-6
HEAD_NORM_EPS = 1e-5
SCAN_COLS = 10 * MIX_WIDTH

LANES = 128
SUBLANES = 8
VMEM_LIMIT = 48 * 1024 * 1024
MOE_ROWS = 128
NEG_INF = float("-inf")


def _cparams(sem):
    return pltpu.CompilerParams(dimension_semantics=sem, vmem_limit_bytes=VMEM_LIMIT)


def _pick(n, cands):
    for c in cands:
        if n % c == 0:
            return c
    return n


def _rms(x, w):
    return x * lax.rsqrt(jnp.mean(x * x, -1, keepdims=True) + NORM_EPS) * w


def _sigmoid(x):
    return 1.0 / (1.0 + jnp.exp(-x))


def _silu(x):
    return x * _sigmoid(x)


def _log_sigmoid(x):
    return jnp.minimum(x, 0.0) - jnp.log(1.0 + jnp.exp(-jnp.abs(x)))


def _dot(a, b):
    return jnp.dot(a.astype(BF16), b.astype(BF16), preferred_element_type=F32)


def _dot_nt(a, b):
    return lax.dot_general(a.astype(BF16), b.astype(BF16), (((1,), (1,)), ((), ())), preferred_element_type=F32)


def _dot_tn(a, b):
    return jnp.dot(a.T.astype(BF16), b.astype(BF16), preferred_element_type=F32)


def _split3(x):
    hi = x.astype(BF16)
    r = x - hi.astype(F32)
    mid = r.astype(BF16)
    lo = (r - mid.astype(F32)).astype(BF16)
    return hi, mid, lo


def _mask_dot(mask_bf, x):
    return sum(jnp.dot(mask_bf, part, preferred_element_type=F32) for part in _split3(x))


def _dot_mask(x, mask_bf):
    return sum(jnp.dot(part, mask_bf, preferred_element_type=F32) for part in _split3(x))


def _head_rmsnorm(o, w):
    return o * lax.rsqrt(jnp.mean(o * o, -1, keepdims=True) + HEAD_NORM_EPS) * w


def _head_layernorm(o, w):
    c = o - jnp.mean(o, -1, keepdims=True)
    return c * lax.rsqrt(jnp.mean(c * c, -1, keepdims=True) + HEAD_NORM_EPS) * w


def _level_masks(c):
    t = np.arange(c)[:, None]
    s = np.arange(c)[None, :]
    masks = [t == s]
    blk = 2
    while blk <= c:
        masks.append((t // blk == s // blk) & (t % blk >= blk // 2) & (s % blk < blk // 2))
        blk *= 2
    return np.stack(masks).astype(np.float32)


def _retention_tables(c):
    idx = np.arange(c, dtype=np.float64)
    lg = np.log(1.0 - 2.0 ** (-5.0 - np.arange(N_HEADS, dtype=np.float64)))[:, None, None]
    rel = idx[:, None] - idx[None, :]
    dm = np.where(rel >= 0, np.exp(np.maximum(rel, 0.0)[None] * lg), 0.0)
    q_in = np.broadcast_to(np.exp((idx + 1.0)[None, :, None] * lg), (N_HEADS, c, HEAD_DIM))
    k_out = np.broadcast_to(np.exp((c - 1.0 - idx)[None, :, None] * lg), (N_HEADS, c, HEAD_DIM))
    g_chunk = tuple(float(g) for g in np.exp(c * lg[:, 0, 0]))
    return dm.astype(np.float32), q_in.astype(np.float32), k_out.astype(np.float32), g_chunk


def _level_ref(G, blk, row):
    c, d = G.shape
    if blk == 2:
        return jnp.where((row & 1) == 1, pltpu.roll(G, 1, 0), G)
    if blk == 4:
        r = row & 3
        return jnp.where(r == 0, pltpu.roll(G, c - 1, 0),
                         jnp.where(r == 1, G, jnp.where(r == 2, pltpu.roll(G, 1, 0), pltpu.roll(G, 2, 0))))
    mid = blk // 2 - 1
    G3 = G.reshape(c // blk, blk, d)
    return jnp.broadcast_to(G3[:, mid:mid + 1, :], (c // blk, blk, d)).reshape(c, d)


def _mixer_kernel(*refs, tt, ck, bb, n_t, n_alias, g_chunk):
    (x_ref, nw_ref, w_ref, cos_ref, sin_ref, lvl_ref, dm_ref, qin_ref, kout_ref,
     sh0_ref, sr0_ref, c0_ref, n0_ref, m0_ref, cb0_ref,
     prm_ref, convw_ref, mq_ref, mk_ref, mv_ref, wg_ref, wgt_ref, bg_ref, bgt_ref) = refs[:24]
    o_ref, sh_ref, sr_ref, c_ref, n_ref, m_ref, cb_ref = refs[24 + n_alias:31 + n_alias]
    z_refs = refs[31 + n_alias:-1]
    ext_ref = refs[-1]
    ti = pl.program_id(1)
    d, W, H = HEAD_DIM, MIX_WIDTH, N_HEADS
    n_lvl = lvl_ref.shape[0]

    @pl.when(ti == 0)
    def _():
        def init(bi, carry):
            for h in range(H):
                sh_ref[bi, h] = sh0_ref[bi, h].T
            return carry
        lax.fori_loop(0, bb, init, 0)
        sr_ref[...] = sr0_ref[...]
        c_ref[...] = c0_ref[...]
        n_ref[...] = n0_ref[...]
        m_ref[...] = m0_ref[...]
        cb_ref[...] = cb0_ref[...]

    x = x_ref[...].reshape(bb * tt, x_ref.shape[-1])
    hn = _rms(x, nw_ref[...]).astype(BF16)
    for j, zj_ref in enumerate(z_refs):
        zj_ref[...] = jnp.dot(hn, w_ref[:, j * W:(j + 1) * W], preferred_element_type=F32)

    def prm(i, h):
        return prm_ref[i:i + 1, h * d:(h + 1) * d]

    t_i = lax.broadcasted_iota(jnp.int32, (ck, ck), 0)
    s_i = lax.broadcasted_iota(jnp.int32, (ck, ck), 1)
    causal = t_i >= s_i
    tri_l = causal.astype(BF16)
    tri_u = (t_i <= s_i).astype(BF16)
    row = lax.broadcasted_iota(jnp.int32, (ck, d), 0)

    def chunk(bi, c0):
        if isinstance(bi, int):
            rows = slice(bi * tt + c0, bi * tt + c0 + ck)
        else:
            rows = pl.ds(pl.multiple_of(bi * tt + c0, ck), ck)
        cos = cos_ref[c0:c0 + ck, :]
        sin = sin_ref[c0:c0 + ck, :]

        def rotary(a_):
            return a_ * cos + pltpu.roll(a_, d // 2, 1) * sin

        def zc(j, h):
            return z_refs[j][rows, h * d:(h + 1) * d]

        def put(j, h, val):
            o_ref[bi, c0:c0 + ck, j * W + h * d:j * W + (h + 1) * d] = val

        fpre = z_refs[1][rows, :]
        e = jnp.exp(-jnp.abs(fpre))
        a = prm_ref[0:1, :]
        b = prm_ref[1:2, :] + (jnp.minimum(fpre, 0.0) - jnp.log(1.0 + e))
        logf = jnp.maximum(a, b) + jnp.log(1.0 + jnp.exp(-jnp.abs(a - b)))
        k_all = prm_ref[2:3, :] * (jnp.where(fpre >= 0.0, e, 1.0) / (1.0 + e))
        G_all = _mask_dot(tri_l, logf)
        for h in range(H):
            hs = slice(h * d, (h + 1) * d)
            q, v, gate = zc(0, h), zc(2, h), zc(3, h)
            k, G = k_all[:, hs], G_all[:, hs]
            q_bf, k_bf = q.astype(BF16), k.astype(BF16)
            A = lvl_ref[0] * _dot_nt(q_bf, k_bf)
            for j in range(1, n_lvl):
                E = jnp.exp(-jnp.abs(G - _level_ref(G, 1 << j, row))).astype(BF16)
                A = A + lvl_ref[j] * _dot_nt(q_bf * E, k_bf * E)
            st = sh_ref[bi, h]
            o_h = _dot(A, v) + _dot_nt(q * jnp.exp(G), st)
            GL = G[ck - 1:ck]
            sh_ref[bi, h] = jnp.exp(GL) * st + _dot(v.T, k * jnp.exp(GL - G))
            put(0, h, _head_rmsnorm(o_h, prm(3, h)) * _silu(gate))

        for h in range(H):
            q = rotary(zc(4, h))
            k = rotary(zc(5, h)) * (d ** -0.5)
            v, gate = zc(6, h), zc(7, h)
            A = _dot_nt(q, k) * dm_ref[h]
            S = sr_ref[bi, h]
            o_h = _dot(A, v) + qin_ref[h] * _dot(q, S)
            sr_ref[bi, h] = g_chunk[h] * S + _dot_tn(k * kout_ref[h], v)
            put(1, h, _head_layernorm(o_h, prm(4, h)) * _silu(gate))

        u = z_refs[8][rows, :]
        ext_ref[SUBLANES - (CONV_W - 1):SUBLANES, :] = cb_ref[bi]
        ext_ref[SUBLANES:SUBLANES + ck, :] = u
        conv = jnp.zeros((ck, W), F32)
        for j in range(CONV_W):
            off = SUBLANES - (CONV_W - 1) + j
            conv = conv + convw_ref[j:j + 1, :] * ext_ref[off:off + ck, :]
        cb_ref[bi] = ext_ref[SUBLANES + ck - (CONV_W - 1):SUBLANES + ck, :]
        uc = _silu(conv + prm_ref[7:8, :])

        qs, ks, vs = [], [], []
        for h in range(H):
            uch = uc[:, h * d:(h + 1) * d].astype(BF16)
            qs.append(jnp.dot(uch, mq_ref[h], preferred_element_type=F32))
            ks.append(jnp.dot(uch, mk_ref[h], preferred_element_type=F32))
            vs.append(jnp.dot(u[:, h * d:(h + 1) * d].astype(BF16), mv_ref[h], preferred_element_type=F32))
        qkv = jnp.concatenate(qs + ks + vs, axis=1).astype(BF16)
        g_col = jnp.dot(qkv, wg_ref[...], preferred_element_type=F32) + bg_ref[...]
        g_row = lax.dot_general(wgt_ref[...], qkv, (((1,), (1,)), ((), ())),
                                preferred_element_type=F32) + bgt_ref[...]
        i_cols = g_col[:, :H]
        f_cols = _mask_dot(tri_l, _log_sigmoid(g_col[:, H:]))
        i_rows = g_row[:H]
        f_rows = _dot_mask(_log_sigmoid(g_row[H:]), tri_u)
        m_prev_all = m_ref[bi]
        for h in range(H):
            q, k, v = qs[h], ks[h] * (d ** -0.5), vs[h]
            i_col, F_col = i_cols[:, h:h + 1], f_cols[:, h:h + 1]
            a_row = i_rows[h:h + 1] - f_rows[h:h + 1]
            m_prev = m_prev_all[:, h:h + 1]
            cm = jnp.max(jnp.where(causal, a_row, NEG_INF), -1, keepdims=True)
            m_col = F_col + jnp.maximum(m_prev, cm)
            logd = (F_col - m_col) + a_row
            dmat = jnp.where(causal, jnp.exp(jnp.where(causal, logd, 0.0)), 0.0)
            Sc = _dot_nt(q, k) * dmat
            inter = jnp.exp(F_col + m_prev - m_col)
            Cst = c_ref[bi, h]
            n_row = n_ref[bi, h:h + 1, :]
            num = _dot(Sc, v) + inter * _dot(q, Cst)
            den = jnp.sum(Sc, -1, keepdims=True) + inter * jnp.sum(q * n_row, -1, keepdims=True)
            hout = num / jnp.maximum(jnp.abs(den), jnp.exp(-m_col))
            mL = m_col[ck - 1:ck]
            FL = F_col[ck - 1:ck]
            kw = k * jnp.exp(FL - F_col + i_col - mL)
            decay = jnp.exp(FL + m_prev - mL)
            c_ref[bi, h] = decay * Cst + _dot_tn(kw, v)
            n_ref[bi, h:h + 1, :] = decay * n_row + jnp.sum(kw, 0, keepdims=True)
            m_ref[bi, :, h:h + 1] = mL
            y = _head_layernorm(hout, prm(5, h)) + prm(6, h) * uc[:, h * d:(h + 1) * d]
            put(2, h, y * _silu(zc(9, h)))

    def sequence(bi):
        for c in range(tt // ck):
            chunk(bi, c * ck)

    if bb == 1:
        sequence(0)
    else:
        def per_batch(bi, carry):
            sequence(bi)
            return carry
        lax.fori_loop(0, bb, per_batch, 0, unroll=2)

    @pl.when(ti == n_t - 1)
    def _():
        def fin(bi, carry):
            for h in range(H):
                sh_ref[bi, h] = sh_ref[bi, h].T
            return carry
        lax.fori_loop(0, bb, fin, 0)


def _mixers(x, cos, sin, states_in, l, prev_out, p):
    B, T, D = x.shape
    depth = states_in[0].shape[0]
    tt = _pick(T, (256, 128, 64, 32, 16, 8))
    ck = min(tt, 128)
    bb = _pick(B, tuple(c for c in (8, 4, 2) if c * tt <= 128) + (1,))
    n_t = T // tt
    H, d, W = N_HEADS, HEAD_DIM, MIX_WIDTH
    dm, q_in, k_out, g_chunk = _retention_tables(ck)
    consts = [jnp.asarray(_level_masks(ck)), jnp.asarray(dm), jnp.asarray(q_in), jnp.asarray(k_out)]
    st_spec = pl.BlockSpec((None, bb, H, d, d), lambda b, t: (l, b, 0, 0, 0))
    n_spec = pl.BlockSpec((None, bb, H, d), lambda b, t: (l, b, 0, 0))
    m_spec = pl.BlockSpec((None, bb, 1, H), lambda b, t: (l, b, 0, 0))
    cb_spec = pl.BlockSpec((None, bb, CONV_W - 1, W), lambda b, t: (l, b, 0, 0))
    state_specs = [st_spec, st_spec, st_spec, n_spec, m_spec, cb_spec]

    def full(a):
        nd = a.ndim
        return pl.BlockSpec(a.shape, lambda b, t: (0,) * nd)

    def layer(a):
        nd = a.ndim - 1
        return pl.BlockSpec((None,) + a.shape[1:], lambda b, t: (l,) + (0,) * nd)

    weights = [p['prm'], p['conv_w'], p['mq'], p['mk'], p['mv'], p['m_wg'], p['m_wgt'], p['m_bg'], p['m_bgt']]
    alias_in = list(prev_out) if prev_out is not None else []
    n_fixed = 5 + len(consts) + 6 + len(weights)
    state_shapes = [jax.ShapeDtypeStruct((depth, B, H, d, d), F32)] * 3 + [
        jax.ShapeDtypeStruct((depth, B, H, d), F32),
        jax.ShapeDtypeStruct((depth, B, 1, H), F32),
        jax.ShapeDtypeStruct((depth, B, CONV_W - 1, W), F32)]
    outs = pl.pallas_call(
        functools.partial(_mixer_kernel, tt=tt, ck=ck, bb=bb, n_t=n_t, n_alias=len(alias_in), g_chunk=g_chunk),
        grid=(B // bb, n_t),
        in_specs=[pl.BlockSpec((bb, tt, D), lambda b, t: (b, t, 0)),
                  pl.BlockSpec((None, 1, D), lambda b, t: (l, 0, 0)),
                  pl.BlockSpec((None, D, SCAN_COLS), lambda b, t: (l, 0, 0), pipeline_mode=pl.Buffered(1)),
                  pl.BlockSpec((tt, d), lambda b, t: (t, 0)),
                  pl.BlockSpec((tt, d), lambda b, t: (t, 0))]
                 + [full(c) for c in consts] + state_specs + [layer(w) for w in weights]
                 + [pl.BlockSpec(memory_space=pl.ANY)] * len(alias_in),
        out_specs=[pl.BlockSpec((bb, tt, N_BRANCH * W), lambda b, t: (b, t, 0))] + state_specs,
        out_shape=[jax.ShapeDtypeStruct((B, T, N_BRANCH * W), F32)] + state_shapes,
        input_output_aliases={n_fixed + i: 1 + i for i in range(len(alias_in))},
        scratch_shapes=[pltpu.VMEM((bb * tt, W), F32)] * (SCAN_COLS // W) + [pltpu.VMEM((SUBLANES + ck, W), F32)],
        name="mixers",
        compiler_params=_cparams(("parallel", "arbitrary")),
    )(x, p['norm_mix'], p['w_in'], cos, sin, *consts, *states_in, *weights, *alias_in)
    return outs[0], tuple(outs[1:])


def _merge_kernel(x_ref, o_ref, nw_ref, wg0_ref, wg1_ref, wg2_ref, wbr_ref, wout_ref, y_ref):
    x = x_ref[...]
    W = MIX_WIDTH
    hn = _rms(x, nw_ref[...]).astype(BF16)
    merged = jnp.zeros(x.shape, F32)
    for n, wg_ref in enumerate((wg0_ref, wg1_ref, wg2_ref)):
        gz = jnp.dot(hn, wg_ref[...], preferred_element_type=F32)
        proj = jnp.dot(o_ref[:, n * W:(n + 1) * W].astype(BF16), wbr_ref[n], preferred_element_type=F32)
        merged = merged + _sigmoid(gz) * proj
    y_ref[...] = x + jnp.dot(merged.astype(BF16), wout_ref[...], preferred_element_type=F32)


def _merge(x, o, nw, w_in, wbr, wout, l):
    n, d = x.shape
    tm = _pick(n, (512, 256, 128))
    g0 = SCAN_COLS // d

    def gate_spec(k):
        return pl.BlockSpec((None, d, d), lambda i: (l, 0, g0 + k))

    return pl.pallas_call(
        _merge_kernel,
        grid=(n // tm,),
        in_specs=[pl.BlockSpec((tm, d), lambda i: (i, 0)),
                  pl.BlockSpec((tm, N_BRANCH * MIX_WIDTH), lambda i: (i, 0)),
                  pl.BlockSpec((None, 1, d), lambda i: (l, 0, 0)),
                  gate_spec(0), gate_spec(1), gate_spec(2),
                  pl.BlockSpec((None,) + wbr.shape[1:], lambda i: (l, 0, 0, 0)),
                  pl.BlockSpec((None,) + wout.shape[1:], lambda i: (l, 0, 0))],
        out_specs=pl.BlockSpec((tm, d), lambda i: (i, 0)),
        out_shape=jax.ShapeDtypeStruct((n, d), F32),
        name="merge",
        compiler_params=_cparams(("parallel",)),
    )(x, o, nw, w_in, w_in, w_in, wbr, wout)


def _router_kernel(x_ref, nw_ref, wr_ref, br_ref, hn_ref, mi_ref, mf_ref, cnt_ref, carry_ref):
    i = pl.program_id(0)
    tm = x_ref.shape[0]
    n_sub = x_ref.shape[1] // LANES

    @pl.when(i == 0)
    def _():
        carry_ref[...] = jnp.zeros(carry_ref.shape, F32)

    hn = _rms(x_ref[...], nw_ref[...])
    for s in range(n_sub):
        hn_ref[pl.ds(s, tm, stride=n_sub), :] = hn[:, s * LANES:(s + 1) * LANES]
    logits = _dot(hn, wr_ref[...]) + br_ref[...]
    lane = lax.broadcasted_iota(jnp.int32, logits.shape, 1)
    big = jnp.int32(1 << 20)

    def first_max(mask):
        vmax = jnp.max(jnp.where(mask, logits, NEG_INF), -1, keepdims=True)
        imax = jnp.min(jnp.where(mask, jnp.where(logits == vmax, lane, big), big), -1, keepdims=True)
        return vmax, imax

    in_groups = lane < N_GROUPS
    g_max, g_top = first_max(in_groups)
    g_w = 1.0 / jnp.sum(jnp.where(in_groups, jnp.exp(logits - g_max), 0.0), -1, keepdims=True)
    e_lo = N_GROUPS + EXPERTS_PER_GROUP * g_top
    in_e = jnp.logical_and(lane >= e_lo, lane < e_lo + EXPERTS_PER_GROUP)
    v1, i1 = first_max(in_e)
    v2, i2 = first_max(jnp.logical_and(in_e, lane != i1))
    e2 = jnp.exp(v2 - v1)
    w1 = g_w / (1.0 + e2)
    w2 = g_w * e2 / (1.0 + e2)

    hit1 = lane == i1
    hit2 = lane == i2
    onehot = jnp.where(hit1, 1.0, 0.0) + jnp.where(hit2, 1.0, 0.0)
    strict = (lax.broadcasted_iota(jnp.int32, (tm, tm), 0) > lax.broadcasted_iota(jnp.int32, (tm, tm), 1))
    before = _dot(strict.astype(F32), onehot) + carry_ref[...]
    r1 = jnp.sum(jnp.where(hit1, before, 0.0), -1, keepdims=True).astype(jnp.int32)
    r2 = jnp.sum(jnp.where(hit2, before, 0.0), -1, keepdims=True).astype(jnp.int32)
    carry_ref[...] = carry_ref[...] + jnp.sum(onehot, 0, keepdims=True)
    cnt_ref[...] = carry_ref[...]

    meta = jnp.where(lane == 0, i1 - N_GROUPS,
                     jnp.where(lane == 1, i2 - N_GROUPS,
                               jnp.where(lane == 2, r1, jnp.where(lane == 3, r2, 0))))
    mi_ref[...] = meta.T[:SUBLANES]
    mf_ref[...] = jnp.where(lane == 0, w1, jnp.where(lane == 1, w2, 0.0))


def _router(x, nw, wr, br, l):
    n, d = x.shape
    n_sub = d // LANES
    tm = _pick(n, (256, 128))
    return pl.pallas_call(
        _router_kernel,
        grid=(n // tm,),
        in_specs=[pl.BlockSpec((tm, d), lambda i: (i, 0)),
                  pl.BlockSpec((None, 1, d), lambda i: (l, 0, 0)),
                  pl.BlockSpec((None, d, LANES), lambda i: (l, 0, 0)),
                  pl.BlockSpec((None, 1, LANES), lambda i: (l, 0, 0))],
        out_specs=[pl.BlockSpec((tm * n_sub, LANES), lambda i: (i, 0)),
                   pl.BlockSpec((SUBLANES, tm), lambda i: (0, i)),
                   pl.BlockSpec((tm, LANES), lambda i: (i, 0)),
                   pl.BlockSpec((1, LANES), lambda i: (0, 0))],
        out_shape=[jax.ShapeDtypeStruct((n * n_sub, LANES), F32),
                   jax.ShapeDtypeStruct((SUBLANES, n), jnp.int32),
                   jax.ShapeDtypeStruct((n, LANES), F32),
                   jax.ShapeDtypeStruct((1, LANES), F32)],
        scratch_shapes=[pltpu.VMEM((1, LANES), F32)],
        name="router",
        compiler_params=_cparams(("arbitrary",)),
    )(x, nw, wr, br)


def _slots_kernel(d0_ref, d1_ref, pair_ref):
    def body(t, carry):
        pair_ref[d0_ref[t]] = TOP_K * t
        pair_ref[d1_ref[t]] = TOP_K * t + 1
        return carry

    lax.fori_loop(0, d0_ref.shape[0], body, 0, unroll=8)


def _slots(d0, d1, cap):
    return pl.pallas_call(
        _slots_kernel,
        grid_spec=pltpu.PrefetchScalarGridSpec(
            num_scalar_prefetch=2,
            grid=(1,),
            in_specs=[],
            out_specs=pl.BlockSpec(memory_space=pltpu.SMEM)),
        out_shape=jax.ShapeDtypeStruct((cap,), jnp.int32),
        name="slots",
        compiler_params=pltpu.CompilerParams(dimension_semantics=("arbitrary",)),
    )(d0, d1)


def _experts_kernel(be_ref, nv_ref, pair_ref, hn_ref, wg_ref, wu_ref, wd_ref, y2_ref,
                    xbuf, ybuf, gsem, ssem, wg_bf, wu_bf, wd_bf, *, n_sub):
    i = pl.program_id(0)
    n_i = pl.num_programs(0)
    slot = i % 2

    def row(r):
        return pl.ds(pl.multiple_of(r * n_sub, n_sub), n_sub)

    def gather(blk, sl, r):
        tok = pair_ref[blk * MOE_ROWS + r] // TOP_K
        return pltpu.make_async_copy(hn_ref.at[tok], xbuf.at[sl, row(r)], gsem.at[sl])

    def scatter(blk, sl, r):
        return pltpu.make_async_copy(ybuf.at[sl, row(r)], y2_ref.at[pair_ref[blk * MOE_ROWS + r]], ssem.at[sl])

    def each_valid(copy, blk, sl, start):
        def body(r, carry):
            cp = copy(blk, sl, r)
            if start:
                cp.start()
            else:
                cp.wait()
            return carry
        lax.fori_loop(0, nv_ref[blk], body, 0)

    @pl.when(i == 0)
    def _():
        xbuf[...] = jnp.zeros(xbuf.shape, F32)
        each_valid(gather, 0, 0, True)

    @pl.when(i + 1 < n_i)
    def _():
        each_valid(gather, i + 1, 1 - slot, True)

    each_valid(gather, i, slot, False)

    @pl.when(i >= 2)
    def _():
        each_valid(scatter, i - 2, slot, False)

    @pl.when(nv_ref[i] > 0)
    def _():
        changed = jnp.logical_or(i == 0, be_ref[i] != be_ref[jnp.maximum(i - 1, 0)])

        @pl.when(changed)
        def _():
            wg_bf[...] = wg_ref[...].astype(BF16)
            wu_bf[...] = wu_ref[...].astype(BF16)
            wd_bf[...] = wd_ref[...].astype(BF16)

        x = jnp.concatenate([xbuf[slot, pl.ds(s, MOE_ROWS, stride=n_sub), :] for s in range(n_sub)], axis=1)
        x = x.astype(BF16)
        g = jnp.dot(x, wg_bf[...], preferred_element_type=F32)
        u = jnp.dot(x, wu_bf[...], preferred_element_type=F32)
        y = jnp.dot((_silu(g) * u).astype(BF16), wd_bf[...], preferred_element_type=F32)
        for s in range(n_sub):
            ybuf[slot, pl.ds(s, MOE_ROWS, stride=n_sub), :] = y[:, s * LANES:(s + 1) * LANES]

    each_valid(scatter, i, slot, True)

    @pl.when(i == n_i - 1)
    def _():
        @pl.when(i >= 1)
        def _():
            each_valid(scatter, i - 1, 1 - slot, False)
        each_valid(scatter, i, slot, False)


def _experts(block_e, n_valid, pairs, hn, w_g, w_u, w_d, l):
    n, n_sub, _ = hn.shape
    _, _, d, f = w_g.shape
    n_blocks = block_e.shape[0]
    buf = pltpu.VMEM((2, MOE_ROWS * n_sub, LANES), F32)
    return pl.pallas_call(
        functools.partial(_experts_kernel, n_sub=n_sub),
        grid_spec=pltpu.PrefetchScalarGridSpec(
            num_scalar_prefetch=3,
            grid=(n_blocks,),
            in_specs=[pl.BlockSpec(memory_space=pl.ANY),
                      pl.BlockSpec((None, None, d, f), lambda i, be, nv, pr: (l, be[i], 0, 0)),
                      pl.BlockSpec((None, None, d, f), lambda i, be, nv, pr: (l, be[i], 0, 0)),
                      pl.BlockSpec((None, None, f, d), lambda i, be, nv, pr: (l, be[i], 0, 0))],
            out_specs=pl.BlockSpec(memory_space=pl.ANY),
            scratch_shapes=[buf, buf, pltpu.SemaphoreType.DMA((2,)), pltpu.SemaphoreType.DMA((2,)),
                            pltpu.VMEM((d, f), BF16), pltpu.VMEM((d, f), BF16), pltpu.VMEM((f, d), BF16)]),
        out_shape=jax.ShapeDtypeStruct((n * TOP_K, n_sub, LANES), F32),
        name="experts",
        compiler_params=pltpu.CompilerParams(dimension_semantics=("arbitrary",), vmem_limit_bytes=VMEM_LIMIT,
                                             has_side_effects=True),
    )(block_e, n_valid, pairs, hn, w_g, w_u, w_d)


def _combine_kernel(x_ref, mf_ref, nw_ref, y2_ref, y_ref, *, final_norm):
    tm = x_ref.shape[0]
    n_sub = x_ref.shape[1] // LANES
    w = mf_ref[...]
    stride = TOP_K * n_sub
    y0 = jnp.concatenate([y2_ref[pl.ds(s, tm, stride=stride), :] for s in range(n_sub)], axis=1)
    y1 = jnp.concatenate([y2_ref[pl.ds(n_sub + s, tm, stride=stride), :] for s in range(n_sub)], axis=1)
    out = x_ref[...] + (y0 * w[:, 0:1] + y1 * w[:, 1:2])
    if final_norm:
        out = _rms(out, nw_ref[...])
    y_ref[...] = out


def _combine(x, mf, nw, y2, final_norm):
    n, d = x.shape
    n_sub = d // LANES
    tm = _pick(n, (256, 128))
    return pl.pallas_call(
        functools.partial(_combine_kernel, final_norm=final_norm),
        grid=(n // tm,),
        in_specs=[pl.BlockSpec((tm, d), lambda i: (i, 0)),
                  pl.BlockSpec((tm, LANES), lambda i: (i, 0)),
                  pl.BlockSpec((1, d), lambda i: (0, 0)),
                  pl.BlockSpec((tm * TOP_K * n_sub, LANES), lambda i: (i, 0))],
        out_specs=pl.BlockSpec((tm, d), lambda i: (i, 0)),
        out_shape=jax.ShapeDtypeStruct((n, d), F32),
        name="combine",
        compiler_params=_cparams(("parallel",)),
    )(x, mf, nw, y2.reshape(n * TOP_K * n_sub, LANES))


def _moe(x, l, p, final_nw):
    n, d = x.shape
    n_sub = d // LANES
    hn, mi, mf, cnt = _router(x, p['norm_ffn'], p['w_r'], p['b_r'], l)
    counts = cnt[0, N_GROUPS:N_GROUPS + N_EXPERTS].astype(jnp.int32)
    padded = (counts + MOE_ROWS - 1) // MOE_ROWS * MOE_ROWS
    pad_end = jnp.cumsum(padded)
    pad_start = pad_end - padded
    experts = jnp.arange(N_EXPERTS, dtype=jnp.int32)

    def lookup(table, e):
        return jnp.sum(jnp.where(e[:, None] == experts[None, :], table[None, :], 0), axis=1)

    d0 = mi[2] + lookup(pad_start, mi[0])
    d1 = mi[3] + lookup(pad_start, mi[1])
    n_blocks = -(-(n * TOP_K) // MOE_ROWS) + N_EXPERTS
    block_row = jnp.arange(n_blocks, dtype=jnp.int32) * MOE_ROWS
    block_e = jnp.minimum(jnp.sum((pad_end[None, :] <= block_row[:, None]).astype(jnp.int32), axis=1), N_EXPERTS - 1)
    used_end = lookup(pad_start + counts, block_e)
    n_valid = jnp.clip(used_end - block_row, 0, MOE_ROWS).astype(jnp.int32)
    pairs = _slots(d0, d1, n_blocks * MOE_ROWS)
    y2 = _experts(block_e.astype(jnp.int32), n_valid, pairs, hn.reshape(n, n_sub, LANES),
                  p['w_eg'], p['w_eu'], p['w_ed'], l)
    nw = (final_nw if final_nw is not None else p['norm_ffn'][l, 0])[None]
    return _combine(x, mf, nw, y2, final_nw is not None)


def _trunk(x, pos, states, p):
    B, T, D = x.shape
    depth = p['w_in'].shape[0]
    half = HEAD_DIM // 2
    inv = ROPE_BASE ** (-jnp.arange(half, dtype=F32) / half)
    ang = pos.astype(F32)[:, None] * inv[None, :]
    cos = jnp.concatenate([jnp.cos(ang), jnp.cos(ang)], -1)
    sin = jnp.concatenate([-jnp.sin(ang), jnp.sin(ang)], -1)
    xf = x.reshape(B * T, D)
    new_states = None
    for l in range(depth):
        o, new_states = _mixers(xf.reshape(B, T, D), cos, sin, states, l, new_states, p)
        xf = _merge(xf, o.reshape(B * T, N_BRANCH * MIX_WIDTH), p['norm_mix'], p['w_in'], p['w_branch'], p['w_out'], l)
        xf = _moe(xf, l, p, p['norm_final'] if l == depth - 1 else None)
    sh, sr, c, n, m, cb = new_states
    return xf.reshape(B, T, D), (sh, sr, c, n, m.reshape(depth, B, N_HEADS), cb)


def kernel(x_prompt, x_sample, state_hgrn, state_ret, state_mlstm_C, state_mlstm_n, state_mlstm_m, state_mlstm_conv,
           norm_mix, norm_ffn, norm_final, w_in, hgrn_lb, hgrn_norm, ret_norm, mlstm_conv_w, mlstm_conv_b,
           mlstm_wq, mlstm_wk, mlstm_wv, mlstm_w_gates, mlstm_b_gates, mlstm_norm, mlstm_skip, w_branch, w_out,
           w_router_group, b_router_group, w_router_expert, b_router_expert, w_exp_gate, w_exp_up, w_exp_down):
    depth, D = norm_mix.shape
    H, d, W = N_HEADS, HEAD_DIM, MIX_WIDTH
    lb = jnp.cumsum(jax.nn.softmax(hgrn_lb.astype(F32), axis=0), axis=0)
    lb = lb - lb[0:1]
    prm = jnp.stack([jnp.log(lb), jnp.log1p(-lb), 1.0 - lb, hgrn_norm.astype(F32), ret_norm.astype(F32),
                     mlstm_norm.astype(F32), mlstm_skip.astype(F32), mlstm_conv_b.astype(F32)], axis=1)
    pad = LANES - N_GROUPS - N_EXPERTS
    w_r = jnp.concatenate([w_router_group, w_router_expert, jnp.zeros((depth, D, pad), F32)], -1)
    b_r = jnp.concatenate([b_router_group, b_router_expert, jnp.zeros((depth, pad), F32)], -1)[:, None, :]
    wg_bf = mlstm_w_gates.astype(BF16)
    p = {'norm_mix': norm_mix[:, None, :], 'norm_ffn': norm_ffn[:, None, :], 'norm_final': norm_final,
         'w_in': w_in.astype(BF16), 'prm': prm, 'conv_w': mlstm_conv_w.astype(F32),
         'mq': mlstm_wq.astype(BF16), 'mk': mlstm_wk.astype(BF16), 'mv': mlstm_wv.astype(BF16),
         'm_wg': wg_bf, 'm_wgt': jnp.swapaxes(wg_bf, 1, 2),
         'm_bg': mlstm_b_gates[:, None, :], 'm_bgt': mlstm_b_gates[:, :, None],
         'w_branch': w_branch.astype(BF16), 'w_out': w_out.astype(BF16),
         'w_r': w_r, 'b_r': b_r, 'w_eg': w_exp_gate, 'w_eu': w_exp_up, 'w_ed': w_exp_down}

    Bp, Tp = x_prompt.shape[0], x_prompt.shape[1]
    Bs = x_sample.shape[0]
    zero_states = (jnp.zeros((depth, Bp, H, d, d), F32), jnp.zeros((depth, Bp, H, d, d), F32),
                   jnp.zeros((depth, Bp, H, d, d), F32), jnp.zeros((depth, Bp, H, d), F32),
                   jnp.zeros((depth, Bp, 1, H), F32), jnp.zeros((depth, Bp, CONV_W - 1, W), F32))
    pos_prompt = jnp.arange(Tp, dtype=jnp.int32)
    pos_sample = PAST_LEN + jnp.arange(x_sample.shape[1], dtype=jnp.int32)
    y_prompt, ps = _trunk(x_prompt, pos_prompt, zero_states, p)
    sample_states = (state_hgrn, state_ret, state_mlstm_C, state_mlstm_n,
                     state_mlstm_m.reshape(depth, Bs, 1, H), state_mlstm_conv)
    y_sample, ss = _trunk(x_sample, pos_sample, sample_states, p)
    return (y_prompt, y_sample) + ps + ss
```

```python
import functools

import jax
import jax.numpy as jnp
import numpy as np
from jax import lax
from jax.experimental import pallas as pl
from jax.experimental.pallas import tpu as pltpu

F32 = jnp.float32
BF16 = jnp.bfloat16
HIGHEST = lax.Precision.HIGHEST

HEAD_DIM = 128
N_HEADS = 4
MIX_WIDTH = HEAD_DIM * N_HEADS
N_BRANCH = 3
CONV_W = 4
ROPE_BASE = 10000.0
N_GROUPS = 4
EXPERTS_PER_GROUP = 8
N_EXPERTS = N_GROUPS * EXPERTS_PER_GROUP
TOP_K = 2
PAST_LEN = 16384
NORM_EPS = 1e-6
HEAD_NORM_EPS = 1e-5
SCAN_COLS = 10 * MIX_WIDTH

LANES = 128
SUBLANES = 8
VMEM_LIMIT = 48 * 1024 * 1024
MOE_ROWS = 128
ISSUE_UNROLL = 8
NEG_INF = float("-inf")


def _cparams(sem):
    return pltpu.CompilerParams(dimension_semantics=sem, vmem_limit_bytes=VMEM_LIMIT)


def _pick(n, cands):
    for c in cands:
        if n % c == 0:
            return c
    return n


def _rms(x, w):
    return x * lax.rsqrt(jnp.mean(x * x, -1, keepdims=True) + NORM_EPS) * w


def _sigmoid(x):
    return 1.0 / (1.0 + jnp.exp(-x))


def _silu(x):
    return x * _sigmoid(x)


def _log_sigmoid(x):
    return jnp.minimum(x, 0.0) - jnp.log(1.0 + jnp.exp(-jnp.abs(x)))


def _dot(a, b):
    return jnp.dot(a.astype(BF16), b.astype(BF16), preferred_element_type=F32)


def _dot_nt(a, b):
    return lax.dot_general(a.astype(BF16), b.astype(BF16), (((1,), (1,)), ((), ())), preferred_element_type=F32)


def _dot_tn(a, b):
    return jnp.dot(a.T.astype(BF16), b.astype(BF16), preferred_element_type=F32)


def _split3(x):
    hi = x.astype(BF16)
    r = x - hi.astype(F32)
    mid = r.astype(BF16)
    lo = (r - mid.astype(F32)).astype(BF16)
    return hi, mid, lo


def _mask_dot(mask_bf, x):
    return sum(jnp.dot(mask_bf, part, preferred_element_type=F32) for part in _split3(x))


def _dot_mask(x, mask_bf):
    return sum(jnp.dot(part, mask_bf, preferred_element_type=F32) for part in _split3(x))


def _head_rmsnorm(o, w):
    return o * lax.rsqrt(jnp.mean(o * o, -1, keepdims=True) + HEAD_NORM_EPS) * w


def _head_layernorm(o, w):
    c = o - jnp.mean(o, -1, keepdims=True)
    return c * lax.rsqrt(jnp.mean(c * c, -1, keepdims=True) + HEAD_NORM_EPS) * w


def _level_masks(c):
    t = np.arange(c)[:, None]
    s = np.arange(c)[None, :]
    masks = [t == s]
    blk = 2
    while blk <= c:
        masks.append((t // blk == s // blk) & (t % blk >= blk // 2) & (s % blk < blk // 2))
        blk *= 2
    return np.stack(masks).astype(np.float32)


def _retention_tables(c):
    idx = np.arange(c, dtype=np.float64)
    lg = np.log(1.0 - 2.0 ** (-5.0 - np.arange(N_HEADS, dtype=np.float64)))[:, None, None]
    rel = idx[:, None] - idx[None, :]
    dm = np.where(rel >= 0, np.exp(np.maximum(rel, 0.0)[None] * lg), 0.0)
    q_in = np.broadcast_to(np.exp((idx + 1.0)[None, :, None] * lg), (N_HEADS, c, HEAD_DIM))
    k_out = np.broadcast_to(np.exp((c - 1.0 - idx)[None, :, None] * lg), (N_HEADS, c, HEAD_DIM))
    g_chunk = tuple(float(g) for g in np.exp(c * lg[:, 0, 0]))
    return dm.astype(np.float32), q_in.astype(np.float32), k_out.astype(np.float32), g_chunk


def _level_ref(G, blk, row):
    c, d = G.shape
    if blk == 2:
        return jnp.where((row & 1) == 1, pltpu.roll(G, 1, 0), G)
    if blk == 4:
        r = row & 3
        return jnp.where(r == 0, pltpu.roll(G, c - 1, 0),
                         jnp.where(r == 1, G, jnp.where(r == 2, pltpu.roll(G, 1, 0), pltpu.roll(G, 2, 0))))
    mid = blk // 2 - 1
    G3 = G.reshape(c // blk, blk, d)
    return jnp.broadcast_to(G3[:, mid:mid + 1, :], (c // blk, blk, d)).reshape(c, d)


def _mixer_kernel(*refs, tt, ck, bb, n_t, n_alias, g_chunk):
    (x_ref, nw_ref, w_ref, cos_ref, sin_ref, lvl_ref, dm_ref, qin_ref, kout_ref,
     sh0_ref, sr0_ref, c0_ref, n0_ref, m0_ref, cb0_ref,
     prm_ref, convw_ref, mq_ref, mk_ref, mv_ref, wg_ref, wgt_ref, bg_ref, bgt_ref) = refs[:24]
    o_ref, sh_ref, sr_ref, c_ref, n_ref, m_ref, cb_ref = refs[24 + n_alias:31 + n_alias]
    z_refs = refs[31 + n_alias:-1]
    ext_ref = refs[-1]
    ti = pl.program_id(1)
    d, W, H = HEAD_DIM, MIX_WIDTH, N_HEADS
    n_lvl = lvl_ref.shape[0]

    @pl.when(ti == 0)
    def _():
        def init(bi, carry):
            for h in range(H):
                sh_ref[bi, h] = sh0_ref[bi, h].T
            return carry
        lax.fori_loop(0, bb, init, 0)
        sr_ref[...] = sr0_ref[...]
        c_ref[...] = c0_ref[...]
        n_ref[...] = n0_ref[...]
        m_ref[...] = m0_ref[...]
        cb_ref[...] = cb0_ref[...]

    x = x_ref[...].reshape(bb * tt, x_ref.shape[-1])
    hn = _rms(x, nw_ref[...]).astype(BF16)
    for j, zj_ref in enumerate(z_refs):
        zj_ref[...] = jnp.dot(hn, w_ref[:, j * W:(j + 1) * W], preferred_element_type=F32)

    def prm(i, h):
        return prm_ref[i:i + 1, h * d:(h + 1) * d]

    t_i = lax.broadcasted_iota(jnp.int32, (ck, ck), 0)
    s_i = lax.broadcasted_iota(jnp.int32, (ck, ck), 1)
    causal = t_i >= s_i
    tri_l = causal.astype(BF16)
    tri_u = (t_i <= s_i).astype(BF16)
    row = lax.broadcasted_iota(jnp.int32, (ck, d), 0)

    def chunk(bi, c0):
        if isinstance(bi, int):
            rows = slice(bi * tt + c0, bi * tt + c0 + ck)
        else:
            rows = pl.ds(pl.multiple_of(bi * tt + c0, ck), ck)
        cos = cos_ref[c0:c0 + ck, :]
        sin = sin_ref[c0:c0 + ck, :]

        def rotary(a_):
            return a_ * cos + pltpu.roll(a_, d // 2, 1) * sin

        def zc(j, h):
            return z_refs[j][rows, h * d:(h + 1) * d]

        def put(j, h, val):
            o_ref[bi, c0:c0 + ck, j * W + h * d:j * W + (h + 1) * d] = val

        fpre = z_refs[1][rows, :]
        e = jnp.exp(-jnp.abs(fpre))
        a = prm_ref[0:1, :]
        b = prm_ref[1:2, :] + (jnp.minimum(fpre, 0.0) - jnp.log(1.0 + e))
        logf = jnp.maximum(a, b) + jnp.log(1.0 + jnp.exp(-jnp.abs(a - b)))
        k_all = prm_ref[2:3, :] * (jnp.where(fpre >= 0.0, e, 1.0) / (1.0 + e))
        G_all = _mask_dot(tri_l, logf)
        for h in range(H):
            hs = slice(h * d, (h + 1) * d)
            q, v, gate = zc(0, h), zc(2, h), zc(3, h)
            k, G = k_all[:, hs], G_all[:, hs]
            q_bf, k_bf = q.astype(BF16), k.astype(BF16)
            A = lvl_ref[0] * _dot_nt(q_bf, k_bf)
            for j in range(1, n_lvl):
                E = jnp.exp(-jnp.abs(G - _level_ref(G, 1 << j, row))).astype(BF16)
                A = A + lvl_ref[j] * _dot_nt(q_bf * E, k_bf * E)
            st = sh_ref[bi, h]
            o_h = _dot(A, v) + _dot_nt(q * jnp.exp(G), st)
            GL = G[ck - 1:ck]
            sh_ref[bi, h] = jnp.exp(GL) * st + _dot(v.T, k * jnp.exp(GL - G))
            put(0, h, _head_rmsnorm(o_h, prm(3, h)) * _silu(gate))

        for h in range(H):
            q = rotary(zc(4, h))
            k = rotary(zc(5, h)) * (d ** -0.5)
            v, gate = zc(6, h), zc(7, h)
            A = _dot_nt(q, k) * dm_ref[h]
            S = sr_ref[bi, h]
            o_h = _dot(A, v) + qin_ref[h] * _dot(q, S)
            sr_ref[bi, h] = g_chunk[h] * S + _dot_tn(k * kout_ref[h], v)
            put(1, h, _head_layernorm(o_h, prm(4, h)) * _silu(gate))

        u = z_refs[8][rows, :]
        ext_ref[SUBLANES - (CONV_W - 1):SUBLANES, :] = cb_ref[bi]
        ext_ref[SUBLANES:SUBLANES + ck, :] = u
        conv = jnp.zeros((ck, W), F32)
        for j in range(CONV_W):
            off = SUBLANES - (CONV_W - 1) + j
            conv = conv + convw_ref[j:j + 1, :] * ext_ref[off:off + ck, :]
        cb_ref[bi] = ext_ref[SUBLANES + ck - (CONV_W - 1):SUBLANES + ck, :]
        uc = _silu(conv + prm_ref[7:8, :])

        qs, ks, vs = [], [], []
        for h in range(H):
            uch = uc[:, h * d:(h + 1) * d].astype(BF16)
            qs.append(jnp.dot(uch, mq_ref[h], preferred_element_type=F32))
            ks.append(jnp.dot(uch, mk_ref[h], preferred_element_type=F32))
            vs.append(jnp.dot(u[:, h * d:(h + 1) * d].astype(BF16), mv_ref[h], preferred_element_type=F32))
        qkv = jnp.concatenate(qs + ks + vs, axis=1).astype(BF16)
        g_col = jnp.dot(qkv, wg_ref[...], preferred_element_type=F32) + bg_ref[...]
        g_row = lax.dot_general(wgt_ref[...], qkv, (((1,), (1,)), ((), ())),
                                preferred_element_type=F32) + bgt_ref[...]
        i_cols = g_col[:, :H]
        f_cols = _mask_dot(tri_l, _log_sigmoid(g_col[:, H:]))
        i_rows = g_row[:H]
        f_rows = _dot_mask(_log_sigmoid(g_row[H:]), tri_u)
        m_prev_all = m_ref[bi]
        for h in range(H):
            q, k, v = qs[h], ks[h] * (d ** -0.5), vs[h]
            i_col, F_col = i_cols[:, h:h + 1], f_cols[:, h:h + 1]
            a_row = i_rows[h:h + 1] - f_rows[h:h + 1]
            m_prev = m_prev_all[:, h:h + 1]
            cm = jnp.max(jnp.where(causal, a_row, NEG_INF), -1, keepdims=True)
            m_col = F_col + jnp.maximum(m_prev, cm)
            logd = (F_col - m_col) + a_row
            dmat = jnp.where(causal, jnp.exp(jnp.where(causal, logd, 0.0)), 0.0)
            Sc = _dot_nt(q, k) * dmat
            inter = jnp.exp(F_col + m_prev - m_col)
            Cst = c_ref[bi, h]
            n_row = n_ref[bi, h:h + 1, :]
            num = _dot(Sc, v) + inter * _dot(q, Cst)
            den = jnp.sum(Sc, -1, keepdims=True) + inter * jnp.sum(q * n_row, -1, keepdims=True)
            hout = num / jnp.maximum(jnp.abs(den), jnp.exp(-m_col))
            mL = m_col[ck - 1:ck]
            FL = F_col[ck - 1:ck]
            kw = k * jnp.exp(FL - F_col + i_col - mL)
            decay = jnp.exp(FL + m_prev - mL)
            c_ref[bi, h] = decay * Cst + _dot_tn(kw, v)
            n_ref[bi, h:h + 1, :] = decay * n_row + jnp.sum(kw, 0, keepdims=True)
            m_ref[bi, :, h:h + 1] = mL
            y = _head_layernorm(hout, prm(5, h)) + prm(6, h) * uc[:, h * d:(h + 1) * d]
            put(2, h, y * _silu(zc(9, h)))

    def sequence(bi):
        for c in range(tt // ck):
            chunk(bi, c * ck)

    if bb == 1:
        sequence(0)
    else:
        def per_batch(bi, carry):
            sequence(bi)
            return carry
        lax.fori_loop(0, bb, per_batch, 0, unroll=2)

    @pl.when(ti == n_t - 1)
    def _():
        def fin(bi, carry):
            for h in range(H):
                sh_ref[bi, h] = sh_ref[bi, h].T
            return carry
        lax.fori_loop(0, bb, fin, 0)


def _mixers(x, cos, sin, states_in, l, prev_out, p):
    B, T, D = x.shape
    depth = states_in[0].shape[0]
    tt = _pick(T, (256, 128, 64, 32, 16, 8))
    ck = min(tt, 128)
    bb = _pick(B, tuple(c for c in (8, 4, 2) if c * tt <= 128) + (1,))
    n_t = T // tt
    H, d, W = N_HEADS, HEAD_DIM, MIX_WIDTH
    dm, q_in, k_out, g_chunk = _retention_tables(ck)
    consts = [jnp.asarray(_level_masks(ck)), jnp.asarray(dm), jnp.asarray(q_in), jnp.asarray(k_out)]
    st_spec = pl.BlockSpec((None, bb, H, d, d), lambda b, t: (l, b, 0, 0, 0))
    n_spec = pl.BlockSpec((None, bb, H, d), lambda b, t: (l, b, 0, 0))
    m_spec = pl.BlockSpec((None, bb, 1, H), lambda b, t: (l, b, 0, 0))
    cb_spec = pl.BlockSpec((None, bb, CONV_W - 1, W), lambda b, t: (l, b, 0, 0))
    state_specs = [st_spec, st_spec, st_spec, n_spec, m_spec, cb_spec]

    def full(a):
        nd = a.ndim
        return pl.BlockSpec(a.shape, lambda b, t: (0,) * nd)

    def layer(a):
        nd = a.ndim - 1
        return pl.BlockSpec((None,) + a.shape[1:], lambda b, t: (l,) + (0,) * nd)

    weights = [p['prm'], p['conv_w'], p['mq'], p['mk'], p['mv'], p['m_wg'], p['m_wgt'], p['m_bg'], p['m_bgt']]
    alias_in = list(prev_out) if prev_out is not None else []
    n_fixed = 5 + len(consts) + 6 + len(weights)
    state_shapes = [jax.ShapeDtypeStruct((depth, B, H, d, d), F32)] * 3 + [
        jax.ShapeDtypeStruct((depth, B, H, d), F32),
        jax.ShapeDtypeStruct((depth, B, 1, H), F32),
        jax.ShapeDtypeStruct((depth, B, CONV_W - 1, W), F32)]
    outs = pl.pallas_call(
        functools.partial(_mixer_kernel, tt=tt, ck=ck, bb=bb, n_t=n_t, n_alias=len(alias_in), g_chunk=g_chunk),
        grid=(B // bb, n_t),
        in_specs=[pl.BlockSpec((bb, tt, D), lambda b, t: (b, t, 0)),
                  pl.BlockSpec((None, 1, D), lambda b, t: (l, 0, 0)),
                  pl.BlockSpec((None, D, SCAN_COLS), lambda b, t: (l, 0, 0), pipeline_mode=pl.Buffered(1)),
                  pl.BlockSpec((tt, d), lambda b, t: (t, 0)),
                  pl.BlockSpec((tt, d), lambda b, t: (t, 0))]
                 + [full(c) for c in consts] + state_specs + [layer(w) for w in weights]
                 + [pl.BlockSpec(memory_space=pl.ANY)] * len(alias_in),
        out_specs=[pl.BlockSpec((bb, tt, N_BRANCH * W), lambda b, t: (b, t, 0))] + state_specs,
        out_shape=[jax.ShapeDtypeStruct((B, T, N_BRANCH * W), F32)] + state_shapes,
        input_output_aliases={n_fixed + i: 1 + i for i in range(len(alias_in))},
        scratch_shapes=[pltpu.VMEM((bb * tt, W), F32)] * (SCAN_COLS // W) + [pltpu.VMEM((SUBLANES + ck, W), F32)],
        name="mixers",
        compiler_params=_cparams(("parallel", "arbitrary")),
    )(x, p['norm_mix'], p['w_in'], cos, sin, *consts, *states_in, *weights, *alias_in)
    return outs[0], tuple(outs[1:])


def _merge_kernel(x_ref, o_ref, nw_ref, wg0_ref, wg1_ref, wg2_ref, wbr_ref, wout_ref, y_ref):
    x = x_ref[...]
    W = MIX_WIDTH
    hn = _rms(x, nw_ref[...]).astype(BF16)
    merged = jnp.zeros(x.shape, F32)
    for n, wg_ref in enumerate((wg0_ref, wg1_ref, wg2_ref)):
        gz = jnp.dot(hn, wg_ref[...], preferred_element_type=F32)
        proj = jnp.dot(o_ref[:, n * W:(n + 1) * W].astype(BF16), wbr_ref[n], preferred_element_type=F32)
        merged = merged + _sigmoid(gz) * proj
    y_ref[...] = x + jnp.dot(merged.astype(BF16), wout_ref[...], preferred_element_type=F32)


def _merge(x, o, nw, w_in, wbr, wout, l):
    n, d = x.shape
    tm = _pick(n, (512, 256, 128))
    g0 = SCAN_COLS // d

    def gate_spec(k):
        return pl.BlockSpec((None, d, d), lambda i: (l, 0, g0 + k))

    return pl.pallas_call(
        _merge_kernel,
        grid=(n // tm,),
        in_specs=[pl.BlockSpec((tm, d), lambda i: (i, 0)),
                  pl.BlockSpec((tm, N_BRANCH * MIX_WIDTH), lambda i: (i, 0)),
                  pl.BlockSpec((None, 1, d), lambda i: (l, 0, 0)),
                  gate_spec(0), gate_spec(1), gate_spec(2),
                  pl.BlockSpec((None,) + wbr.shape[1:], lambda i: (l, 0, 0, 0)),
                  pl.BlockSpec((None,) + wout.shape[1:], lambda i: (l, 0, 0))],
        out_specs=pl.BlockSpec((tm, d), lambda i: (i, 0)),
        out_shape=jax.ShapeDtypeStruct((n, d), F32),
        name="merge",
        compiler_params=_cparams(("parallel",)),
    )(x, o, nw, w_in, w_in, w_in, wbr, wout)


def _router_kernel(x_ref, nw_ref, wr_ref, br_ref, hn_ref, mi_ref, mf_ref, cnt_ref, carry_ref):
    i = pl.program_id(0)
    tm = x_ref.shape[0]
    n_sub = x_ref.shape[1] // LANES

    @pl.when(i == 0)
    def _():
        carry_ref[...] = jnp.zeros(carry_ref.shape, F32)

    hn = _rms(x_ref[...], nw_ref[...])
    for s in range(n_sub):
        hn_ref[pl.ds(s, tm, stride=n_sub), :] = hn[:, s * LANES:(s + 1) * LANES]
    logits = _dot(hn, wr_ref[...]) + br_ref[...]
    lane = lax.broadcasted_iota(jnp.int32, logits.shape, 1)
    big = jnp.int32(1 << 20)

    def first_max(mask):
        vmax = jnp.max(jnp.where(mask, logits, NEG_INF), -1, keepdims=True)
        imax = jnp.min(jnp.where(mask, jnp.where(logits == vmax, lane, big), big), -1, keepdims=True)
        return vmax, imax

    in_groups = lane < N_GROUPS
    g_max, g_top = first_max(in_groups)
    g_w = 1.0 / jnp.sum(jnp.where(in_groups, jnp.exp(logits - g_max), 0.0), -1, keepdims=True)
    e_lo = N_GROUPS + EXPERTS_PER_GROUP * g_top
    in_e = jnp.logical_and(lane >= e_lo, lane < e_lo + EXPERTS_PER_GROUP)
    v1, i1 = first_max(in_e)
    v2, i2 = first_max(jnp.logical_and(in_e, lane != i1))
    e2 = jnp.exp(v2 - v1)
    w1 = g_w / (1.0 + e2)
    w2 = g_w * e2 / (1.0 + e2)

    hit1 = lane == i1
    hit2 = lane == i2
    onehot = jnp.where(hit1, 1.0, 0.0) + jnp.where(hit2, 1.0, 0.0)
    strict = (lax.broadcasted_iota(jnp.int32, (tm, tm), 0) > lax.broadcasted_iota(jnp.int32, (tm, tm), 1))
    before = _dot(strict.astype(F32), onehot) + carry_ref[...]
    r1 = jnp.sum(jnp.where(hit1, before, 0.0), -1, keepdims=True).astype(jnp.int32)
    r2 = jnp.sum(jnp.where(hit2, before, 0.0), -1, keepdims=True).astype(jnp.int32)
    carry_ref[...] = carry_ref[...] + jnp.sum(onehot, 0, keepdims=True)
    cnt_ref[...] = carry_ref[...]

    meta = jnp.where(lane == 0, i1 - N_GROUPS,
                     jnp.where(lane == 1, i2 - N_GROUPS,
                               jnp.where(lane == 2, r1, jnp.where(lane == 3, r2, 0))))
    mi_ref[...] = meta.T[:SUBLANES]
    mf_ref[...] = jnp.where(lane == 0, w1, jnp.where(lane == 1, w2, 0.0))


def _router(x, nw, wr, br, l):
    n, d = x.shape
    n_sub = d // LANES
    tm = _pick(n, (256, 128))
    return pl.pallas_call(
        _router_kernel,
        grid=(n // tm,),
        in_specs=[pl.BlockSpec((tm, d), lambda i: (i, 0)),
                  pl.BlockSpec((None, 1, d), lambda i: (l, 0, 0)),
                  pl.BlockSpec((None, d, LANES), lambda i: (l, 0, 0)),
                  pl.BlockSpec((None, 1, LANES), lambda i: (l, 0, 0))],
        out_specs=[pl.BlockSpec((tm * n_sub, LANES), lambda i: (i, 0)),
                   pl.BlockSpec((SUBLANES, tm), lambda i: (0, i)),
                   pl.BlockSpec((tm, LANES), lambda i: (i, 0)),
                   pl.BlockSpec((1, LANES), lambda i: (0, 0))],
        out_shape=[jax.ShapeDtypeStruct((n * n_sub, LANES), F32),
                   jax.ShapeDtypeStruct((SUBLANES, n), jnp.int32),
                   jax.ShapeDtypeStruct((n, LANES), F32),
                   jax.ShapeDtypeStruct((1, LANES), F32)],
        scratch_shapes=[pltpu.VMEM((1, LANES), F32)],
        name="router",
        compiler_params=_cparams(("arbitrary",)),
    )(x, nw, wr, br)


def _slots_kernel(d0_ref, d1_ref, pair_ref):
    def body(t, carry):
        pair_ref[d0_ref[t]] = TOP_K * t
        pair_ref[d1_ref[t]] = TOP_K * t + 1
        return carry

    lax.fori_loop(0, d0_ref.shape[0], body, 0, unroll=8)


def _slots(d0, d1, cap):
    return pl.pallas_call(
        _slots_kernel,
        grid_spec=pltpu.PrefetchScalarGridSpec(
            num_scalar_prefetch=2,
            grid=(1,),
            in_specs=[],
            out_specs=pl.BlockSpec(memory_space=pltpu.SMEM)),
        out_shape=jax.ShapeDtypeStruct((cap,), jnp.int32),
        name="slots",
        compiler_params=pltpu.CompilerParams(dimension_semantics=("arbitrary",)),
    )(d0, d1)


def _experts_kernel(be_ref, nv_ref, pair_ref, hn_ref, wg_ref, wu_ref, wd_ref, y2_ref,
                    xbuf, ybuf, gsem, ssem, wg_bf, wu_bf, wd_bf, *, n_sub):
    i = pl.program_id(0)
    n_i = pl.num_programs(0)
    slot = i % 2

    def row(r):
        return pl.ds(pl.multiple_of(r * n_sub, n_sub), n_sub)

    def gather(blk, sl, r):
        tok = jnp.right_shift(pair_ref[blk * MOE_ROWS + r], TOP_K.bit_length() - 1)
        return pltpu.make_async_copy(hn_ref.at[row(tok)], xbuf.at[sl, row(r)], gsem.at[sl])

    def scatter(blk, sl, r):
        return pltpu.make_async_copy(ybuf.at[sl, row(r)], y2_ref.at[row(pair_ref[blk * MOE_ROWS + r])], ssem.at[sl])

    def start_valid(copy, blk, sl):
        n_valid = nv_ref[blk]
        n_full = n_valid // ISSUE_UNROLL

        def body(j, carry):
            for k in range(ISSUE_UNROLL):
                copy(blk, sl, j * ISSUE_UNROLL + k).start()
            return carry

        def tail(r, carry):
            copy(blk, sl, r).start()
            return carry

        lax.fori_loop(0, n_full, body, 0)
        lax.fori_loop(n_full * ISSUE_UNROLL, n_valid, tail, 0)

    def wait_valid(buf, blk, sl):
        n_valid = nv_ref[blk]

        @pl.when(n_valid > 0)
        def _():
            rows = pl.ds(0, n_valid * n_sub)
            if buf is xbuf:
                pltpu.make_async_copy(hn_ref.at[rows], xbuf.at[sl, rows], gsem.at[sl]).wait()
            else:
                pltpu.make_async_copy(ybuf.at[sl, rows], y2_ref.at[rows], ssem.at[sl]).wait()

    @pl.when(i == 0)
    def _():
        xbuf[...] = jnp.zeros(xbuf.shape, F32)
        start_valid(gather, 0, 0)

    @pl.when(i + 1 < n_i)
    def _():
        start_valid(gather, i + 1, 1 - slot)

    wait_valid(xbuf, i, slot)

    @pl.when(i >= 2)
    def _():
        wait_valid(ybuf, i - 2, slot)

    @pl.when(nv_ref[i] > 0)
    def _():
        changed = jnp.logical_or(i == 0, be_ref[i] != be_ref[jnp.maximum(i - 1, 0)])

        @pl.when(changed)
        def _():
            wg_bf[...] = wg_ref[...].astype(BF16)
            wu_bf[...] = wu_ref[...].astype(BF16)
            wd_bf[...] = wd_ref[...].astype(BF16)

        x = jnp.concatenate([xbuf[slot, pl.ds(s, MOE_ROWS, stride=n_sub), :] for s in range(n_sub)], axis=1)
        x = x.astype(BF16)
        g = jnp.dot(x, wg_bf[...], preferred_element_type=F32)
        u = jnp.dot(x, wu_bf[...], preferred_element_type=F32)
        y = jnp.dot((_silu(g) * u).astype(BF16), wd_bf[...], preferred_element_type=F32)
        for s in range(n_sub):
            ybuf[slot, pl.ds(s, MOE_ROWS, stride=n_sub), :] = y[:, s * LANES:(s + 1) * LANES]

    start_valid(scatter, i, slot)

    @pl.when(i == n_i - 1)
    def _():
        @pl.when(i >= 1)
        def _():
            wait_valid(ybuf, i - 1, 1 - slot)
        wait_valid(ybuf, i, slot)


def _experts(block_e, n_valid, pairs, hn, w_g, w_u, w_d, l):
    _, _, d, f = w_g.shape
    n_sub = d // LANES
    n = hn.shape[0] // n_sub
    n_blocks = block_e.shape[0]
    buf = pltpu.VMEM((2, MOE_ROWS * n_sub, LANES), F32)
    return pl.pallas_call(
        functools.partial(_experts_kernel, n_sub=n_sub),
        grid_spec=pltpu.PrefetchScalarGridSpec(
            num_scalar_prefetch=3,
            grid=(n_blocks,),
            in_specs=[pl.BlockSpec(memory_space=pl.ANY),
                      pl.BlockSpec((None, None, d, f), lambda i, be, nv, pr: (l, be[i], 0, 0)),
                      pl.BlockSpec((None, None, d, f), lambda i, be, nv, pr: (l, be[i], 0, 0)),
                      pl.BlockSpec((None, None, f, d), lambda i, be, nv, pr: (l, be[i], 0, 0))],
            out_specs=pl.BlockSpec(memory_space=pl.ANY),
            scratch_shapes=[buf, buf, pltpu.SemaphoreType.DMA((2,)), pltpu.SemaphoreType.DMA((2,)),
                            pltpu.VMEM((d, f), BF16), pltpu.VMEM((d, f), BF16), pltpu.VMEM((f, d), BF16)]),
        out_shape=jax.ShapeDtypeStruct((n * TOP_K * n_sub, LANES), F32),
        name="experts",
        compiler_params=pltpu.CompilerParams(dimension_semantics=("arbitrary",), vmem_limit_bytes=VMEM_LIMIT,
                                             has_side_effects=True),
    )(block_e, n_valid, pairs, hn, w_g, w_u, w_d)


def _combine_kernel(x_ref, mf_ref, nw_ref, y2_ref, y_ref, *, final_norm):
    tm = x_ref.shape[0]
    n_sub = x_ref.shape[1] // LANES
    w = mf_ref[...]
    stride = TOP_K * n_sub
    y0 = jnp.concatenate([y2_ref[pl.ds(s, tm, stride=stride), :] for s in range(n_sub)], axis=1)
    y1 = jnp.concatenate([y2_ref[pl.ds(n_sub + s, tm, stride=stride), :] for s in range(n_sub)], axis=1)
    out = x_ref[...] + (y0 * w[:, 0:1] + y1 * w[:, 1:2])
    if final_norm:
        out = _rms(out, nw_ref[...])
    y_ref[...] = out


def _combine(x, mf, nw, y2, final_norm):
    n, d = x.shape
    n_sub = d // LANES
    tm = _pick(n, (256, 128))
    return pl.pallas_call(
        functools.partial(_combine_kernel, final_norm=final_norm),
        grid=(n // tm,),
        in_specs=[pl.BlockSpec((tm, d), lambda i: (i, 0)),
                  pl.BlockSpec((tm, LANES), lambda i: (i, 0)),
                  pl.BlockSpec((1, d), lambda i: (0, 0)),
                  pl.BlockSpec((tm * TOP_K * n_sub, LANES), lambda i: (i, 0))],
        out_specs=pl.BlockSpec((tm, d), lambda i: (i, 0)),
        out_shape=jax.ShapeDtypeStruct((n, d), F32),
        name="combine",
        compiler_params=_cparams(("parallel",)),
    )(x, mf, nw, y2)


def _moe(x, l, p, final_nw):
    n, d = x.shape
    n_sub = d // LANES
    hn, mi, mf, cnt = _router(x, p['norm_ffn'], p['w_r'], p['b_r'], l)
    counts = cnt[0, N_GROUPS:N_GROUPS + N_EXPERTS].astype(jnp.int32)
    padded = (counts + MOE_ROWS - 1) // MOE_ROWS * MOE_ROWS
    pad_end = jnp.cumsum(padded)
    pad_start = pad_end - padded
    experts = jnp.arange(N_EXPERTS, dtype=jnp.int32)

    def lookup(table, e):
        return jnp.sum(jnp.where(e[:, None] == experts[None, :], table[None, :], 0), axis=1)

    d0 = mi[2] + lookup(pad_start, mi[0])
    d1 = mi[3] + lookup(pad_start, mi[1])
    n_blocks = -(-(n * TOP_K) // MOE_ROWS) + N_EXPERTS
    block_row = jnp.arange(n_blocks, dtype=jnp.int32) * MOE_ROWS
    block_e = jnp.minimum(jnp.sum((pad_end[None, :] <= block_row[:, None]).astype(jnp.int32), axis=1), N_EXPERTS - 1)
    used_end = lookup(pad_start + counts, block_e)
    n_valid = jnp.clip(used_end - block_row, 0, MOE_ROWS).astype(jnp.int32)
    pairs = _slots(d0, d1, n_blocks * MOE_ROWS)
    y2 = _experts(block_e.astype(jnp.int32), n_valid, pairs, hn, p['w_eg'], p['w_eu'], p['w_ed'], l)
    nw = (final_nw if final_nw is not None else p['norm_ffn'][l, 0])[None]
    return _combine(x, mf, nw, y2, final_nw is not None)


def _trunk(x, pos, states, p):
    B, T, D = x.shape
    depth = p['w_in'].shape[0]
    half = HEAD_DIM // 2
    inv = ROPE_BASE ** (-jnp.arange(half, dtype=F32) / half)
    ang = pos.astype(F32)[:, None] * inv[None, :]
    cos = jnp.concatenate([jnp.cos(ang), jnp.cos(ang)], -1)
    sin = jnp.concatenate([-jnp.sin(ang), jnp.sin(ang)], -1)
    xf = x.reshape(B * T, D)
    new_states = None
    for l in range(depth):
        o, new_states = _mixers(xf.reshape(B, T, D), cos, sin, states, l, new_states, p)
        xf = _merge(xf, o.reshape(B * T, N_BRANCH * MIX_WIDTH), p['norm_mix'], p['w_in'], p['w_branch'], p['w_out'], l)
        xf = _moe(xf, l, p, p['norm_final'] if l == depth - 1 else None)
    sh, sr, c, n, m, cb = new_states
    return xf.reshape(B, T, D), (sh, sr, c, n, m.reshape(depth, B, N_HEADS), cb)


def kernel(x_prompt, x_sample, state_hgrn, state_ret, state_mlstm_C, state_mlstm_n, state_mlstm_m, state_mlstm_conv,
           norm_mix, norm_ffn, norm_final, w_in, hgrn_lb, hgrn_norm, ret_norm, mlstm_conv_w, mlstm_conv_b,
           mlstm_wq, mlstm_wk, mlstm_wv, mlstm_w_gates, mlstm_b_gates, mlstm_norm, mlstm_skip, w_branch, w_out,
           w_router_group, b_router_group, w_router_expert, b_router_expert, w_exp_gate, w_exp_up, w_exp_down):
    depth, D = norm_mix.shape
    H, d, W = N_HEADS, HEAD_DIM, MIX_WIDTH
    lb = jnp.cumsum(jax.nn.softmax(hgrn_lb.astype(F32), axis=0), axis=0)
    lb = lb - lb[0:1]
    prm = jnp.stack([jnp.log(lb), jnp.log1p(-lb), 1.0 - lb, hgrn_norm.astype(F32), ret_norm.astype(F32),
                     mlstm_norm.astype(F32), mlstm_skip.astype(F32), mlstm_conv_b.astype(F32)], axis=1)
    pad = LANES - N_GROUPS - N_EXPERTS
    w_r = jnp.concatenate([w_router_group, w_router_expert, jnp.zeros((depth, D, pad), F32)], -1)
    b_r = jnp.concatenate([b_router_group, b_router_expert, jnp.zeros((depth, pad), F32)], -1)[:, None, :]
    wg_bf = mlstm_w_gates.astype(BF16)
    p = {'norm_mix': norm_mix[:, None, :], 'norm_ffn': norm_ffn[:, None, :], 'norm_final': norm_final,
         'w_in': w_in.astype(BF16), 'prm': prm, 'conv_w': mlstm_conv_w.astype(F32),
         'mq': mlstm_wq.astype(BF16), 'mk': mlstm_wk.astype(BF16), 'mv': mlstm_wv.astype(BF16),
         'm_wg': wg_bf, 'm_wgt': jnp.swapaxes(wg_bf, 1, 2),
         'm_bg': mlstm_b_gates[:, None, :], 'm_bgt': mlstm_b_gates[:, :, None],
         'w_branch': w_branch.astype(BF16), 'w_out': w_out.astype(BF16),
         'w_r': w_r, 'b_r': b_r, 'w_eg': w_exp_gate, 'w_eu': w_exp_up, 'w_ed': w_exp_down}

    Bp, Tp = x_prompt.shape[0], x_prompt.shape[1]
    Bs = x_sample.shape[0]
    zero_states = (jnp.zeros((depth, Bp, H, d, d), F32), jnp.zeros((depth, Bp, H, d, d), F32),
                   jnp.zeros((depth, Bp, H, d, d), F32), jnp.zeros((depth, Bp, H, d), F32),
                   jnp.zeros((depth, Bp, 1, H), F32), jnp.zeros((depth, Bp, CONV_W - 1, W), F32))
    pos_prompt = jnp.arange(Tp, dtype=jnp.int32)
    pos_sample = PAST_LEN + jnp.arange(x_sample.shape[1], dtype=jnp.int32)
    y_prompt, ps = _trunk(x_prompt, pos_prompt, zero_states, p)
    sample_states = (state_hgrn, state_ret, state_mlstm_C, state_mlstm_n,
                     state_mlstm_m.reshape(depth, Bs, 1, H), state_mlstm_conv)
    y_sample, ss = _trunk(x_sample, pos_sample, sample_states, p)
    return (y_prompt, y_sample) + ps + ss
```

```python
import functools

import jax
import jax.numpy as jnp
import numpy as np
from jax import lax
from jax.experimental import pallas as pl
from jax.experimental.pallas import tpu as pltpu

F32 = jnp.float32
BF16 = jnp.bfloat16
HIGHEST = lax.Precision.HIGHEST

HEAD_DIM = 128
N_HEADS = 4
MIX_WIDTH = HEAD_DIM * N_HEADS
N_BRANCH = 3
CONV_W = 4
ROPE_BASE = 10000.0
N_GROUPS = 4
EXPERTS_PER_GROUP = 8
N_EXPERTS = N_GROUPS * EXPERTS_PER_GROUP
TOP_K = 2
PAST_LEN = 16384
NORM_EPS = 1e-6
HEAD_NORM_EPS = 1e-5
SCAN_COLS = 10 * MIX_WIDTH

LANES = 128
SUBLANES = 8
VMEM_LIMIT = 48 * 1024 * 1024
MOE_ROWS = 256
ISSUE_UNROLL = 8
NEG_INF = float("-inf")


def _cparams(sem):
    return pltpu.CompilerParams(dimension_semantics=sem, vmem_limit_bytes=VMEM_LIMIT)


def _pick(n, cands):
    for c in cands:
        if n % c == 0:
            return c
    return n


def _rms(x, w):
    return x * lax.rsqrt(jnp.mean(x * x, -1, keepdims=True) + NORM_EPS) * w


def _sigmoid(x):
    return 1.0 / (1.0 + jnp.exp(-x))


def _silu(x):
    return x * _sigmoid(x)


def _log_sigmoid(x):
    return jnp.minimum(x, 0.0) - jnp.log(1.0 + jnp.exp(-jnp.abs(x)))


def _dot(a, b):
    return jnp.dot(a.astype(BF16), b.astype(BF16), preferred_element_type=F32)


def _dot_nt(a, b):
    return lax.dot_general(a.astype(BF16), b.astype(BF16), (((1,), (1,)), ((), ())), preferred_element_type=F32)


def _dot_tn(a, b):
    return jnp.dot(a.T.astype(BF16), b.astype(BF16), preferred_element_type=F32)


def _split3(x):
    hi = x.astype(BF16)
    r = x - hi.astype(F32)
    mid = r.astype(BF16)
    lo = (r - mid.astype(F32)).astype(BF16)
    return hi, mid, lo


def _mask_dot(mask_bf, x):
    return sum(jnp.dot(mask_bf, part, preferred_element_type=F32) for part in _split3(x))


def _dot_mask(x, mask_bf):
    return sum(jnp.dot(part, mask_bf, preferred_element_type=F32) for part in _split3(x))


def _head_rmsnorm(o, w):
    return o * lax.rsqrt(jnp.mean(o * o, -1, keepdims=True) + HEAD_NORM_EPS) * w


def _head_layernorm(o, w):
    c = o - jnp.mean(o, -1, keepdims=True)
    return c * lax.rsqrt(jnp.mean(c * c, -1, keepdims=True) + HEAD_NORM_EPS) * w


def _level_masks(c):
    t = np.arange(c)[:, None]
    s = np.arange(c)[None, :]
    masks = [t == s]
    blk = 2
    while blk <= c:
        masks.append((t // blk == s // blk) & (t % blk >= blk // 2) & (s % blk < blk // 2))
        blk *= 2
    return np.stack(masks).astype(np.float32)


def _retention_tables(c):
    idx = np.arange(c, dtype=np.float64)
    lg = np.log(1.0 - 2.0 ** (-5.0 - np.arange(N_HEADS, dtype=np.float64)))[:, None, None]
    rel = idx[:, None] - idx[None, :]
    dm = np.where(rel >= 0, np.exp(np.maximum(rel, 0.0)[None] * lg), 0.0)
    q_in = np.broadcast_to(np.exp((idx + 1.0)[None, :, None] * lg), (N_HEADS, c, HEAD_DIM))
    k_out = np.broadcast_to(np.exp((c - 1.0 - idx)[None, :, None] * lg), (N_HEADS, c, HEAD_DIM))
    g_chunk = tuple(float(g) for g in np.exp(c * lg[:, 0, 0]))
    return dm.astype(np.float32), q_in.astype(np.float32), k_out.astype(np.float32), g_chunk


def _level_ref(G, blk, row):
    c, d = G.shape
    if blk == 2:
        return jnp.where((row & 1) == 1, pltpu.roll(G, 1, 0), G)
    if blk == 4:
        r = row & 3
        return jnp.where(r == 0, pltpu.roll(G, c - 1, 0),
                         jnp.where(r == 1, G, jnp.where(r == 2, pltpu.roll(G, 1, 0), pltpu.roll(G, 2, 0))))
    mid = blk // 2 - 1
    G3 = G.reshape(c // blk, blk, d)
    return jnp.broadcast_to(G3[:, mid:mid + 1, :], (c // blk, blk, d)).reshape(c, d)


def _seq_tables(seq, n_seq):
    rows = seq * n_seq
    t = np.arange(rows)[:, None]
    s = np.arange(rows)[None, :]
    same = (t // seq) == (s // seq)
    tri = np.stack([same & (t >= s), same & (t <= s)]).astype(np.float32)
    levels = _level_masks(seq)
    lvl = np.zeros((levels.shape[0], rows, rows), np.float32)
    dm1, q_in1, k_out1, g_chunk = _retention_tables(seq)
    dm = np.zeros((N_HEADS, rows, rows), np.float32)
    for b in range(n_seq):
        sl = slice(b * seq, (b + 1) * seq)
        lvl[:, sl, sl] = levels
        dm[:, sl, sl] = dm1
    q_in = np.tile(q_in1, (1, n_seq, 1))
    k_out = np.tile(k_out1, (1, n_seq, 1))
    return [lvl, tri, dm, q_in, k_out], g_chunk


def _mixer_kernel(*refs, tt, seq, n_seq, n_alias, g_chunk):
    (x_ref, nw_ref, w_ref, cos_ref, sin_ref, lvl_ref, tri_ref, dm_ref, qin_ref, kout_ref,
     sh0_ref, sr0_ref, c0_ref, n0_ref, m0_ref, cb0_ref,
     prm_ref, convw_ref, mq_ref, mk_ref, mv_ref, wg_ref, wgt_ref, bg_ref, bgt_ref) = refs[:25]
    o_ref, sh_ref, sr_ref, c_ref, n_ref, m_ref, cb_ref = refs[25 + n_alias:32 + n_alias]
    z_refs = refs[32 + n_alias:-1]
    ext_ref = refs[-1]
    ti = pl.program_id(1)
    d, W, H = HEAD_DIM, MIX_WIDTH, N_HEADS
    bb = x_ref.shape[0]
    n_lvl = lvl_ref.shape[0]
    R = seq * n_seq

    @pl.when(ti == 0)
    def _():
        sh_ref[...] = sh0_ref[...]
        sr_ref[...] = sr0_ref[...]
        c_ref[...] = c0_ref[...]
        n_ref[...] = n0_ref[...]
        m_ref[...] = m0_ref[...]
        cb_ref[...] = cb0_ref[...]

    x = x_ref[...].reshape(bb * tt, x_ref.shape[-1])
    hn = _rms(x, nw_ref[...]).astype(BF16)
    for j, zj_ref in enumerate(z_refs):
        zj_ref[...] = jnp.dot(hn, w_ref[:, j * W:(j + 1) * W], preferred_element_type=F32)

    def prm(i, h):
        return prm_ref[i:i + 1, h * d:(h + 1) * d]

    tri_l = tri_ref[0].astype(BF16)
    tri_u = tri_ref[1].astype(BF16)
    causal = tri_ref[0] > 0.0
    row = lax.broadcasted_iota(jnp.int32, (R, d), 0)

    def sl(b):
        return slice(b * seq, (b + 1) * seq)

    def per_seq(fn):
        parts = [fn(b) for b in range(n_seq)]
        return parts[0] if n_seq == 1 else jnp.concatenate(parts, axis=0)

    def last_rows(a):
        n = a.shape[1]
        a3 = a.reshape(n_seq, seq, n)
        return jnp.broadcast_to(a3[:, seq - 1:seq, :], (n_seq, seq, n)).reshape(R, n)

    def chunk(c0):
        b0 = 0
        t0 = c0 if n_seq == 1 else 0
        rows = slice(c0, c0 + R)
        cos = cos_ref[rows, :]
        sin = sin_ref[rows, :]

        def rotary(a_):
            return a_ * cos + pltpu.roll(a_, d // 2, 1) * sin

        def zc(j, h):
            return z_refs[j][rows, h * d:(h + 1) * d]

        def put(j, h, val):
            cols = slice(j * W + h * d, j * W + (h + 1) * d)
            if n_seq == 1:
                o_ref[0, t0:t0 + seq, cols] = val
            else:
                for b in range(n_seq):
                    o_ref[b, :, cols] = val[sl(b)]

        fpre = z_refs[1][rows, :]
        e = jnp.exp(-jnp.abs(fpre))
        a = prm_ref[0:1, :]
        b_ = prm_ref[1:2, :] + (jnp.minimum(fpre, 0.0) - jnp.log(1.0 + e))
        logf = jnp.maximum(a, b_) + jnp.log(1.0 + jnp.exp(-jnp.abs(a - b_)))
        k_all = prm_ref[2:3, :] * (jnp.where(fpre >= 0.0, e, 1.0) / (1.0 + e))
        G_all = _mask_dot(tri_l, logf)
        for h in range(H):
            hs = slice(h * d, (h + 1) * d)
            q, v, gate = zc(0, h), zc(2, h), zc(3, h)
            k, G = k_all[:, hs], G_all[:, hs]
            q_bf, k_bf = q.astype(BF16), k.astype(BF16)
            A = lvl_ref[0] * _dot_nt(q_bf, k_bf)
            for j in range(1, n_lvl):
                E = jnp.exp(-jnp.abs(G - _level_ref(G, 1 << j, row))).astype(BF16)
                A = A + lvl_ref[j] * _dot_nt(q_bf * E, k_bf * E)
            qg = q * jnp.exp(G)
            G_last = last_rows(G)
            kd = k * jnp.exp(G_last - G)
            o_h = _dot(A, v) + per_seq(lambda b: _dot(qg[sl(b)], sh_ref[b0 + b, h]))
            for b in range(n_seq):
                decay = jnp.exp(G_last[b * seq:b * seq + SUBLANES]).T[:, 0:1]
                sh_ref[b0 + b, h] = decay * sh_ref[b0 + b, h] + _dot_tn(kd[sl(b)], v[sl(b)])
            put(0, h, _head_rmsnorm(o_h, prm(3, h)) * _silu(gate))

        for h in range(H):
            q = rotary(zc(4, h))
            k = rotary(zc(5, h)) * (d ** -0.5)
            v, gate = zc(6, h), zc(7, h)
            A = _dot_nt(q, k) * dm_ref[h]
            o_h = _dot(A, v) + qin_ref[h] * per_seq(lambda b: _dot(q[sl(b)], sr_ref[b0 + b, h]))
            ko = k * kout_ref[h]
            for b in range(n_seq):
                sr_ref[b0 + b, h] = g_chunk[h] * sr_ref[b0 + b, h] + _dot_tn(ko[sl(b)], v[sl(b)])
            put(1, h, _head_layernorm(o_h, prm(4, h)) * _silu(gate))

        u = z_refs[8][rows, :]
        ext_rows = SUBLANES + seq
        for b in range(n_seq):
            base = b * ext_rows
            ext_ref[base + SUBLANES - (CONV_W - 1):base + SUBLANES, :] = cb_ref[b0 + b]
            ext_ref[base + SUBLANES:base + ext_rows, :] = u[sl(b)]
        conv = jnp.zeros((R, W), F32)
        for j in range(CONV_W):
            off = SUBLANES - (CONV_W - 1) + j
            conv = conv + convw_ref[j:j + 1, :] * per_seq(
                lambda b: ext_ref[b * ext_rows + off:b * ext_rows + off + seq, :])
        for b in range(n_seq):
            cb_ref[b0 + b] = ext_ref[(b + 1) * ext_rows - (CONV_W - 1):(b + 1) * ext_rows, :]
        uc = _silu(conv + prm_ref[7:8, :])

        qs, ks, vs = [], [], []
        for h in range(H):
            uch = uc[:, h * d:(h + 1) * d].astype(BF16)
            qs.append(jnp.dot(uch, mq_ref[h], preferred_element_type=F32))
            ks.append(jnp.dot(uch, mk_ref[h], preferred_element_type=F32))
            vs.append(jnp.dot(u[:, h * d:(h + 1) * d].astype(BF16), mv_ref[h], preferred_element_type=F32))
        qkv = jnp.concatenate(qs + ks + vs, axis=1).astype(BF16)
        g_col = jnp.dot(qkv, wg_ref[...], preferred_element_type=F32) + bg_ref[...]
        g_row = lax.dot_general(wgt_ref[...], qkv, (((1,), (1,)), ((), ())),
                                preferred_element_type=F32) + bgt_ref[...]
        i_cols = g_col[:, :H]
        f_cols = _mask_dot(tri_l, _log_sigmoid(g_col[:, H:]))
        i_rows = g_row[:H]
        f_rows = _dot_mask(_log_sigmoid(g_row[H:]), tri_u)
        m_prev_rows = per_seq(lambda b: jnp.broadcast_to(m_ref[b0 + b], (seq, H)))
        F_last = last_rows(f_cols)
        for h in range(H):
            q, k, v = qs[h], ks[h] * (d ** -0.5), vs[h]
            i_col, F_col = i_cols[:, h:h + 1], f_cols[:, h:h + 1]
            a_row = i_rows[h:h + 1] - f_rows[h:h + 1]
            m_prev = m_prev_rows[:, h:h + 1]
            cm = jnp.max(jnp.where(causal, a_row, NEG_INF), -1, keepdims=True)
            m_col = F_col + jnp.maximum(m_prev, cm)
            logd = (F_col - m_col) + a_row
            dmat = jnp.where(causal, jnp.exp(jnp.where(causal, logd, 0.0)), 0.0)
            Sc = _dot_nt(q, k) * dmat
            inter = jnp.exp(F_col + m_prev - m_col)
            n_rows = jnp.broadcast_to(n_ref[b0:b0 + n_seq, h:h + 1, :], (n_seq, seq, d)).reshape(R, d)
            num = _dot(Sc, v) + inter * per_seq(lambda b: _dot(q[sl(b)], c_ref[b0 + b, h]))
            den = jnp.sum(Sc, -1, keepdims=True) + inter * jnp.sum(q * n_rows, -1, keepdims=True)
            hout = num / jnp.maximum(jnp.abs(den), jnp.exp(-m_col))
            mL = last_rows(jnp.broadcast_to(m_col, (R, d)))
            FL = jnp.broadcast_to(F_last[:, h:h + 1], (R, d))
            kw = k * jnp.exp((FL - mL) + (i_col - F_col))
            decay_rows = jnp.exp((FL - mL) + m_prev)
            for b in range(n_seq):
                decay = decay_rows[b * seq:b * seq + 1]
                c_ref[b0 + b, h] = decay * c_ref[b0 + b, h] + _dot_tn(kw[sl(b)], v[sl(b)])
                n_ref[b0 + b, h:h + 1, :] = decay * n_ref[b0 + b, h:h + 1, :] + jnp.sum(kw[sl(b)], 0, keepdims=True)
                m_ref[b0 + b, :, h:h + 1] = mL[b * seq:b * seq + 1, 0:1]
            y = _head_layernorm(hout, prm(5, h)) + prm(6, h) * uc[:, h * d:(h + 1) * d]
            put(2, h, y * _silu(zc(9, h)))

    for c in range(bb * tt // R):
        chunk(c * R)


def _mixers(x, cos, sin, states_in, l, prev_out, p):
    B, T, D = x.shape
    depth = states_in[0].shape[0]
    H, d, W = N_HEADS, HEAD_DIM, MIX_WIDTH
    tt = _pick(T, (256, 128, 64, 32, 16, 8))
    n_t = T // tt
    if tt >= 128:
        bb, seq, n_seq = 1, 128, 1
    else:
        assert n_t == 1
        bb = _pick(B, tuple(c for c in (8, 4, 2) if c * tt <= 128) + (1,))
        seq, n_seq = tt, bb
        cos, sin = jnp.tile(cos, (bb, 1)), jnp.tile(sin, (bb, 1))
    R = seq * n_seq
    tables, g_chunk = _seq_tables(seq, n_seq)
    consts = [jnp.asarray(t) for t in tables]
    st_spec = pl.BlockSpec((None, bb, H, d, d), lambda b, t: (l, b, 0, 0, 0))
    n_spec = pl.BlockSpec((None, bb, H, d), lambda b, t: (l, b, 0, 0))
    m_spec = pl.BlockSpec((None, bb, 1, H), lambda b, t: (l, b, 0, 0))
    cb_spec = pl.BlockSpec((None, bb, CONV_W - 1, W), lambda b, t: (l, b, 0, 0))
    state_specs = [st_spec, st_spec, st_spec, n_spec, m_spec, cb_spec]

    def full(a):
        nd = a.ndim
        return pl.BlockSpec(a.shape, lambda b, t: (0,) * nd)

    def layer(a):
        nd = a.ndim - 1
        return pl.BlockSpec((None,) + a.shape[1:], lambda b, t: (l,) + (0,) * nd)

    weights = [p['prm'], p['conv_w'], p['mq'], p['mk'], p['mv'], p['m_wg'], p['m_wgt'], p['m_bg'], p['m_bgt']]
    alias_in = list(prev_out) if prev_out is not None else []
    n_fixed = 5 + len(consts) + 6 + len(weights)
    state_shapes = [jax.ShapeDtypeStruct((depth, B, H, d, d), F32)] * 3 + [
        jax.ShapeDtypeStruct((depth, B, H, d), F32),
        jax.ShapeDtypeStruct((depth, B, 1, H), F32),
        jax.ShapeDtypeStruct((depth, B, CONV_W - 1, W), F32)]
    rows_spec = pl.BlockSpec((bb * tt, d), lambda b, t: (t, 0))
    outs = pl.pallas_call(
        functools.partial(_mixer_kernel, tt=tt, seq=seq, n_seq=n_seq, n_alias=len(alias_in),
                          g_chunk=g_chunk),
        grid=(B // bb, n_t),
        in_specs=[pl.BlockSpec((bb, tt, D), lambda b, t: (b, t, 0)),
                  pl.BlockSpec((None, 1, D), lambda b, t: (l, 0, 0)),
                  pl.BlockSpec((None, D, SCAN_COLS), lambda b, t: (l, 0, 0), pipeline_mode=pl.Buffered(1)),
                  rows_spec, rows_spec]
                 + [full(c) for c in consts] + state_specs + [layer(w) for w in weights]
                 + [pl.BlockSpec(memory_space=pl.ANY)] * len(alias_in),
        out_specs=[pl.BlockSpec((bb, tt, N_BRANCH * W), lambda b, t: (b, t, 0))] + state_specs,
        out_shape=[jax.ShapeDtypeStruct((B, T, N_BRANCH * W), F32)] + state_shapes,
        input_output_aliases={n_fixed + i: 1 + i for i in range(len(alias_in))},
        scratch_shapes=[pltpu.VMEM((bb * tt, W), F32)] * (SCAN_COLS // W)
                       + [pltpu.VMEM((n_seq * (SUBLANES + seq), W), F32)],
        name="mixers",
        compiler_params=_cparams(("parallel", "arbitrary")),
    )(x, p['norm_mix'], p['w_in'], cos, sin, *consts, *states_in, *weights, *alias_in)
    return outs[0], tuple(outs[1:])


def _merge_kernel(x_ref, o_ref, nw_ref, wg0_ref, wg1_ref, wg2_ref, wbr_ref, wout_ref, y_ref):
    x = x_ref[...]
    W = MIX_WIDTH
    hn = _rms(x, nw_ref[...]).astype(BF16)
    merged = jnp.zeros(x.shape, F32)
    for n, wg_ref in enumerate((wg0_ref, wg1_ref, wg2_ref)):
        gz = jnp.dot(hn, wg_ref[...], preferred_element_type=F32)
        proj = jnp.dot(o_ref[:, n * W:(n + 1) * W].astype(BF16), wbr_ref[n], preferred_element_type=F32)
        merged = merged + _sigmoid(gz) * proj
    y_ref[...] = x + jnp.dot(merged.astype(BF16), wout_ref[...], preferred_element_type=F32)


def _merge(x, o, nw, w_in, wbr, wout, l):
    n, d = x.shape
    tm = _pick(n, (512, 256, 128))
    g0 = SCAN_COLS // d

    def gate_spec(k):
        return pl.BlockSpec((None, d, d), lambda i: (l, 0, g0 + k))

    return pl.pallas_call(
        _merge_kernel,
        grid=(n // tm,),
        in_specs=[pl.BlockSpec((tm, d), lambda i: (i, 0)),
                  pl.BlockSpec((tm, N_BRANCH * MIX_WIDTH), lambda i: (i, 0)),
                  pl.BlockSpec((None, 1, d), lambda i: (l, 0, 0)),
                  gate_spec(0), gate_spec(1), gate_spec(2),
                  pl.BlockSpec((None,) + wbr.shape[1:], lambda i: (l, 0, 0, 0)),
                  pl.BlockSpec((None,) + wout.shape[1:], lambda i: (l, 0, 0))],
        out_specs=pl.BlockSpec((tm, d), lambda i: (i, 0)),
        out_shape=jax.ShapeDtypeStruct((n, d), F32),
        name="merge",
        compiler_params=_cparams(("parallel",)),
    )(x, o, nw, w_in, w_in, w_in, wbr, wout)


def _router_kernel(x_ref, nw_ref, wr_ref, br_ref, hn_ref, mi_ref, mf_ref, cnt_ref, carry_ref):
    i = pl.program_id(0)
    tm = x_ref.shape[0]
    n_sub = x_ref.shape[1] // LANES

    @pl.when(i == 0)
    def _():
        carry_ref[...] = jnp.zeros(carry_ref.shape, F32)

    hn = _rms(x_ref[...], nw_ref[...])
    for s in range(n_sub):
        hn_ref[pl.ds(s, tm, stride=n_sub), :] = hn[:, s * LANES:(s + 1) * LANES]
    logits = _dot(hn, wr_ref[...]) + br_ref[...]
    lane = lax.broadcasted_iota(jnp.int32, logits.shape, 1)
    big = jnp.int32(1 << 20)

    def first_max(mask):
        vmax = jnp.max(jnp.where(mask, logits, NEG_INF), -1, keepdims=True)
        imax = jnp.min(jnp.where(mask, jnp.where(logits == vmax, lane, big), big), -1, keepdims=True)
        return vmax, imax

    in_groups = lane < N_GROUPS
    g_max, g_top = first_max(in_groups)
    g_w = 1.0 / jnp.sum(jnp.where(in_groups, jnp.exp(logits - g_max), 0.0), -1, keepdims=True)
    e_lo = N_GROUPS + EXPERTS_PER_GROUP * g_top
    in_e = jnp.logical_and(lane >= e_lo, lane < e_lo + EXPERTS_PER_GROUP)
    v1, i1 = first_max(in_e)
    v2, i2 = first_max(jnp.logical_and(in_e, lane != i1))
    e2 = jnp.exp(v2 - v1)
    w1 = g_w / (1.0 + e2)
    w2 = g_w * e2 / (1.0 + e2)

    hit1 = lane == i1
    hit2 = lane == i2
    onehot = jnp.where(hit1, 1.0, 0.0) + jnp.where(hit2, 1.0, 0.0)
    strict = (lax.broadcasted_iota(jnp.int32, (tm, tm), 0) > lax.broadcasted_iota(jnp.int32, (tm, tm), 1))
    before = _dot(strict.astype(F32), onehot) + carry_ref[...]
    r1 = jnp.sum(jnp.where(hit1, before, 0.0), -1, keepdims=True).astype(jnp.int32)
    r2 = jnp.sum(jnp.where(hit2, before, 0.0), -1, keepdims=True).astype(jnp.int32)
    carry_ref[...] = carry_ref[...] + jnp.sum(onehot, 0, keepdims=True)
    cnt_ref[...] = carry_ref[...]

    meta = jnp.where(lane == 0, i1 - N_GROUPS,
                     jnp.where(lane == 1, i2 - N_GROUPS,
                               jnp.where(lane == 2, r1, jnp.where(lane == 3, r2, 0))))
    mi_ref[...] = meta.T[:SUBLANES]
    mf_ref[...] = jnp.where(lane == 0, w1, jnp.where(lane == 1, w2, 0.0))


def _router(x, nw, wr, br, l):
    n, d = x.shape
    n_sub = d // LANES
    tm = _pick(n, (256, 128))
    return pl.pallas_call(
        _router_kernel,
        grid=(n // tm,),
        in_specs=[pl.BlockSpec((tm, d), lambda i: (i, 0)),
                  pl.BlockSpec((None, 1, d), lambda i: (l, 0, 0)),
                  pl.BlockSpec((None, d, LANES), lambda i: (l, 0, 0)),
                  pl.BlockSpec((None, 1, LANES), lambda i: (l, 0, 0))],
        out_specs=[pl.BlockSpec((tm * n_sub, LANES), lambda i: (i, 0)),
                   pl.BlockSpec((SUBLANES, tm), lambda i: (0, i)),
                   pl.BlockSpec((tm, LANES), lambda i: (i, 0)),
                   pl.BlockSpec((1, LANES), lambda i: (0, 0))],
        out_shape=[jax.ShapeDtypeStruct((n * n_sub, LANES), F32),
                   jax.ShapeDtypeStruct((SUBLANES, n), jnp.int32),
                   jax.ShapeDtypeStruct((n, LANES), F32),
                   jax.ShapeDtypeStruct((1, LANES), F32)],
        scratch_shapes=[pltpu.VMEM((1, LANES), F32)],
        name="router",
        compiler_params=_cparams(("arbitrary",)),
    )(x, nw, wr, br)


def _slots_kernel(d0_ref, d1_ref, pair_ref):
    def body(t, carry):
        pair_ref[d0_ref[t]] = TOP_K * t
        pair_ref[d1_ref[t]] = TOP_K * t + 1
        return carry

    lax.fori_loop(0, d0_ref.shape[0], body, 0, unroll=8)


def _slots(d0, d1, cap):
    return pl.pallas_call(
        _slots_kernel,
        grid_spec=pltpu.PrefetchScalarGridSpec(
            num_scalar_prefetch=2,
            grid=(1,),
            in_specs=[],
            out_specs=pl.BlockSpec(memory_space=pltpu.SMEM)),
        out_shape=jax.ShapeDtypeStruct((cap,), jnp.int32),
        name="slots",
        compiler_params=pltpu.CompilerParams(dimension_semantics=("arbitrary",)),
    )(d0, d1)


def _experts_kernel(be_ref, nv_ref, pair_ref, hn_ref, wg_ref, wu_ref, wd_ref, y2_ref,
                    xbuf, ybuf, gsem, ssem, wg_bf, wu_bf, wd_bf, *, n_sub):
    i = pl.program_id(0)
    n_i = pl.num_programs(0)
    slot = i % 2

    def row(r):
        return pl.ds(pl.multiple_of(r * n_sub, n_sub), n_sub)

    def gather(blk, sl, r):
        tok = jnp.right_shift(pair_ref[blk * MOE_ROWS + r], TOP_K.bit_length() - 1)
        return pltpu.make_async_copy(hn_ref.at[row(tok)], xbuf.at[sl, row(r)], gsem.at[sl])

    def scatter(blk, sl, r):
        return pltpu.make_async_copy(ybuf.at[sl, row(r)], y2_ref.at[row(pair_ref[blk * MOE_ROWS + r])], ssem.at[sl])

    def start_valid(copy, blk, sl):
        n_valid = nv_ref[blk]
        n_full = n_valid // ISSUE_UNROLL

        def body(j, carry):
            for k in range(ISSUE_UNROLL):
                copy(blk, sl, j * ISSUE_UNROLL + k).start()
            return carry

        def tail(r, carry):
            copy(blk, sl, r).start()
            return carry

        lax.fori_loop(0, n_full, body, 0)
        lax.fori_loop(n_full * ISSUE_UNROLL, n_valid, tail, 0)

    def wait_valid(buf, blk, sl):
        n_valid = nv_ref[blk]

        @pl.when(n_valid > 0)
        def _():
            rows = pl.ds(0, n_valid * n_sub)
            if buf is xbuf:
                pltpu.make_async_copy(hn_ref.at[rows], xbuf.at[sl, rows], gsem.at[sl]).wait()
            else:
                pltpu.make_async_copy(ybuf.at[sl, rows], y2_ref.at[rows], ssem.at[sl]).wait()

    @pl.when(i == 0)
    def _():
        xbuf[...] = jnp.zeros(xbuf.shape, F32)
        start_valid(gather, 0, 0)

    @pl.when(i + 1 < n_i)
    def _():
        start_valid(gather, i + 1, 1 - slot)

    wait_valid(xbuf, i, slot)

    @pl.when(i >= 2)
    def _():
        wait_valid(ybuf, i - 2, slot)

    @pl.when(nv_ref[i] > 0)
    def _():
        changed = jnp.logical_or(i == 0, be_ref[i] != be_ref[jnp.maximum(i - 1, 0)])

        @pl.when(changed)
        def _():
            wg_bf[...] = wg_ref[...].astype(BF16)
            wu_bf[...] = wu_ref[...].astype(BF16)
            wd_bf[...] = wd_ref[...].astype(BF16)

        x = jnp.concatenate([xbuf[slot, pl.ds(s, MOE_ROWS, stride=n_sub), :] for s in range(n_sub)], axis=1)
        x = x.astype(BF16)
        g = jnp.dot(x, wg_bf[...], preferred_element_type=F32)
        u = jnp.dot(x, wu_bf[...], preferred_element_type=F32)
        y = jnp.dot((_silu(g) * u).astype(BF16), wd_bf[...], preferred_element_type=F32)
        for s in range(n_sub):
            ybuf[slot, pl.ds(s, MOE_ROWS, stride=n_sub), :] = y[:, s * LANES:(s + 1) * LANES]

    start_valid(scatter, i, slot)

    @pl.when(i == n_i - 1)
    def _():
        @pl.when(i >= 1)
        def _():
            wait_valid(ybuf, i - 1, 1 - slot)
        wait_valid(ybuf, i, slot)


def _experts(block_e, n_valid, pairs, hn, w_g, w_u, w_d, l):
    _, _, d, f = w_g.shape
    n_sub = d // LANES
    n = hn.shape[0] // n_sub
    n_blocks = block_e.shape[0]
    buf = pltpu.VMEM((2, MOE_ROWS * n_sub, LANES), F32)
    return pl.pallas_call(
        functools.partial(_experts_kernel, n_sub=n_sub),
        grid_spec=pltpu.PrefetchScalarGridSpec(
            num_scalar_prefetch=3,
            grid=(n_blocks,),
            in_specs=[pl.BlockSpec(memory_space=pl.ANY),
                      pl.BlockSpec((None, None, d, f), lambda i, be, nv, pr: (l, be[i], 0, 0)),
                      pl.BlockSpec((None, None, d, f), lambda i, be, nv, pr: (l, be[i], 0, 0)),
                      pl.BlockSpec((None, None, f, d), lambda i, be, nv, pr: (l, be[i], 0, 0))],
            out_specs=pl.BlockSpec(memory_space=pl.ANY),
            scratch_shapes=[buf, buf, pltpu.SemaphoreType.DMA((2,)), pltpu.SemaphoreType.DMA((2,)),
                            pltpu.VMEM((d, f), BF16), pltpu.VMEM((d, f), BF16), pltpu.VMEM((f, d), BF16)]),
        out_shape=jax.ShapeDtypeStruct((n * TOP_K * n_sub, LANES), F32),
        name="experts",
        compiler_params=pltpu.CompilerParams(dimension_semantics=("arbitrary",), vmem_limit_bytes=VMEM_LIMIT,
                                             has_side_effects=True),
    )(block_e, n_valid, pairs, hn, w_g, w_u, w_d)


def _combine_kernel(x_ref, mf_ref, nw_ref, y2_ref, y_ref, *, final_norm):
    tm = x_ref.shape[0]
    n_sub = x_ref.shape[1] // LANES
    w = mf_ref[...]
    stride = TOP_K * n_sub
    y0 = jnp.concatenate([y2_ref[pl.ds(s, tm, stride=stride), :] for s in range(n_sub)], axis=1)
    y1 = jnp.concatenate([y2_ref[pl.ds(n_sub + s, tm, stride=stride), :] for s in range(n_sub)], axis=1)
    out = x_ref[...] + (y0 * w[:, 0:1] + y1 * w[:, 1:2])
    if final_norm:
        out = _rms(out, nw_ref[...])
    y_ref[...] = out


def _combine(x, mf, nw, y2, final_norm):
    n, d = x.shape
    n_sub = d // LANES
    tm = _pick(n, (256, 128))
    return pl.pallas_call(
        functools.partial(_combine_kernel, final_norm=final_norm),
        grid=(n // tm,),
        in_specs=[pl.BlockSpec((tm, d), lambda i: (i, 0)),
                  pl.BlockSpec((tm, LANES), lambda i: (i, 0)),
                  pl.BlockSpec((1, d), lambda i: (0, 0)),
                  pl.BlockSpec((tm * TOP_K * n_sub, LANES), lambda i: (i, 0))],
        out_specs=pl.BlockSpec((tm, d), lambda i: (i, 0)),
        out_shape=jax.ShapeDtypeStruct((n, d), F32),
        name="combine",
        compiler_params=_cparams(("parallel",)),
    )(x, mf, nw, y2)


def _moe(x, l, p, final_nw):
    n, d = x.shape
    n_sub = d // LANES
    hn, mi, mf, cnt = _router(x, p['norm_ffn'], p['w_r'], p['b_r'], l)
    counts = cnt[0, N_GROUPS:N_GROUPS + N_EXPERTS].astype(jnp.int32)
    padded = (counts + MOE_ROWS - 1) // MOE_ROWS * MOE_ROWS
    pad_end = jnp.cumsum(padded)
    pad_start = pad_end - padded
    experts = jnp.arange(N_EXPERTS, dtype=jnp.int32)

    def lookup(table, e):
        return jnp.sum(jnp.where(e[:, None] == experts[None, :], table[None, :], 0), axis=1)

    d0 = mi[2] + lookup(pad_start, mi[0])
    d1 = mi[3] + lookup(pad_start, mi[1])
    n_blocks = -(-(n * TOP_K) // MOE_ROWS) + N_EXPERTS
    block_row = jnp.arange(n_blocks, dtype=jnp.int32) * MOE_ROWS
    block_e = jnp.minimum(jnp.sum((pad_end[None, :] <= block_row[:, None]).astype(jnp.int32), axis=1), N_EXPERTS - 1)
    used_end = lookup(pad_start + counts, block_e)
    n_valid = jnp.clip(used_end - block_row, 0, MOE_ROWS).astype(jnp.int32)
    pairs = _slots(d0, d1, n_blocks * MOE_ROWS)
    y2 = _experts(block_e.astype(jnp.int32), n_valid, pairs, hn, p['w_eg'], p['w_eu'], p['w_ed'], l)
    nw = (final_nw if final_nw is not None else p['norm_ffn'][l, 0])[None]
    return _combine(x, mf, nw, y2, final_nw is not None)


def _trunk(x, pos, states, p):
    B, T, D = x.shape
    depth = p['w_in'].shape[0]
    half = HEAD_DIM // 2
    inv = ROPE_BASE ** (-jnp.arange(half, dtype=F32) / half)
    ang = pos.astype(F32)[:, None] * inv[None, :]
    cos = jnp.concatenate([jnp.cos(ang), jnp.cos(ang)], -1)
    sin = jnp.concatenate([-jnp.sin(ang), jnp.sin(ang)], -1)
    xf = x.reshape(B * T, D)
    new_states = None
    for l in range(depth):
        o, new_states = _mixers(xf.reshape(B, T, D), cos, sin, states, l, new_states, p)
        xf = _merge(xf, o.reshape(B * T, N_BRANCH * MIX_WIDTH), p['norm_mix'], p['w_in'], p['w_branch'], p['w_out'], l)
        xf = _moe(xf, l, p, p['norm_final'] if l == depth - 1 else None)
    sh, sr, c, n, m, cb = new_states
    return xf.reshape(B, T, D), (sh, sr, c, n, m.reshape(depth, B, N_HEADS), cb)


def kernel(x_prompt, x_sample, state_hgrn, state_ret, state_mlstm_C, state_mlstm_n, state_mlstm_m, state_mlstm_conv,
           norm_mix, norm_ffn, norm_final, w_in, hgrn_lb, hgrn_norm, ret_norm, mlstm_conv_w, mlstm_conv_b,
           mlstm_wq, mlstm_wk, mlstm_wv, mlstm_w_gates, mlstm_b_gates, mlstm_norm, mlstm_skip, w_branch, w_out,
           w_router_group, b_router_group, w_router_expert, b_router_expert, w_exp_gate, w_exp_up, w_exp_down):
    depth, D = norm_mix.shape
    H, d, W = N_HEADS, HEAD_DIM, MIX_WIDTH
    lb = jnp.cumsum(jax.nn.softmax(hgrn_lb.astype(F32), axis=0), axis=0)
    lb = lb - lb[0:1]
    prm = jnp.stack([jnp.log(lb), jnp.log1p(-lb), 1.0 - lb, hgrn_norm.astype(F32), ret_norm.astype(F32),
                     mlstm_norm.astype(F32), mlstm_skip.astype(F32), mlstm_conv_b.astype(F32)], axis=1)
    pad = LANES - N_GROUPS - N_EXPERTS
    w_r = jnp.concatenate([w_router_group, w_router_expert, jnp.zeros((depth, D, pad), F32)], -1)
    b_r = jnp.concatenate([b_router_group, b_router_expert, jnp.zeros((depth, pad), F32)], -1)[:, None, :]
    wg_bf = mlstm_w_gates.astype(BF16)
    p = {'norm_mix': norm_mix[:, None, :], 'norm_ffn': norm_ffn[:, None, :], 'norm_final': norm_final,
         'w_in': w_in.astype(BF16), 'prm': prm, 'conv_w': mlstm_conv_w.astype(F32),
         'mq': mlstm_wq.astype(BF16), 'mk': mlstm_wk.astype(BF16), 'mv': mlstm_wv.astype(BF16),
         'm_wg': wg_bf, 'm_wgt': jnp.swapaxes(wg_bf, 1, 2),
         'm_bg': mlstm_b_gates[:, None, :], 'm_bgt': mlstm_b_gates[:, :, None],
         'w_branch': w_branch.astype(BF16), 'w_out': w_out.astype(BF16),
         'w_r': w_r, 'b_r': b_r, 'w_eg': w_exp_gate, 'w_eu': w_exp_up, 'w_ed': w_exp_down}

    Bp, Tp = x_prompt.shape[0], x_prompt.shape[1]
    Bs = x_sample.shape[0]
    zero_states = (jnp.zeros((depth, Bp, H, d, d), F32), jnp.zeros((depth, Bp, H, d, d), F32),
                   jnp.zeros((depth, Bp, H, d, d), F32), jnp.zeros((depth, Bp, H, d), F32),
                   jnp.zeros((depth, Bp, 1, H), F32), jnp.zeros((depth, Bp, CONV_W - 1, W), F32))
    pos_prompt = jnp.arange(Tp, dtype=jnp.int32)
    pos_sample = PAST_LEN + jnp.arange(x_sample.shape[1], dtype=jnp.int32)
    y_prompt, ps = _trunk(x_prompt, pos_prompt, zero_states, p)
    sample_states = (state_hgrn, state_ret, state_mlstm_C, state_mlstm_n,
                     state_mlstm_m.reshape(depth, Bs, 1, H), state_mlstm_conv)
    y_sample, ss = _trunk(x_sample, pos_sample, sample_states, p)
    return (y_prompt, y_sample) + ps + ss
```

```python
import functools

import jax
import jax.numpy as jnp
import numpy as np
from jax import lax
from jax.experimental import pallas as pl
from jax.experimental.pallas import tpu as pltpu

F32 = jnp.float32
BF16 = jnp.bfloat16
HIGHEST = lax.Precision.HIGHEST

HEAD_DIM = 128
N_HEADS = 4
MIX_WIDTH = HEAD_DIM * N_HEADS
N_BRANCH = 3
CONV_W = 4
ROPE_BASE = 10000.0
N_GROUPS = 4
EXPERTS_PER_GROUP = 8
N_EXPERTS = N_GROUPS * EXPERTS_PER_GROUP
TOP_K = 2
PAST_LEN = 16384
NORM_EPS = 1e-6
HEAD_NORM_EPS = 1e-5
SCAN_COLS = 10 * MIX_WIDTH

LANES = 128
SUBLANES = 8
VMEM_LIMIT = 48 * 1024 * 1024
MOE_ROWS = 256
ISSUE_UNROLL = 8
NEG_INF = float("-inf")


def _cparams(sem):
    return pltpu.CompilerParams(dimension_semantics=sem, vmem_limit_bytes=VMEM_LIMIT)


def _pick(n, cands):
    for c in cands:
        if n % c == 0:
            return c
    return n


def _rms(x, w):
    return x * lax.rsqrt(jnp.mean(x * x, -1, keepdims=True) + NORM_EPS) * w


def _sigmoid(x):
    return 1.0 / (1.0 + jnp.exp(-x))


def _silu(x):
    return x * _sigmoid(x)


def _log_sigmoid(x):
    return jnp.minimum(x, 0.0) - jnp.log(1.0 + jnp.exp(-jnp.abs(x)))


def _dot(a, b):
    return jnp.dot(a.astype(BF16), b.astype(BF16), preferred_element_type=F32)


def _dot_nt(a, b):
    return lax.dot_general(a.astype(BF16), b.astype(BF16), (((1,), (1,)), ((), ())), preferred_element_type=F32)


def _dot_tn(a, b):
    return jnp.dot(a.T.astype(BF16), b.astype(BF16), preferred_element_type=F32)


def _split3(x):
    hi = x.astype(BF16)
    r = x - hi.astype(F32)
    mid = r.astype(BF16)
    lo = (r - mid.astype(F32)).astype(BF16)
    return hi, mid, lo


def _mask_dot(mask_bf, x):
    return sum(jnp.dot(mask_bf, part, preferred_element_type=F32) for part in _split3(x))


def _dot_mask(x, mask_bf):
    return sum(jnp.dot(part, mask_bf, preferred_element_type=F32) for part in _split3(x))


def _head_rmsnorm(o, w):
    return o * lax.rsqrt(jnp.mean(o * o, -1, keepdims=True) + HEAD_NORM_EPS) * w


def _head_layernorm(o, w):
    c = o - jnp.mean(o, -1, keepdims=True)
    return c * lax.rsqrt(jnp.mean(c * c, -1, keepdims=True) + HEAD_NORM_EPS) * w


def _level_masks(c):
    t = np.arange(c)[:, None]
    s = np.arange(c)[None, :]
    masks = [t == s]
    blk = 2
    while blk <= c:
        masks.append((t // blk == s // blk) & (t % blk >= blk // 2) & (s % blk < blk // 2))
        blk *= 2
    return np.stack(masks).astype(np.float32)


def _retention_tables(c):
    idx = np.arange(c, dtype=np.float64)
    lg = np.log(1.0 - 2.0 ** (-5.0 - np.arange(N_HEADS, dtype=np.float64)))[:, None, None]
    rel = idx[:, None] - idx[None, :]
    dm = np.where(rel >= 0, np.exp(np.maximum(rel, 0.0)[None] * lg), 0.0)
    q_in = np.broadcast_to(np.exp((idx + 1.0)[None, :, None] * lg), (N_HEADS, c, HEAD_DIM))
    k_out = np.broadcast_to(np.exp((c - 1.0 - idx)[None, :, None] * lg), (N_HEADS, c, HEAD_DIM))
    g_chunk = tuple(float(g) for g in np.exp(c * lg[:, 0, 0]))
    return dm.astype(np.float32), q_in.astype(np.float32), k_out.astype(np.float32), g_chunk


def _level_ref(G, blk, row):
    c, d = G.shape
    if blk == 2:
        return jnp.where((row & 1) == 1, pltpu.roll(G, 1, 0), G)
    if blk == 4:
        r = row & 3
        return jnp.where(r == 0, pltpu.roll(G, c - 1, 0),
                         jnp.where(r == 1, G, jnp.where(r == 2, pltpu.roll(G, 1, 0), pltpu.roll(G, 2, 0))))
    mid = blk // 2 - 1
    G3 = G.reshape(c // blk, blk, d)
    return jnp.broadcast_to(G3[:, mid:mid + 1, :], (c // blk, blk, d)).reshape(c, d)


def _seq_tables(seq, n_seq):
    rows = seq * n_seq
    t = np.arange(rows)[:, None]
    s = np.arange(rows)[None, :]
    same = (t // seq) == (s // seq)
    tri = np.stack([same & (t >= s), same & (t <= s)]).astype(np.float32)
    levels = _level_masks(seq)
    lvl = np.zeros((levels.shape[0], rows, rows), np.float32)
    dm1, q_in1, k_out1, g_chunk = _retention_tables(seq)
    dm = np.zeros((N_HEADS, rows, rows), np.float32)
    for b in range(n_seq):
        sl = slice(b * seq, (b + 1) * seq)
        lvl[:, sl, sl] = levels
        dm[:, sl, sl] = dm1
    q_in = np.tile(q_in1, (1, n_seq, 1))
    k_out = np.tile(k_out1, (1, n_seq, 1))
    return [lvl, tri, dm, q_in, k_out], g_chunk


def _mixer_kernel(*refs, tt, seq, n_seq, n_alias, g_chunk):
    (x_ref, nw_ref, w_ref, cos_ref, sin_ref, lvl_ref, tri_ref, dm_ref, qin_ref, kout_ref,
     sh0_ref, sr0_ref, c0_ref, n0_ref, m0_ref, cb0_ref,
     prm_ref, convw_ref, mq_ref, mk_ref, mv_ref, wg_ref, wgt_ref, bg_ref, bgt_ref) = refs[:25]
    o_ref, sh_ref, sr_ref, c_ref, n_ref, m_ref, cb_ref = refs[25 + n_alias:32 + n_alias]
    z_refs = refs[32 + n_alias:-1]
    ext_ref = refs[-1]
    ti = pl.program_id(1)
    d, W, H = HEAD_DIM, MIX_WIDTH, N_HEADS
    bb = sh_ref.shape[0]
    n_lvl = lvl_ref.shape[0]
    R = seq * n_seq

    @pl.when(ti == 0)
    def _():
        sh_ref[...] = sh0_ref[...]
        sr_ref[...] = sr0_ref[...]
        c_ref[...] = c0_ref[...]
        n_ref[...] = n0_ref[...]
        m_ref[...] = m0_ref[...]
        cb_ref[...] = cb0_ref[...]

    hn = _rms(x_ref[...], nw_ref[...]).astype(BF16)
    for j, zj_ref in enumerate(z_refs):
        zj_ref[...] = jnp.dot(hn, w_ref[:, j * W:(j + 1) * W], preferred_element_type=F32)

    def prm(i, h):
        return prm_ref[i:i + 1, h * d:(h + 1) * d]

    tri_l = tri_ref[0].astype(BF16)
    tri_u = tri_ref[1].astype(BF16)
    causal = tri_ref[0] > 0.0
    row = lax.broadcasted_iota(jnp.int32, (R, d), 0)

    def sl(b):
        return slice(b * seq, (b + 1) * seq)

    def per_seq(fn):
        parts = [fn(b) for b in range(n_seq)]
        return parts[0] if n_seq == 1 else jnp.concatenate(parts, axis=0)

    def last_rows(a):
        n = a.shape[1]
        a3 = a.reshape(n_seq, seq, n)
        return jnp.broadcast_to(a3[:, seq - 1:seq, :], (n_seq, seq, n)).reshape(R, n)

    def chunk(c0):
        b0 = 0
        t0 = c0 if n_seq == 1 else 0
        rows = slice(c0, c0 + R)
        cos = cos_ref[rows, :]
        sin = sin_ref[rows, :]

        def rotary(a_):
            return a_ * cos + pltpu.roll(a_, d // 2, 1) * sin

        def zc(j, h):
            return z_refs[j][rows, h * d:(h + 1) * d]

        def put(j, h, val):
            cols = slice(j * W + h * d, j * W + (h + 1) * d)
            if n_seq == 1:
                o_ref[0, t0:t0 + seq, cols] = val
            else:
                for b in range(n_seq):
                    o_ref[b, :, cols] = val[sl(b)]

        fpre = z_refs[1][rows, :]
        e = jnp.exp(-jnp.abs(fpre))
        a = prm_ref[0:1, :]
        b_ = prm_ref[1:2, :] + (jnp.minimum(fpre, 0.0) - jnp.log(1.0 + e))
        logf = jnp.maximum(a, b_) + jnp.log(1.0 + jnp.exp(-jnp.abs(a - b_)))
        k_all = prm_ref[2:3, :] * (jnp.where(fpre >= 0.0, e, 1.0) / (1.0 + e))
        G_all = _mask_dot(tri_l, logf)
        for h in range(H):
            hs = slice(h * d, (h + 1) * d)
            q, v, gate = zc(0, h), zc(2, h), zc(3, h)
            k, G = k_all[:, hs], G_all[:, hs]
            q_bf, k_bf = q.astype(BF16), k.astype(BF16)
            A = lvl_ref[0] * _dot_nt(q_bf, k_bf)
            for j in range(1, n_lvl):
                E = jnp.exp(-jnp.abs(G - _level_ref(G, 1 << j, row))).astype(BF16)
                A = A + lvl_ref[j] * _dot_nt(q_bf * E, k_bf * E)
            qg = q * jnp.exp(G)
            G_last = last_rows(G)
            kd = k * jnp.exp(G_last - G)
            o_h = _dot(A, v) + per_seq(lambda b: _dot(qg[sl(b)], sh_ref[b0 + b, h]))
            for b in range(n_seq):
                decay = jnp.exp(G_last[b * seq:b * seq + SUBLANES]).T[:, 0:1]
                sh_ref[b0 + b, h] = decay * sh_ref[b0 + b, h] + _dot_tn(kd[sl(b)], v[sl(b)])
            put(0, h, _head_rmsnorm(o_h, prm(3, h)) * _silu(gate))

        for h in range(H):
            q = rotary(zc(4, h))
            k = rotary(zc(5, h)) * (d ** -0.5)
            v, gate = zc(6, h), zc(7, h)
            A = _dot_nt(q, k) * dm_ref[h]
            o_h = _dot(A, v) + qin_ref[h] * per_seq(lambda b: _dot(q[sl(b)], sr_ref[b0 + b, h]))
            ko = k * kout_ref[h]
            for b in range(n_seq):
                sr_ref[b0 + b, h] = g_chunk[h] * sr_ref[b0 + b, h] + _dot_tn(ko[sl(b)], v[sl(b)])
            put(1, h, _head_layernorm(o_h, prm(4, h)) * _silu(gate))

        u = z_refs[8][rows, :]
        ext_rows = SUBLANES + seq
        for b in range(n_seq):
            base = b * ext_rows
            ext_ref[base + SUBLANES - (CONV_W - 1):base + SUBLANES, :] = cb_ref[b0 + b]
            ext_ref[base + SUBLANES:base + ext_rows, :] = u[sl(b)]
        conv = jnp.zeros((R, W), F32)
        for j in range(CONV_W):
            off = SUBLANES - (CONV_W - 1) + j
            conv = conv + convw_ref[j:j + 1, :] * per_seq(
                lambda b: ext_ref[b * ext_rows + off:b * ext_rows + off + seq, :])
        for b in range(n_seq):
            cb_ref[b0 + b] = ext_ref[(b + 1) * ext_rows - (CONV_W - 1):(b + 1) * ext_rows, :]
        uc = _silu(conv + prm_ref[7:8, :])

        qs, ks, vs = [], [], []
        for h in range(H):
            uch = uc[:, h * d:(h + 1) * d].astype(BF16)
            qs.append(jnp.dot(uch, mq_ref[h], preferred_element_type=F32))
            ks.append(jnp.dot(uch, mk_ref[h], preferred_element_type=F32))
            vs.append(jnp.dot(u[:, h * d:(h + 1) * d].astype(BF16), mv_ref[h], preferred_element_type=F32))
        qkv = jnp.concatenate(qs + ks + vs, axis=1).astype(BF16)
        g_col = jnp.dot(qkv, wg_ref[...], preferred_element_type=F32) + bg_ref[...]
        g_row = lax.dot_general(wgt_ref[...], qkv, (((1,), (1,)), ((), ())),
                                preferred_element_type=F32) + bgt_ref[...]
        i_cols = g_col[:, :H]
        f_cols = _mask_dot(tri_l, _log_sigmoid(g_col[:, H:]))
        i_rows = g_row[:H]
        f_rows = _dot_mask(_log_sigmoid(g_row[H:]), tri_u)
        m_prev_rows = per_seq(lambda b: jnp.broadcast_to(m_ref[b0 + b], (seq, H)))
        F_last = last_rows(f_cols)
        for h in range(H):
            q, k, v = qs[h], ks[h] * (d ** -0.5), vs[h]
            i_col, F_col = i_cols[:, h:h + 1], f_cols[:, h:h + 1]
            a_row = i_rows[h:h + 1] - f_rows[h:h + 1]
            m_prev = m_prev_rows[:, h:h + 1]
            cm = jnp.max(jnp.where(causal, a_row, NEG_INF), -1, keepdims=True)
            m_col = F_col + jnp.maximum(m_prev, cm)
            logd = (F_col - m_col) + a_row
            dmat = jnp.where(causal, jnp.exp(jnp.where(causal, logd, 0.0)), 0.0)
            Sc = _dot_nt(q, k) * dmat
            inter = jnp.exp(F_col + m_prev - m_col)
            n_rows = jnp.broadcast_to(n_ref[b0:b0 + n_seq, h:h + 1, :], (n_seq, seq, d)).reshape(R, d)
            num = _dot(Sc, v) + inter * per_seq(lambda b: _dot(q[sl(b)], c_ref[b0 + b, h]))
            den = jnp.sum(Sc, -1, keepdims=True) + inter * jnp.sum(q * n_rows, -1, keepdims=True)
            hout = num / jnp.maximum(jnp.abs(den), jnp.exp(-m_col))
            mL = last_rows(jnp.broadcast_to(m_col, (R, d)))
            FL = jnp.broadcast_to(F_last[:, h:h + 1], (R, d))
            kw = k * jnp.exp((FL - mL) + (i_col - F_col))
            decay_rows = jnp.exp((FL - mL) + m_prev)
            for b in range(n_seq):
                decay = decay_rows[b * seq:b * seq + 1]
                c_ref[b0 + b, h] = decay * c_ref[b0 + b, h] + _dot_tn(kw[sl(b)], v[sl(b)])
                n_ref[b0 + b, h:h + 1, :] = decay * n_ref[b0 + b, h:h + 1, :] + jnp.sum(kw[sl(b)], 0, keepdims=True)
                m_ref[b0 + b, :, h:h + 1] = mL[b * seq:b * seq + 1, 0:1]
            y = _head_layernorm(hout, prm(5, h)) + prm(6, h) * uc[:, h * d:(h + 1) * d]
            put(2, h, y * _silu(zc(9, h)))

    for c in range(bb * tt // R):
        chunk(c * R)


def _mixers(x, x_row0, B, T, cos, sin, states_in, l, prev_out, p):
    D = x.shape[1]
    depth = states_in[0].shape[0]
    H, d, W = N_HEADS, HEAD_DIM, MIX_WIDTH
    tt = _pick(T, (256, 128, 64, 32, 16, 8))
    n_t = T // tt
    if tt >= 128:
        bb, seq, n_seq = 1, 128, 1
    else:
        assert n_t == 1
        bb = _pick(B, tuple(c for c in (8, 4, 2) if c * tt <= 128) + (1,))
        seq, n_seq = tt, bb
        cos, sin = jnp.tile(cos, (bb, 1)), jnp.tile(sin, (bb, 1))
    assert x_row0 % (bb * tt) == 0
    xb = x_row0 // (bb * tt)
    R = seq * n_seq
    tables, g_chunk = _seq_tables(seq, n_seq)
    consts = [jnp.asarray(t) for t in tables]
    st_spec = pl.BlockSpec((None, bb, H, d, d), lambda b, t: (l, b, 0, 0, 0))
    n_spec = pl.BlockSpec((None, bb, H, d), lambda b, t: (l, b, 0, 0))
    m_spec = pl.BlockSpec((None, bb, 1, H), lambda b, t: (l, b, 0, 0))
    cb_spec = pl.BlockSpec((None, bb, CONV_W - 1, W), lambda b, t: (l, b, 0, 0))
    state_specs = [st_spec, st_spec, st_spec, n_spec, m_spec, cb_spec]

    def full(a):
        nd = a.ndim
        return pl.BlockSpec(a.shape, lambda b, t: (0,) * nd)

    def layer(a):
        nd = a.ndim - 1
        return pl.BlockSpec((None,) + a.shape[1:], lambda b, t: (l,) + (0,) * nd)

    weights = [p['prm'], p['conv_w'], p['mq'], p['mk'], p['mv'], p['m_wg'], p['m_wgt'], p['m_bg'], p['m_bgt']]
    alias_in = list(prev_out) if prev_out is not None else []
    n_fixed = 5 + len(consts) + 6 + len(weights)
    state_shapes = [jax.ShapeDtypeStruct((depth, B, H, d, d), F32)] * 3 + [
        jax.ShapeDtypeStruct((depth, B, H, d), F32),
        jax.ShapeDtypeStruct((depth, B, 1, H), F32),
        jax.ShapeDtypeStruct((depth, B, CONV_W - 1, W), F32)]
    rows_spec = pl.BlockSpec((bb * tt, d), lambda b, t: (t, 0))
    outs = pl.pallas_call(
        functools.partial(_mixer_kernel, tt=tt, seq=seq, n_seq=n_seq, n_alias=len(alias_in),
                          g_chunk=g_chunk),
        grid=(B // bb, n_t),
        in_specs=[pl.BlockSpec((bb * tt, D), lambda b, t: (xb + b * n_t + t, 0)),
                  pl.BlockSpec((None, 1, D), lambda b, t: (l, 0, 0)),
                  pl.BlockSpec((None, D, SCAN_COLS), lambda b, t: (l, 0, 0), pipeline_mode=pl.Buffered(1)),
                  rows_spec, rows_spec]
                 + [full(c) for c in consts] + state_specs + [layer(w) for w in weights]
                 + [pl.BlockSpec(memory_space=pl.ANY)] * len(alias_in),
        out_specs=[pl.BlockSpec((bb, tt, N_BRANCH * W), lambda b, t: (b, t, 0))] + state_specs,
        out_shape=[jax.ShapeDtypeStruct((B, T, N_BRANCH * W), F32)] + state_shapes,
        input_output_aliases={n_fixed + i: 1 + i for i in range(len(alias_in))},
        scratch_shapes=[pltpu.VMEM((bb * tt, W), F32)] * (SCAN_COLS // W)
                       + [pltpu.VMEM((n_seq * (SUBLANES + seq), W), F32)],
        name="mixers",
        compiler_params=_cparams(("parallel", "arbitrary")),
    )(x, p['norm_mix'], p['w_in'], cos, sin, *consts, *states_in, *weights, *alias_in)
    return outs[0], tuple(outs[1:])


def _merge_kernel(x_ref, o_ref, nw_ref, wg0_ref, wg1_ref, wg2_ref, wbr_ref, wout_ref, *rest):
    y_ref = rest[-1]
    x = x_ref[...]
    W = MIX_WIDTH
    hn = _rms(x, nw_ref[...]).astype(BF16)
    merged = jnp.zeros(x.shape, F32)
    for n, wg_ref in enumerate((wg0_ref, wg1_ref, wg2_ref)):
        gz = jnp.dot(hn, wg_ref[...], preferred_element_type=F32)
        proj = jnp.dot(o_ref[:, n * W:(n + 1) * W].astype(BF16), wbr_ref[n], preferred_element_type=F32)
        merged = merged + _sigmoid(gz) * proj
    y_ref[...] = x + jnp.dot(merged.astype(BF16), wout_ref[...], preferred_element_type=F32)


def _merge(x, x_row0, o, nw, w_in, wbr, wout, l, out, out_row0, out_rows):
    n = o.shape[0]
    d = x.shape[1]
    tm = _pick(n, (512, 256, 128))
    assert x_row0 % tm == 0 and out_row0 % tm == 0
    xb, ob = x_row0 // tm, out_row0 // tm
    g0 = SCAN_COLS // d

    def gate_spec(k):
        return pl.BlockSpec((None, d, d), lambda i: (l, 0, g0 + k))

    alias_in = [] if out is None else [out]
    return pl.pallas_call(
        _merge_kernel,
        grid=(n // tm,),
        in_specs=[pl.BlockSpec((tm, d), lambda i: (xb + i, 0)),
                  pl.BlockSpec((tm, N_BRANCH * MIX_WIDTH), lambda i: (i, 0)),
                  pl.BlockSpec((None, 1, d), lambda i: (l, 0, 0)),
                  gate_spec(0), gate_spec(1), gate_spec(2),
                  pl.BlockSpec((None,) + wbr.shape[1:], lambda i: (l, 0, 0, 0)),
                  pl.BlockSpec((None,) + wout.shape[1:], lambda i: (l, 0, 0))]
                 + [pl.BlockSpec(memory_space=pl.ANY)] * len(alias_in),
        out_specs=pl.BlockSpec((tm, d), lambda i: (ob + i, 0)),
        out_shape=jax.ShapeDtypeStruct((out_rows, d), F32),
        input_output_aliases={8: 0} if alias_in else {},
        name="merge",
        compiler_params=_cparams(("parallel",)),
    )(x, o, nw, w_in, w_in, w_in, wbr, wout, *alias_in)


def _router_kernel(x_ref, nw_ref, wr_ref, br_ref, hn_ref, mi_ref, mf_ref, cnt_ref, carry_ref):
    i = pl.program_id(0)
    tm = x_ref.shape[0]
    n_sub = x_ref.shape[1] // LANES

    @pl.when(i == 0)
    def _():
        carry_ref[...] = jnp.zeros(carry_ref.shape, F32)

    hn = _rms(x_ref[...], nw_ref[...])
    for s in range(n_sub):
        hn_ref[pl.ds(s, tm, stride=n_sub), :] = hn[:, s * LANES:(s + 1) * LANES]
    logits = _dot(hn, wr_ref[...]) + br_ref[...]
    lane = lax.broadcasted_iota(jnp.int32, logits.shape, 1)
    big = jnp.int32(1 << 20)

    def first_max(mask):
        vmax = jnp.max(jnp.where(mask, logits, NEG_INF), -1, keepdims=True)
        imax = jnp.min(jnp.where(mask, jnp.where(logits == vmax, lane, big), big), -1, keepdims=True)
        return vmax, imax

    in_groups = lane < N_GROUPS
    g_max, g_top = first_max(in_groups)
    g_w = 1.0 / jnp.sum(jnp.where(in_groups, jnp.exp(logits - g_max), 0.0), -1, keepdims=True)
    e_lo = N_GROUPS + EXPERTS_PER_GROUP * g_top
    in_e = jnp.logical_and(lane >= e_lo, lane < e_lo + EXPERTS_PER_GROUP)
    v1, i1 = first_max(in_e)
    v2, i2 = first_max(jnp.logical_and(in_e, lane != i1))
    e2 = jnp.exp(v2 - v1)
    w1 = g_w / (1.0 + e2)
    w2 = g_w * e2 / (1.0 + e2)

    hit1 = lane == i1
    hit2 = lane == i2
    onehot = jnp.where(hit1, 1.0, 0.0) + jnp.where(hit2, 1.0, 0.0)
    strict = (lax.broadcasted_iota(jnp.int32, (tm, tm), 0) > lax.broadcasted_iota(jnp.int32, (tm, tm), 1))
    before = _dot(strict.astype(F32), onehot) + carry_ref[...]
    r1 = jnp.sum(jnp.where(hit1, before, 0.0), -1, keepdims=True).astype(jnp.int32)
    r2 = jnp.sum(jnp.where(hit2, before, 0.0), -1, keepdims=True).astype(jnp.int32)
    carry_ref[...] = carry_ref[...] + jnp.sum(onehot, 0, keepdims=True)
    cnt_ref[...] = carry_ref[...]

    meta = jnp.where(lane == 0, i1 - N_GROUPS,
                     jnp.where(lane == 1, i2 - N_GROUPS,
                               jnp.where(lane == 2, r1, jnp.where(lane == 3, r2, 0))))
    mi_ref[...] = meta.T[:SUBLANES]
    mf_ref[...] = jnp.where(lane == 0, w1, jnp.where(lane == 1, w2, 0.0))


def _router(x, nw, wr, br, l):
    n, d = x.shape
    n_sub = d // LANES
    tm = _pick(n, (256, 128))
    return pl.pallas_call(
        _router_kernel,
        grid=(n // tm,),
        in_specs=[pl.BlockSpec((tm, d), lambda i: (i, 0)),
                  pl.BlockSpec((None, 1, d), lambda i: (l, 0, 0)),
                  pl.BlockSpec((None, d, LANES), lambda i: (l, 0, 0)),
                  pl.BlockSpec((None, 1, LANES), lambda i: (l, 0, 0))],
        out_specs=[pl.BlockSpec((tm * n_sub, LANES), lambda i: (i, 0)),
                   pl.BlockSpec((SUBLANES, tm), lambda i: (0, i)),
                   pl.BlockSpec((tm, LANES), lambda i: (i, 0)),
                   pl.BlockSpec((1, LANES), lambda i: (0, 0))],
        out_shape=[jax.ShapeDtypeStruct((n * n_sub, LANES), F32),
                   jax.ShapeDtypeStruct((SUBLANES, n), jnp.int32),
                   jax.ShapeDtypeStruct((n, LANES), F32),
                   jax.ShapeDtypeStruct((1, LANES), F32)],
        scratch_shapes=[pltpu.VMEM((1, LANES), F32)],
        name="router",
        compiler_params=_cparams(("arbitrary",)),
    )(x, nw, wr, br)


def _slots_kernel(d0_ref, d1_ref, pair_ref):
    def body(t, carry):
        pair_ref[d0_ref[t]] = TOP_K * t
        pair_ref[d1_ref[t]] = TOP_K * t + 1
        return carry

    lax.fori_loop(0, d0_ref.shape[0], body, 0, unroll=8)


def _slots(d0, d1, cap):
    return pl.pallas_call(
        _slots_kernel,
        grid_spec=pltpu.PrefetchScalarGridSpec(
            num_scalar_prefetch=2,
            grid=(1,),
            in_specs=[],
            out_specs=pl.BlockSpec(memory_space=pltpu.SMEM)),
        out_shape=jax.ShapeDtypeStruct((cap,), jnp.int32),
        name="slots",
        compiler_params=pltpu.CompilerParams(dimension_semantics=("arbitrary",)),
    )(d0, d1)


def _experts_kernel(be_ref, nv_ref, pair_ref, hn_ref, wg_ref, wu_ref, wd_ref, y2_ref,
                    xbuf, ybuf, gsem, ssem, wg_bf, wu_bf, wd_bf, *, n_sub):
    i = pl.program_id(0)
    n_i = pl.num_programs(0)
    slot = i % 2

    def row(r):
        return pl.ds(pl.multiple_of(r * n_sub, n_sub), n_sub)

    def gather(blk, sl, r):
        tok = jnp.right_shift(pair_ref[blk * MOE_ROWS + r], TOP_K.bit_length() - 1)
        return pltpu.make_async_copy(hn_ref.at[row(tok)], xbuf.at[sl, row(r)], gsem.at[sl])

    def scatter(blk, sl, r):
        return pltpu.make_async_copy(ybuf.at[sl, row(r)], y2_ref.at[row(pair_ref[blk * MOE_ROWS + r])], ssem.at[sl])

    def start_valid(copy, blk, sl):
        n_valid = nv_ref[blk]
        n_full = n_valid // ISSUE_UNROLL

        def body(j, carry):
            for k in range(ISSUE_UNROLL):
                copy(blk, sl, j * ISSUE_UNROLL + k).start()
            return carry

        def tail(r, carry):
            copy(blk, sl, r).start()
            return carry

        lax.fori_loop(0, n_full, body, 0)
        lax.fori_loop(n_full * ISSUE_UNROLL, n_valid, tail, 0)

    def wait_valid(buf, blk, sl):
        n_valid = nv_ref[blk]

        @pl.when(n_valid > 0)
        def _():
            rows = pl.ds(0, n_valid * n_sub)
            if buf is xbuf:
                pltpu.make_async_copy(hn_ref.at[rows], xbuf.at[sl, rows], gsem.at[sl]).wait()
            else:
                pltpu.make_async_copy(ybuf.at[sl, rows], y2_ref.at[rows], ssem.at[sl]).wait()

    @pl.when(i == 0)
    def _():
        xbuf[...] = jnp.zeros(xbuf.shape, F32)
        start_valid(gather, 0, 0)

    @pl.when(i + 1 < n_i)
    def _():
        start_valid(gather, i + 1, 1 - slot)

    wait_valid(xbuf, i, slot)

    @pl.when(i >= 2)
    def _():
        wait_valid(ybuf, i - 2, slot)

    @pl.when(nv_ref[i] > 0)
    def _():
        changed = jnp.logical_or(i == 0, be_ref[i] != be_ref[jnp.maximum(i - 1, 0)])

        @pl.when(changed)
        def _():
            wg_bf[...] = wg_ref[...].astype(BF16)
            wu_bf[...] = wu_ref[...].astype(BF16)
            wd_bf[...] = wd_ref[...].astype(BF16)

        x = jnp.concatenate([xbuf[slot, pl.ds(s, MOE_ROWS, stride=n_sub), :] for s in range(n_sub)], axis=1)
        x = x.astype(BF16)
        g = jnp.dot(x, wg_bf[...], preferred_element_type=F32)
        u = jnp.dot(x, wu_bf[...], preferred_element_type=F32)
        y = jnp.dot((_silu(g) * u).astype(BF16), wd_bf[...], preferred_element_type=F32)
        for s in range(n_sub):
            ybuf[slot, pl.ds(s, MOE_ROWS, stride=n_sub), :] = y[:, s * LANES:(s + 1) * LANES]

    start_valid(scatter, i, slot)

    @pl.when(i == n_i - 1)
    def _():
        @pl.when(i >= 1)
        def _():
            wait_valid(ybuf, i - 1, 1 - slot)
        wait_valid(ybuf, i, slot)


def _experts(block_e, n_valid, pairs, hn, w_g, w_u, w_d, l):
    _, _, d, f = w_g.shape
    n_sub = d // LANES
    n = hn.shape[0] // n_sub
    n_blocks = block_e.shape[0]
    buf = pltpu.VMEM((2, MOE_ROWS * n_sub, LANES), F32)
    return pl.pallas_call(
        functools.partial(_experts_kernel, n_sub=n_sub),
        grid_spec=pltpu.PrefetchScalarGridSpec(
            num_scalar_prefetch=3,
            grid=(n_blocks,),
            in_specs=[pl.BlockSpec(memory_space=pl.ANY),
                      pl.BlockSpec((None, None, d, f), lambda i, be, nv, pr: (l, be[i], 0, 0)),
                      pl.BlockSpec((None, None, d, f), lambda i, be, nv, pr: (l, be[i], 0, 0)),
                      pl.BlockSpec((None, None, f, d), lambda i, be, nv, pr: (l, be[i], 0, 0))],
            out_specs=pl.BlockSpec(memory_space=pl.ANY),
            scratch_shapes=[buf, buf, pltpu.SemaphoreType.DMA((2,)), pltpu.SemaphoreType.DMA((2,)),
                            pltpu.VMEM((d, f), BF16), pltpu.VMEM((d, f), BF16), pltpu.VMEM((f, d), BF16)]),
        out_shape=jax.ShapeDtypeStruct((n * TOP_K * n_sub, LANES), F32),
        name="experts",
        compiler_params=pltpu.CompilerParams(dimension_semantics=("arbitrary",), vmem_limit_bytes=VMEM_LIMIT,
                                             has_side_effects=True),
    )(block_e, n_valid, pairs, hn, w_g, w_u, w_d)


def _combine_kernel(x_ref, mf_ref, nw_ref, y2_ref, y_ref, *, final_norm):
    tm = x_ref.shape[0]
    n_sub = x_ref.shape[1] // LANES
    w = mf_ref[...]
    stride = TOP_K * n_sub
    y0 = jnp.concatenate([y2_ref[pl.ds(s, tm, stride=stride), :] for s in range(n_sub)], axis=1)
    y1 = jnp.concatenate([y2_ref[pl.ds(n_sub + s, tm, stride=stride), :] for s in range(n_sub)], axis=1)
    out = x_ref[...] + (y0 * w[:, 0:1] + y1 * w[:, 1:2])
    if final_norm:
        out = _rms(out, nw_ref[...])
    y_ref[...] = out


def _combine(x, mf, nw, y2, final_norm, row0, n):
    d = x.shape[1]
    n_sub = d // LANES
    tm = _pick(n, (256, 128))
    assert row0 % tm == 0
    rb = row0 // tm
    return pl.pallas_call(
        functools.partial(_combine_kernel, final_norm=final_norm),
        grid=(n // tm,),
        in_specs=[pl.BlockSpec((tm, d), lambda i: (rb + i, 0)),
                  pl.BlockSpec((tm, LANES), lambda i: (rb + i, 0)),
                  pl.BlockSpec((1, d), lambda i: (0, 0)),
                  pl.BlockSpec((tm * TOP_K * n_sub, LANES), lambda i: (rb + i, 0))],
        out_specs=pl.BlockSpec((tm, d), lambda i: (i, 0)),
        out_shape=jax.ShapeDtypeStruct((n, d), F32),
        name="combine",
        compiler_params=_cparams(("parallel",)),
    )(x, mf, nw, y2)


def _moe(x, l, p, final_nw, out_ranges):
    n, d = x.shape
    n_sub = d // LANES
    hn, mi, mf, cnt = _router(x, p['norm_ffn'], p['w_r'], p['b_r'], l)
    counts = cnt[0, N_GROUPS:N_GROUPS + N_EXPERTS].astype(jnp.int32)
    padded = (counts + MOE_ROWS - 1) // MOE_ROWS * MOE_ROWS
    pad_end = jnp.cumsum(padded)
    pad_start = pad_end - padded
    experts = jnp.arange(N_EXPERTS, dtype=jnp.int32)

    def lookup(table, e):
        return jnp.sum(jnp.where(e[:, None] == experts[None, :], table[None, :], 0), axis=1)

    d0 = mi[2] + lookup(pad_start, mi[0])
    d1 = mi[3] + lookup(pad_start, mi[1])
    n_blocks = -(-(n * TOP_K) // MOE_ROWS) + N_EXPERTS
    block_row = jnp.arange(n_blocks, dtype=jnp.int32) * MOE_ROWS
    block_e = jnp.minimum(jnp.sum((pad_end[None, :] <= block_row[:, None]).astype(jnp.int32), axis=1), N_EXPERTS - 1)
    used_end = lookup(pad_start + counts, block_e)
    n_valid = jnp.clip(used_end - block_row, 0, MOE_ROWS).astype(jnp.int32)
    pairs = _slots(d0, d1, n_blocks * MOE_ROWS)
    y2 = _experts(block_e.astype(jnp.int32), n_valid, pairs, hn, p['w_eg'], p['w_eu'], p['w_ed'], l)
    nw = (final_nw if final_nw is not None else p['norm_ffn'][l, 0])[None]
    return [_combine(x, mf, nw, y2, final_nw is not None, row0, rows) for row0, rows in out_ranges]


def _rotary_tables(pos):
    half = HEAD_DIM // 2
    inv = ROPE_BASE ** (-jnp.arange(half, dtype=F32) / half)
    ang = pos.astype(F32)[:, None] * inv[None, :]
    cos = jnp.concatenate([jnp.cos(ang), jnp.cos(ang)], -1)
    sin = jnp.concatenate([-jnp.sin(ang), jnp.sin(ang)], -1)
    return cos, sin


def _trunks(xs, positions, states, p):
    depth = p['w_in'].shape[0]
    D = xs[0].shape[-1]
    shapes = [x.shape[:2] for x in xs]
    n_rows = [B * T for B, T in shapes]
    row0 = [sum(n_rows[:i]) for i in range(len(xs))]
    total = sum(n_rows)
    tables = [_rotary_tables(pos) for pos in positions]
    x_arrays = [x.reshape(n, D) for x, n in zip(xs, n_rows)]
    x_row0 = [0] * len(xs)
    new_states = [None] * len(xs)
    for l in range(depth):
        xm = None
        for i, (B, T) in enumerate(shapes):
            o, new_states[i] = _mixers(x_arrays[i], x_row0[i], B, T, *tables[i], states[i], l, new_states[i], p)
            xm = _merge(x_arrays[i], x_row0[i], o.reshape(n_rows[i], N_BRANCH * MIX_WIDTH), p['norm_mix'], p['w_in'],
                        p['w_branch'], p['w_out'], l, xm, row0[i], total)
        if l == depth - 1:
            outs = _moe(xm, l, p, p['norm_final'], list(zip(row0, n_rows)))
        else:
            x_all, = _moe(xm, l, p, None, [(0, total)])
            x_arrays, x_row0 = [x_all] * len(xs), row0
    results = []
    for y, (B, T), st in zip(outs, shapes, new_states):
        sh, sr, c, n, m, cb = st
        results.append((y.reshape(B, T, D), (sh, sr, c, n, m.reshape(depth, B, N_HEADS), cb)))
    return results


def kernel(x_prompt, x_sample, state_hgrn, state_ret, state_mlstm_C, state_mlstm_n, state_mlstm_m, state_mlstm_conv,
           norm_mix, norm_ffn, norm_final, w_in, hgrn_lb, hgrn_norm, ret_norm, mlstm_conv_w, mlstm_conv_b,
           mlstm_wq, mlstm_wk, mlstm_wv, mlstm_w_gates, mlstm_b_gates, mlstm_norm, mlstm_skip, w_branch, w_out,
           w_router_group, b_router_group, w_router_expert, b_router_expert, w_exp_gate, w_exp_up, w_exp_down):
    depth, D = norm_mix.shape
    H, d, W = N_HEADS, HEAD_DIM, MIX_WIDTH
    lb = jnp.cumsum(jax.nn.softmax(hgrn_lb.astype(F32), axis=0), axis=0)
    lb = lb - lb[0:1]
    prm = jnp.stack([jnp.log(lb), jnp.log1p(-lb), 1.0 - lb, hgrn_norm.astype(F32), ret_norm.astype(F32),
                     mlstm_norm.astype(F32), mlstm_skip.astype(F32), mlstm_conv_b.astype(F32)], axis=1)
    pad = LANES - N_GROUPS - N_EXPERTS
    w_r = jnp.concatenate([w_router_group, w_router_expert, jnp.zeros((depth, D, pad), F32)], -1)
    b_r = jnp.concatenate([b_router_group, b_router_expert, jnp.zeros((depth, pad), F32)], -1)[:, None, :]
    wg_bf = mlstm_w_gates.astype(BF16)
    p = {'norm_mix': norm_mix[:, None, :], 'norm_ffn': norm_ffn[:, None, :], 'norm_final': norm_final,
         'w_in': w_in.astype(BF16), 'prm': prm, 'conv_w': mlstm_conv_w.astype(F32),
         'mq': mlstm_wq.astype(BF16), 'mk': mlstm_wk.astype(BF16), 'mv': mlstm_wv.astype(BF16),
         'm_wg': wg_bf, 'm_wgt': jnp.swapaxes(wg_bf, 1, 2),
         'm_bg': mlstm_b_gates[:, None, :], 'm_bgt': mlstm_b_gates[:, :, None],
         'w_branch': w_branch.astype(BF16), 'w_out': w_out.astype(BF16),
         'w_r': w_r, 'b_r': b_r, 'w_eg': w_exp_gate, 'w_eu': w_exp_up, 'w_ed': w_exp_down}

    Bp, Tp = x_prompt.shape[0], x_prompt.shape[1]
    Bs = x_sample.shape[0]
    zero_states = (jnp.zeros((depth, Bp, H, d, d), F32), jnp.zeros((depth, Bp, H, d, d), F32),
                   jnp.zeros((depth, Bp, H, d, d), F32), jnp.zeros((depth, Bp, H, d), F32),
                   jnp.zeros((depth, Bp, 1, H), F32), jnp.zeros((depth, Bp, CONV_W - 1, W), F32))
    pos_prompt = jnp.arange(Tp, dtype=jnp.int32)
    pos_sample = PAST_LEN + jnp.arange(x_sample.shape[1], dtype=jnp.int32)
    sample_states = (state_hgrn, state_ret, state_mlstm_C, state_mlstm_n,
                     state_mlstm_m.reshape(depth, Bs, 1, H), state_mlstm_conv)
    (y_prompt, ps), (y_sample, ss) = _trunks([x_prompt, x_sample], [pos_prompt, pos_sample],
                                             [zero_states, sample_states], p)
    return (y_prompt, y_sample) + ps + ss
```

```python
import functools

import jax
import jax.numpy as jnp
import numpy as np
from jax import lax
from jax.experimental import pallas as pl
from jax.experimental.pallas import tpu as pltpu

F32 = jnp.float32
BF16 = jnp.bfloat16
HIGHEST = lax.Precision.HIGHEST

HEAD_DIM = 128
N_HEADS = 4
MIX_WIDTH = HEAD_DIM * N_HEADS
N_BRANCH = 3
CONV_W = 4
ROPE_BASE = 10000.0
N_GROUPS = 4
EXPERTS_PER_GROUP = 8
N_EXPERTS = N_GROUPS * EXPERTS_PER_GROUP
TOP_K = 2
PAST_LEN = 16384
NORM_EPS = 1e-6
HEAD_NORM_EPS = 1e-5
SCAN_COLS = 10 * MIX_WIDTH

LANES = 128
SUBLANES = 8
VMEM_LIMIT = 48 * 1024 * 1024
MOE_ROWS = 256
ISSUE_UNROLL = 8
NEG_INF = float("-inf")


def _cparams(sem):
    return pltpu.CompilerParams(dimension_semantics=sem, vmem_limit_bytes=VMEM_LIMIT)


def _pick(n, cands):
    for c in cands:
        if n % c == 0:
            return c
    return n


def _rms(x, w):
    return x * lax.rsqrt(jnp.mean(x * x, -1, keepdims=True) + NORM_EPS) * w


def _sigmoid(x):
    return 1.0 / (1.0 + jnp.exp(-x))


def _silu(x):
    return x * _sigmoid(x)


def _log_sigmoid(x):
    return jnp.minimum(x, 0.0) - jnp.log(1.0 + jnp.exp(-jnp.abs(x)))


def _dot(a, b):
    return jnp.dot(a.astype(BF16), b.astype(BF16), preferred_element_type=F32)


def _dot_nt(a, b):
    return lax.dot_general(a.astype(BF16), b.astype(BF16), (((1,), (1,)), ((), ())), preferred_element_type=F32)


def _dot_tn(a, b):
    return jnp.dot(a.T.astype(BF16), b.astype(BF16), preferred_element_type=F32)


def _split3(x):
    hi = x.astype(BF16)
    r = x - hi.astype(F32)
    mid = r.astype(BF16)
    lo = (r - mid.astype(F32)).astype(BF16)
    return hi, mid, lo


def _mask_dot(mask_bf, x):
    return sum(jnp.dot(mask_bf, part, preferred_element_type=F32) for part in _split3(x))


def _dot_mask(x, mask_bf):
    return sum(jnp.dot(part, mask_bf, preferred_element_type=F32) for part in _split3(x))


def _head_rmsnorm(o, w):
    return o * lax.rsqrt(jnp.mean(o * o, -1, keepdims=True) + HEAD_NORM_EPS) * w


def _head_layernorm(o, w):
    c = o - jnp.mean(o, -1, keepdims=True)
    return c * lax.rsqrt(jnp.mean(c * c, -1, keepdims=True) + HEAD_NORM_EPS) * w


def _level_masks(c):
    t = np.arange(c)[:, None]
    s = np.arange(c)[None, :]
    masks = [t == s]
    blk = 2
    while blk <= c:
        masks.append((t // blk == s // blk) & (t % blk >= blk // 2) & (s % blk < blk // 2))
        blk *= 2
    return np.stack(masks).astype(np.float32)


def _retention_tables(c):
    idx = np.arange(c, dtype=np.float64)
    lg = np.log(1.0 - 2.0 ** (-5.0 - np.arange(N_HEADS, dtype=np.float64)))[:, None, None]
    rel = idx[:, None] - idx[None, :]
    dm = np.where(rel >= 0, np.exp(np.maximum(rel, 0.0)[None] * lg), 0.0)
    q_in = np.broadcast_to(np.exp((idx + 1.0)[None, :, None] * lg), (N_HEADS, c, HEAD_DIM))
    k_out = np.broadcast_to(np.exp((c - 1.0 - idx)[None, :, None] * lg), (N_HEADS, c, HEAD_DIM))
    g_chunk = tuple(float(g) for g in np.exp(c * lg[:, 0, 0]))
    return dm.astype(np.float32), q_in.astype(np.float32), k_out.astype(np.float32), g_chunk


def _level_ref(G, blk, row):
    c, d = G.shape
    if blk == 2:
        return jnp.where((row & 1) == 1, pltpu.roll(G, 1, 0), G)
    if blk == 4:
        r = row & 3
        return jnp.where(r == 0, pltpu.roll(G, c - 1, 0),
                         jnp.where(r == 1, G, jnp.where(r == 2, pltpu.roll(G, 1, 0), pltpu.roll(G, 2, 0))))
    mid = blk // 2 - 1
    G3 = G.reshape(c // blk, blk, d)
    return jnp.broadcast_to(G3[:, mid:mid + 1, :], (c // blk, blk, d)).reshape(c, d)


def _seq_tables(seq, n_seq):
    rows = seq * n_seq
    t = np.arange(rows)[:, None]
    s = np.arange(rows)[None, :]
    same = (t // seq) == (s // seq)
    tri = np.stack([same & (t >= s), same & (t <= s)]).astype(np.float32)
    levels = _level_masks(seq)
    lvl = np.zeros((levels.shape[0], rows, rows), np.float32)
    dm1, q_in1, k_out1, g_chunk = _retention_tables(seq)
    dm = np.zeros((N_HEADS, rows, rows), np.float32)
    for b in range(n_seq):
        sl = slice(b * seq, (b + 1) * seq)
        lvl[:, sl, sl] = levels
        dm[:, sl, sl] = dm1
    q_in = np.tile(q_in1, (1, n_seq, 1))
    k_out = np.tile(k_out1, (1, n_seq, 1))
    return [lvl, tri, dm, q_in, k_out], g_chunk


def _mixer_kernel(*refs, tt, seq, n_seq, n_alias, g_chunk):
    (x_ref, nw_ref, w_ref, cos_ref, sin_ref, lvl_ref, tri_ref, dm_ref, qin_ref, kout_ref,
     sh0_ref, sr0_ref, c0_ref, n0_ref, m0_ref, cb0_ref,
     prm_ref, convw_ref, mq_ref, mk_ref, mv_ref, wg_ref, wgt_ref, bg_ref, bgt_ref) = refs[:25]
    o_ref, sh_ref, sr_ref, c_ref, n_ref, m_ref, cb_ref = refs[25 + n_alias:32 + n_alias]
    z_refs = refs[32 + n_alias:-1]
    ext_ref = refs[-1]
    ti = pl.program_id(1)
    d, W, H = HEAD_DIM, MIX_WIDTH, N_HEADS
    bb = sh_ref.shape[0]
    n_lvl = lvl_ref.shape[0]
    R = seq * n_seq

    @pl.when(ti == 0)
    def _():
        sh_ref[...] = sh0_ref[...]
        sr_ref[...] = sr0_ref[...]
        c_ref[...] = c0_ref[...]
        n_ref[...] = n0_ref[...]
        m_ref[...] = m0_ref[...]
        cb_ref[...] = cb0_ref[...]

    hn = _rms(x_ref[...], nw_ref[...]).astype(BF16)
    for j, zj_ref in enumerate(z_refs):
        zj_ref[...] = jnp.dot(hn, w_ref[:, j * W:(j + 1) * W], preferred_element_type=F32)

    def prm(i, h):
        return prm_ref[i:i + 1, h * d:(h + 1) * d]

    tri_l = tri_ref[0].astype(BF16)
    tri_u = tri_ref[1].astype(BF16)
    causal = tri_ref[0] > 0.0
    row = lax.broadcasted_iota(jnp.int32, (R, d), 0)

    def sl(b):
        return slice(b * seq, (b + 1) * seq)

    def per_seq(fn):
        parts = [fn(b) for b in range(n_seq)]
        return parts[0] if n_seq == 1 else jnp.concatenate(parts, axis=0)

    def last_rows(a):
        n = a.shape[1]
        a3 = a.reshape(n_seq, seq, n)
        return jnp.broadcast_to(a3[:, seq - 1:seq, :], (n_seq, seq, n)).reshape(R, n)

    def chunk(c0):
        b0 = 0
        t0 = c0 if n_seq == 1 else 0
        rows = slice(c0, c0 + R)
        cos = cos_ref[rows, :]
        sin = sin_ref[rows, :]

        def rotary(a_):
            return a_ * cos + pltpu.roll(a_, d // 2, 1) * sin

        def zc(j, h):
            return z_refs[j][rows, h * d:(h + 1) * d]

        def put(j, h, val):
            cols = slice(j * W + h * d, j * W + (h + 1) * d)
            if n_seq == 1:
                o_ref[0, t0:t0 + seq, cols] = val
            else:
                for b in range(n_seq):
                    o_ref[b, :, cols] = val[sl(b)]

        fpre = z_refs[1][rows, :]
        e = jnp.exp(-jnp.abs(fpre))
        a = prm_ref[0:1, :]
        b_ = prm_ref[1:2, :] + (jnp.minimum(fpre, 0.0) - jnp.log(1.0 + e))
        logf = jnp.maximum(a, b_) + jnp.log(1.0 + jnp.exp(-jnp.abs(a - b_)))
        k_all = prm_ref[2:3, :] * (jnp.where(fpre >= 0.0, e, 1.0) / (1.0 + e))
        G_all = _mask_dot(tri_l, logf)
        for h in range(H):
            hs = slice(h * d, (h + 1) * d)
            q, v, gate = zc(0, h), zc(2, h), zc(3, h)
            k, G = k_all[:, hs], G_all[:, hs]
            q_bf, k_bf = q.astype(BF16), k.astype(BF16)
            A = lvl_ref[0] * _dot_nt(q_bf, k_bf)
            for j in range(1, n_lvl):
                E = jnp.exp(-jnp.abs(G - _level_ref(G, 1 << j, row))).astype(BF16)
                A = A + lvl_ref[j] * _dot_nt(q_bf * E, k_bf * E)
            qg = q * jnp.exp(G)
            G_last = last_rows(G)
            kd = k * jnp.exp(G_last - G)
            o_h = _dot(A, v) + per_seq(lambda b: _dot(qg[sl(b)], sh_ref[b0 + b, h]))
            for b in range(n_seq):
                decay = jnp.exp(G_last[b * seq:b * seq + SUBLANES]).T[:, 0:1]
                sh_ref[b0 + b, h] = decay * sh_ref[b0 + b, h] + _dot_tn(kd[sl(b)], v[sl(b)])
            put(0, h, _head_rmsnorm(o_h, prm(3, h)) * _silu(gate))

        for h in range(H):
            q = rotary(zc(4, h))
            k = rotary(zc(5, h)) * (d ** -0.5)
            v, gate = zc(6, h), zc(7, h)
            A = _dot_nt(q, k) * dm_ref[h]
            o_h = _dot(A, v) + qin_ref[h] * per_seq(lambda b: _dot(q[sl(b)], sr_ref[b0 + b, h]))
            ko = k * kout_ref[h]
            for b in range(n_seq):
                sr_ref[b0 + b, h] = g_chunk[h] * sr_ref[b0 + b, h] + _dot_tn(ko[sl(b)], v[sl(b)])
            put(1, h, _head_layernorm(o_h, prm(4, h)) * _silu(gate))

        u = z_refs[8][rows, :]
        ext_rows = SUBLANES + seq
        for b in range(n_seq):
            base = b * ext_rows
            ext_ref[base + SUBLANES - (CONV_W - 1):base + SUBLANES, :] = cb_ref[b0 + b]
            ext_ref[base + SUBLANES:base + ext_rows, :] = u[sl(b)]
        conv = jnp.zeros((R, W), F32)
        for j in range(CONV_W):
            off = SUBLANES - (CONV_W - 1) + j
            conv = conv + convw_ref[j:j + 1, :] * per_seq(
                lambda b: ext_ref[b * ext_rows + off:b * ext_rows + off + seq, :])
        for b in range(n_seq):
            cb_ref[b0 + b] = ext_ref[(b + 1) * ext_rows - (CONV_W - 1):(b + 1) * ext_rows, :]
        uc = _silu(conv + prm_ref[7:8, :])

        qs, ks, vs = [], [], []
        for h in range(H):
            uch = uc[:, h * d:(h + 1) * d].astype(BF16)
            qs.append(jnp.dot(uch, mq_ref[h], preferred_element_type=F32))
            ks.append(jnp.dot(uch, mk_ref[h], preferred_element_type=F32))
            vs.append(jnp.dot(u[:, h * d:(h + 1) * d].astype(BF16), mv_ref[h], preferred_element_type=F32))
        qkv = jnp.concatenate(qs + ks + vs, axis=1).astype(BF16)
        g_col = jnp.dot(qkv, wg_ref[...], preferred_element_type=F32) + bg_ref[...]
        g_row = lax.dot_general(wgt_ref[...], qkv, (((1,), (1,)), ((), ())),
                                preferred_element_type=F32) + bgt_ref[...]
        i_cols = g_col[:, :H]
        f_cols = _mask_dot(tri_l, _log_sigmoid(g_col[:, H:]))
        i_rows = g_row[:H]
        f_rows = _dot_mask(_log_sigmoid(g_row[H:]), tri_u)
        m_prev_rows = per_seq(lambda b: jnp.broadcast_to(m_ref[b0 + b], (seq, H)))
        F_last = last_rows(f_cols)
        for h in range(H):
            q, k, v = qs[h], ks[h] * (d ** -0.5), vs[h]
            i_col, F_col = i_cols[:, h:h + 1], f_cols[:, h:h + 1]
            a_row = i_rows[h:h + 1] - f_rows[h:h + 1]
            m_prev = m_prev_rows[:, h:h + 1]
            cm = jnp.max(jnp.where(causal, a_row, NEG_INF), -1, keepdims=True)
            m_col = F_col + jnp.maximum(m_prev, cm)
            logd = (F_col - m_col) + a_row
            dmat = jnp.where(causal, jnp.exp(jnp.where(causal, logd, 0.0)), 0.0)
            Sc = _dot_nt(q, k) * dmat
            inter = jnp.exp(F_col + m_prev - m_col)
            n_rows = jnp.broadcast_to(n_ref[b0:b0 + n_seq, h:h + 1, :], (n_seq, seq, d)).reshape(R, d)
            num = _dot(Sc, v) + inter * per_seq(lambda b: _dot(q[sl(b)], c_ref[b0 + b, h]))
            den = jnp.sum(Sc, -1, keepdims=True) + inter * jnp.sum(q * n_rows, -1, keepdims=True)
            hout = num / jnp.maximum(jnp.abs(den), jnp.exp(-m_col))
            mL = last_rows(jnp.broadcast_to(m_col, (R, d)))
            FL = jnp.broadcast_to(F_last[:, h:h + 1], (R, d))
            kw = k * jnp.exp((FL - mL) + (i_col - F_col))
            decay_rows = jnp.exp((FL - mL) + m_prev)
            for b in range(n_seq):
                decay = decay_rows[b * seq:b * seq + 1]
                c_ref[b0 + b, h] = decay * c_ref[b0 + b, h] + _dot_tn(kw[sl(b)], v[sl(b)])
                n_ref[b0 + b, h:h + 1, :] = decay * n_ref[b0 + b, h:h + 1, :] + jnp.sum(kw[sl(b)], 0, keepdims=True)
                m_ref[b0 + b, :, h:h + 1] = mL[b * seq:b * seq + 1, 0:1]
            y = _head_layernorm(hout, prm(5, h)) + prm(6, h) * uc[:, h * d:(h + 1) * d]
            put(2, h, y * _silu(zc(9, h)))

    for c in range(bb * tt // R):
        chunk(c * R)


def _mixers(x, x_row0, B, T, cos, sin, states_in, l, prev_out, p):
    D = x.shape[1]
    depth = states_in[0].shape[0]
    H, d, W = N_HEADS, HEAD_DIM, MIX_WIDTH
    tt = _pick(T, (256, 128, 64, 32, 16, 8))
    n_t = T // tt
    if tt >= 128:
        bb, seq, n_seq = 1, 128, 1
    else:
        assert n_t == 1
        bb = _pick(B, tuple(c for c in (8, 4, 2) if c * tt <= 128) + (1,))
        seq, n_seq = tt, bb
        cos, sin = jnp.tile(cos, (bb, 1)), jnp.tile(sin, (bb, 1))
    assert x_row0 % (bb * tt) == 0
    xb = x_row0 // (bb * tt)
    R = seq * n_seq
    tables, g_chunk = _seq_tables(seq, n_seq)
    consts = [jnp.asarray(t) for t in tables]
    st_spec = pl.BlockSpec((None, bb, H, d, d), lambda b, t: (l, b, 0, 0, 0))
    n_spec = pl.BlockSpec((None, bb, H, d), lambda b, t: (l, b, 0, 0))
    m_spec = pl.BlockSpec((None, bb, 1, H), lambda b, t: (l, b, 0, 0))
    cb_spec = pl.BlockSpec((None, bb, CONV_W - 1, W), lambda b, t: (l, b, 0, 0))
    state_specs = [st_spec, st_spec, st_spec, n_spec, m_spec, cb_spec]

    def full(a):
        nd = a.ndim
        return pl.BlockSpec(a.shape, lambda b, t: (0,) * nd)

    def layer(a):
        nd = a.ndim - 1
        return pl.BlockSpec((None,) + a.shape[1:], lambda b, t: (l,) + (0,) * nd)

    weights = [p['prm'], p['conv_w'], p['mq'], p['mk'], p['mv'], p['m_wg'], p['m_wgt'], p['m_bg'], p['m_bgt']]
    alias_in = list(prev_out) if prev_out is not None else []
    n_fixed = 5 + len(consts) + 6 + len(weights)
    state_shapes = [jax.ShapeDtypeStruct((depth, B, H, d, d), F32)] * 3 + [
        jax.ShapeDtypeStruct((depth, B, H, d), F32),
        jax.ShapeDtypeStruct((depth, B, 1, H), F32),
        jax.ShapeDtypeStruct((depth, B, CONV_W - 1, W), F32)]
    rows_spec = pl.BlockSpec((bb * tt, d), lambda b, t: (t, 0))
    outs = pl.pallas_call(
        functools.partial(_mixer_kernel, tt=tt, seq=seq, n_seq=n_seq, n_alias=len(alias_in),
                          g_chunk=g_chunk),
        grid=(B // bb, n_t),
        in_specs=[pl.BlockSpec((bb * tt, D), lambda b, t: (xb + b * n_t + t, 0)),
                  pl.BlockSpec((None, 1, D), lambda b, t: (l, 0, 0)),
                  pl.BlockSpec((None, D, SCAN_COLS), lambda b, t: (l, 0, 0), pipeline_mode=pl.Buffered(1)),
                  rows_spec, rows_spec]
                 + [full(c) for c in consts] + state_specs + [layer(w) for w in weights]
                 + [pl.BlockSpec(memory_space=pl.ANY)] * len(alias_in),
        out_specs=[pl.BlockSpec((bb, tt, N_BRANCH * W), lambda b, t: (b, t, 0))] + state_specs,
        out_shape=[jax.ShapeDtypeStruct((B, T, N_BRANCH * W), F32)] + state_shapes,
        input_output_aliases={n_fixed + i: 1 + i for i in range(len(alias_in))},
        scratch_shapes=[pltpu.VMEM((bb * tt, W), F32)] * (SCAN_COLS // W)
                       + [pltpu.VMEM((n_seq * (SUBLANES + seq), W), F32)],
        name="mixers",
        compiler_params=_cparams(("parallel", "arbitrary")),
    )(x, p['norm_mix'], p['w_in'], cos, sin, *consts, *states_in, *weights, *alias_in)
    return outs[0], tuple(outs[1:])


def _merge_kernel(x_ref, o_ref, nw_ref, wg0_ref, wg1_ref, wg2_ref, wbr_ref, wout_ref, *rest):
    y_ref = rest[-1]
    x = x_ref[...]
    W = MIX_WIDTH
    hn = _rms(x, nw_ref[...]).astype(BF16)
    merged = jnp.zeros(x.shape, F32)
    for n, wg_ref in enumerate((wg0_ref, wg1_ref, wg2_ref)):
        gz = jnp.dot(hn, wg_ref[...], preferred_element_type=F32)
        proj = jnp.dot(o_ref[:, n * W:(n + 1) * W].astype(BF16), wbr_ref[n], preferred_element_type=F32)
        merged = merged + _sigmoid(gz) * proj
    y_ref[...] = x + jnp.dot(merged.astype(BF16), wout_ref[...], preferred_element_type=F32)


def _merge(x, x_row0, o, nw, w_in, wbr, wout, l, out, out_row0, out_rows):
    n = o.shape[0]
    d = x.shape[1]
    tm = _pick(n, (512, 256, 128))
    assert x_row0 % tm == 0 and out_row0 % tm == 0
    xb, ob = x_row0 // tm, out_row0 // tm
    g0 = SCAN_COLS // d

    def gate_spec(k):
        return pl.BlockSpec((None, d, d), lambda i: (l, 0, g0 + k))

    alias_in = [] if out is None else [out]
    return pl.pallas_call(
        _merge_kernel,
        grid=(n // tm,),
        in_specs=[pl.BlockSpec((tm, d), lambda i: (xb + i, 0)),
                  pl.BlockSpec((tm, N_BRANCH * MIX_WIDTH), lambda i: (i, 0)),
                  pl.BlockSpec((None, 1, d), lambda i: (l, 0, 0)),
                  gate_spec(0), gate_spec(1), gate_spec(2),
                  pl.BlockSpec((None,) + wbr.shape[1:], lambda i: (l, 0, 0, 0)),
                  pl.BlockSpec((None,) + wout.shape[1:], lambda i: (l, 0, 0))]
                 + [pl.BlockSpec(memory_space=pl.ANY)] * len(alias_in),
        out_specs=pl.BlockSpec((tm, d), lambda i: (ob + i, 0)),
        out_shape=jax.ShapeDtypeStruct((out_rows, d), F32),
        input_output_aliases={8: 0} if alias_in else {},
        name="merge",
        compiler_params=_cparams(("parallel",)),
    )(x, o, nw, w_in, w_in, w_in, wbr, wout, *alias_in)


def _router_kernel(x_ref, nw_ref, wr_ref, br_ref, hn_ref, mi_ref, mf_ref, cnt_ref, carry_ref):
    i = pl.program_id(0)
    tm = x_ref.shape[0]
    n_sub = x_ref.shape[1] // LANES

    @pl.when(i == 0)
    def _():
        carry_ref[...] = jnp.zeros(carry_ref.shape, F32)

    hn = _rms(x_ref[...], nw_ref[...])
    for s in range(n_sub):
        hn_ref[pl.ds(s, tm, stride=n_sub), :] = hn[:, s * LANES:(s + 1) * LANES]
    logits = _dot(hn, wr_ref[...]) + br_ref[...]
    lane = lax.broadcasted_iota(jnp.int32, logits.shape, 1)
    big = jnp.int32(1 << 20)

    def first_max(mask):
        vmax = jnp.max(jnp.where(mask, logits, NEG_INF), -1, keepdims=True)
        imax = jnp.min(jnp.where(mask, jnp.where(logits == vmax, lane, big), big), -1, keepdims=True)
        return vmax, imax

    in_groups = lane < N_GROUPS
    g_max, g_top = first_max(in_groups)
    g_w = 1.0 / jnp.sum(jnp.where(in_groups, jnp.exp(logits - g_max), 0.0), -1, keepdims=True)
    e_lo = N_GROUPS + EXPERTS_PER_GROUP * g_top
    in_e = jnp.logical_and(lane >= e_lo, lane < e_lo + EXPERTS_PER_GROUP)
    v1, i1 = first_max(in_e)
    v2, i2 = first_max(jnp.logical_and(in_e, lane != i1))
    e2 = jnp.exp(v2 - v1)
    w1 = g_w / (1.0 + e2)
    w2 = g_w * e2 / (1.0 + e2)

    hit1 = lane == i1
    hit2 = lane == i2
    onehot = jnp.where(hit1, 1.0, 0.0) + jnp.where(hit2, 1.0, 0.0)
    strict = (lax.broadcasted_iota(jnp.int32, (tm, tm), 0) > lax.broadcasted_iota(jnp.int32, (tm, tm), 1))
    before = _dot(strict.astype(F32), onehot) + carry_ref[...]
    r1 = jnp.sum(jnp.where(hit1, before, 0.0), -1, keepdims=True).astype(jnp.int32)
    r2 = jnp.sum(jnp.where(hit2, before, 0.0), -1, keepdims=True).astype(jnp.int32)
    carry_ref[...] = carry_ref[...] + jnp.sum(onehot, 0, keepdims=True)
    cnt_ref[...] = carry_ref[...]

    meta = jnp.where(lane == 0, i1 - N_GROUPS,
                     jnp.where(lane == 1, i2 - N_GROUPS,
                               jnp.where(lane == 2, r1, jnp.where(lane == 3, r2, 0))))
    mi_ref[...] = meta.T[:SUBLANES]
    mf_ref[...] = jnp.where(lane == 0, w1, jnp.where(lane == 1, w2, 0.0))


def _router(x, nw, wr, br, l):
    n, d = x.shape
    n_sub = d // LANES
    tm = _pick(n, (512, 256, 128))
    return pl.pallas_call(
        _router_kernel,
        grid=(n // tm,),
        in_specs=[pl.BlockSpec((tm, d), lambda i: (i, 0)),
                  pl.BlockSpec((None, 1, d), lambda i: (l, 0, 0)),
                  pl.BlockSpec((None, d, LANES), lambda i: (l, 0, 0)),
                  pl.BlockSpec((None, 1, LANES), lambda i: (l, 0, 0))],
        out_specs=[pl.BlockSpec((tm * n_sub, LANES), lambda i: (i, 0)),
                   pl.BlockSpec((SUBLANES, tm), lambda i: (0, i)),
                   pl.BlockSpec((tm, LANES), lambda i: (i, 0)),
                   pl.BlockSpec((1, LANES), lambda i: (0, 0))],
        out_shape=[jax.ShapeDtypeStruct((n * n_sub, LANES), F32),
                   jax.ShapeDtypeStruct((SUBLANES, n), jnp.int32),
                   jax.ShapeDtypeStruct((n, LANES), F32),
                   jax.ShapeDtypeStruct((1, LANES), F32)],
        scratch_shapes=[pltpu.VMEM((1, LANES), F32)],
        name="router",
        compiler_params=_cparams(("arbitrary",)),
    )(x, nw, wr, br)


def _slots_kernel(d0_ref, d1_ref, src_ref, dst_ref, *, n_sub):
    n_tok = d0_ref.shape[0]

    def pad(j, carry):
        src_ref[j] = (n_tok - 1) * n_sub
        dst_ref[j] = (TOP_K * n_tok + j % (2 * MOE_ROWS)) * n_sub
        return carry

    def body(t, carry):
        for k, d_ref in enumerate((d0_ref, d1_ref)):
            slot = MOE_ROWS + d_ref[t]
            src_ref[slot] = t * n_sub
            dst_ref[slot] = (TOP_K * t + k) * n_sub
        return carry

    lax.fori_loop(0, src_ref.shape[0], pad, 0, unroll=16)
    lax.fori_loop(0, n_tok, body, 0, unroll=16)


def _slots(d0, d1, cap, n_sub):
    table = jax.ShapeDtypeStruct((cap + 2 * MOE_ROWS,), jnp.int32)
    return pl.pallas_call(
        functools.partial(_slots_kernel, n_sub=n_sub),
        grid_spec=pltpu.PrefetchScalarGridSpec(
            num_scalar_prefetch=2,
            grid=(1,),
            in_specs=[],
            out_specs=[pl.BlockSpec(memory_space=pltpu.SMEM), pl.BlockSpec(memory_space=pltpu.SMEM)]),
        out_shape=[table, table],
        name="slots",
        compiler_params=pltpu.CompilerParams(dimension_semantics=("arbitrary",)),
    )(d0, d1)


def _experts_kernel(be_ref, src_ref, dst_ref, hn_ref, wg_ref, wu_ref, wd_ref, y2_ref,
                    xbuf, ybuf, gsem, ssem, wg_bf, wu_bf, wd_bf, *, n_sub):
    i = pl.program_id(0)
    n_i = pl.num_programs(0)
    slot = i % 2
    block_rows = pl.ds(0, MOE_ROWS * n_sub)

    def rows_at(first):
        return pl.ds(pl.multiple_of(first, n_sub), n_sub)

    def start_gather(table_blk, sl):
        for r in range(MOE_ROWS):
            src = rows_at(src_ref[table_blk * MOE_ROWS + r])
            pltpu.make_async_copy(hn_ref.at[src], xbuf.at[sl, rows_at(r * n_sub)], gsem.at[sl]).start()

    def start_scatter(table_blk, sl):
        for r in range(MOE_ROWS):
            dst = rows_at(dst_ref[table_blk * MOE_ROWS + r])
            pltpu.make_async_copy(ybuf.at[sl, rows_at(r * n_sub)], y2_ref.at[dst], ssem.at[sl]).start()

    def wait_gather(sl):
        pltpu.make_async_copy(hn_ref.at[block_rows], xbuf.at[sl], gsem.at[sl]).wait()

    def wait_scatter(sl):
        pltpu.make_async_copy(ybuf.at[sl], y2_ref.at[block_rows], ssem.at[sl]).wait()

    @pl.when(i == 0)
    def _():
        ybuf[...] = jnp.zeros(ybuf.shape, F32)
        start_gather(1, 0)

    wait_gather(slot)
    changed = jnp.logical_or(i == 0, be_ref[i] != be_ref[jnp.maximum(i - 1, 0)])

    @pl.when(changed)
    def _():
        wg_bf[...] = wg_ref[...].astype(BF16)
        wu_bf[...] = wu_ref[...].astype(BF16)
        wd_bf[...] = wd_ref[...].astype(BF16)

    x = jnp.concatenate([xbuf[slot, pl.ds(s, MOE_ROWS, stride=n_sub), :] for s in range(n_sub)], axis=1)
    x = x.astype(BF16)
    start_gather(i + 2, 1 - slot)
    start_scatter(i, 1 - slot)
    g = jnp.dot(x, wg_bf[...], preferred_element_type=F32)
    u = jnp.dot(x, wu_bf[...], preferred_element_type=F32)
    y = jnp.dot((_silu(g) * u).astype(BF16), wd_bf[...], preferred_element_type=F32)

    @pl.when(i > 0)
    def _():
        wait_scatter(slot)

    for s in range(n_sub):
        ybuf[slot, pl.ds(s, MOE_ROWS, stride=n_sub), :] = y[:, s * LANES:(s + 1) * LANES]

    @pl.when(i == n_i - 1)
    def _():
        start_scatter(i + 1, slot)
        wait_scatter(1 - slot)
        wait_scatter(slot)
        wait_gather(1 - slot)


def _experts(block_e, src_rows, dst_rows, hn, w_g, w_u, w_d, l):
    _, _, d, f = w_g.shape
    n_sub = d // LANES
    n = hn.shape[0] // n_sub
    n_blocks = block_e.shape[0]
    buf = pltpu.VMEM((2, MOE_ROWS * n_sub, LANES), F32)
    return pl.pallas_call(
        functools.partial(_experts_kernel, n_sub=n_sub),
        grid_spec=pltpu.PrefetchScalarGridSpec(
            num_scalar_prefetch=3,
            grid=(n_blocks,),
            in_specs=[pl.BlockSpec(memory_space=pl.ANY),
                      pl.BlockSpec((None, None, d, f), lambda i, be, sr, ds: (l, be[i], 0, 0)),
                      pl.BlockSpec((None, None, d, f), lambda i, be, sr, ds: (l, be[i], 0, 0)),
                      pl.BlockSpec((None, None, f, d), lambda i, be, sr, ds: (l, be[i], 0, 0))],
            out_specs=pl.BlockSpec(memory_space=pl.ANY),
            scratch_shapes=[buf, buf, pltpu.SemaphoreType.DMA((2,)), pltpu.SemaphoreType.DMA((2,)),
                            pltpu.VMEM((d, f), BF16), pltpu.VMEM((d, f), BF16), pltpu.VMEM((f, d), BF16)]),
        out_shape=jax.ShapeDtypeStruct(((n * TOP_K + 2 * MOE_ROWS) * n_sub, LANES), F32),
        name="experts",
        compiler_params=pltpu.CompilerParams(dimension_semantics=("arbitrary",), vmem_limit_bytes=VMEM_LIMIT,
                                             has_side_effects=True),
    )(block_e, src_rows, dst_rows, hn, w_g, w_u, w_d)


def _combine_kernel(x_ref, mf_ref, nw_ref, y2_ref, y_ref, *, final_norm):
    tm = x_ref.shape[0]
    n_sub = x_ref.shape[1] // LANES
    w = mf_ref[...]
    stride = TOP_K * n_sub
    y0 = jnp.concatenate([y2_ref[pl.ds(s, tm, stride=stride), :] for s in range(n_sub)], axis=1)
    y1 = jnp.concatenate([y2_ref[pl.ds(n_sub + s, tm, stride=stride), :] for s in range(n_sub)], axis=1)
    out = x_ref[...] + (y0 * w[:, 0:1] + y1 * w[:, 1:2])
    if final_norm:
        out = _rms(out, nw_ref[...])
    y_ref[...] = out


def _combine(x, mf, nw, y2, final_norm, row0, n):
    d = x.shape[1]
    n_sub = d // LANES
    tm = _pick(n, (256, 128))
    assert row0 % tm == 0
    rb = row0 // tm
    return pl.pallas_call(
        functools.partial(_combine_kernel, final_norm=final_norm),
        grid=(n // tm,),
        in_specs=[pl.BlockSpec((tm, d), lambda i: (rb + i, 0)),
                  pl.BlockSpec((tm, LANES), lambda i: (rb + i, 0)),
                  pl.BlockSpec((1, d), lambda i: (0, 0)),
                  pl.BlockSpec((tm * TOP_K * n_sub, LANES), lambda i: (rb + i, 0))],
        out_specs=pl.BlockSpec((tm, d), lambda i: (i, 0)),
        out_shape=jax.ShapeDtypeStruct((n, d), F32),
        name="combine",
        compiler_params=_cparams(("parallel",)),
    )(x, mf, nw, y2)


def _moe(x, l, p, final_nw, out_ranges):
    n, d = x.shape
    n_sub = d // LANES
    hn, mi, mf, cnt = _router(x, p['norm_ffn'], p['w_r'], p['b_r'], l)
    counts = cnt[0, N_GROUPS:N_GROUPS + N_EXPERTS].astype(jnp.int32)
    padded = (counts + MOE_ROWS - 1) // MOE_ROWS * MOE_ROWS
    pad_end = jnp.cumsum(padded)
    pad_start = pad_end - padded
    experts = jnp.arange(N_EXPERTS, dtype=jnp.int32)

    def lookup(table, e):
        return jnp.sum(jnp.where(e[:, None] == experts[None, :], table[None, :], 0), axis=1)

    d0 = mi[2] + lookup(pad_start, mi[0])
    d1 = mi[3] + lookup(pad_start, mi[1])
    n_blocks = -(-(n * TOP_K) // MOE_ROWS) + N_EXPERTS
    block_row = jnp.arange(n_blocks, dtype=jnp.int32) * MOE_ROWS
    block_e = jnp.minimum(jnp.sum((pad_end[None, :] <= block_row[:, None]).astype(jnp.int32), axis=1), N_EXPERTS - 1)
    src_rows, dst_rows = _slots(d0, d1, n_blocks * MOE_ROWS, n_sub)
    y2 = _experts(block_e.astype(jnp.int32), src_rows, dst_rows, hn, p['w_eg'], p['w_eu'], p['w_ed'], l)
    nw = (final_nw if final_nw is not None else p['norm_ffn'][l, 0])[None]
    return [_combine(x, mf, nw, y2, final_nw is not None, row0, rows) for row0, rows in out_ranges]


def _rotary_tables(pos):
    half = HEAD_DIM // 2
    inv = ROPE_BASE ** (-jnp.arange(half, dtype=F32) / half)
    ang = pos.astype(F32)[:, None] * inv[None, :]
    cos = jnp.concatenate([jnp.cos(ang), jnp.cos(ang)], -1)
    sin = jnp.concatenate([-jnp.sin(ang), jnp.sin(ang)], -1)
    return cos, sin


def _trunks(xs, positions, states, p):
    depth = p['w_in'].shape[0]
    D = xs[0].shape[-1]
    shapes = [x.shape[:2] for x in xs]
    n_rows = [B * T for B, T in shapes]
    row0 = [sum(n_rows[:i]) for i in range(len(xs))]
    total = sum(n_rows)
    tables = [_rotary_tables(pos) for pos in positions]
    x_arrays = [x.reshape(n, D) for x, n in zip(xs, n_rows)]
    x_row0 = [0] * len(xs)
    new_states = [None] * len(xs)
    for l in range(depth):
        xm = None
        for i, (B, T) in enumerate(shapes):
            o, new_states[i] = _mixers(x_arrays[i], x_row0[i], B, T, *tables[i], states[i], l, new_states[i], p)
            xm = _merge(x_arrays[i], x_row0[i], o.reshape(n_rows[i], N_BRANCH * MIX_WIDTH), p['norm_mix'], p['w_in'],
                        p['w_branch'], p['w_out'], l, xm, row0[i], total)
        if l == depth - 1:
            outs = _moe(xm, l, p, p['norm_final'], list(zip(row0, n_rows)))
        else:
            x_all, = _moe(xm, l, p, None, [(0, total)])
            x_arrays, x_row0 = [x_all] * len(xs), row0
    results = []
    for y, (B, T), st in zip(outs, shapes, new_states):
        sh, sr, c, n, m, cb = st
        results.append((y.reshape(B, T, D), (sh, sr, c, n, m.reshape(depth, B, N_HEADS), cb)))
    return results


def kernel(x_prompt, x_sample, state_hgrn, state_ret, state_mlstm_C, state_mlstm_n, state_mlstm_m, state_mlstm_conv,
           norm_mix, norm_ffn, norm_final, w_in, hgrn_lb, hgrn_norm, ret_norm, mlstm_conv_w, mlstm_conv_b,
           mlstm_wq, mlstm_wk, mlstm_wv, mlstm_w_gates, mlstm_b_gates, mlstm_norm, mlstm_skip, w_branch, w_out,
           w_router_group, b_router_group, w_router_expert, b_router_expert, w_exp_gate, w_exp_up, w_exp_down):
    depth, D = norm_mix.shape
    H, d, W = N_HEADS, HEAD_DIM, MIX_WIDTH
    lb = jnp.cumsum(jax.nn.softmax(hgrn_lb.astype(F32), axis=0), axis=0)
    lb = lb - lb[0:1]
    prm = jnp.stack([jnp.log(lb), jnp.log1p(-lb), 1.0 - lb, hgrn_norm.astype(F32), ret_norm.astype(F32),
                     mlstm_norm.astype(F32), mlstm_skip.astype(F32), mlstm_conv_b.astype(F32)], axis=1)
    pad = LANES - N_GROUPS - N_EXPERTS
    w_r = jnp.concatenate([w_router_group, w_router_expert, jnp.zeros((depth, D, pad), F32)], -1)
    b_r = jnp.concatenate([b_router_group, b_router_expert, jnp.zeros((depth, pad), F32)], -1)[:, None, :]
    wg_bf = mlstm_w_gates.astype(BF16)
    p = {'norm_mix': norm_mix[:, None, :], 'norm_ffn': norm_ffn[:, None, :], 'norm_final': norm_final,
         'w_in': w_in.astype(BF16), 'prm': prm, 'conv_w': mlstm_conv_w.astype(F32),
         'mq': mlstm_wq.astype(BF16), 'mk': mlstm_wk.astype(BF16), 'mv': mlstm_wv.astype(BF16),
         'm_wg': wg_bf, 'm_wgt': jnp.swapaxes(wg_bf, 1, 2),
         'm_bg': mlstm_b_gates[:, None, :], 'm_bgt': mlstm_b_gates[:, :, None],
         'w_branch': w_branch.astype(BF16), 'w_out': w_out.astype(BF16),
         'w_r': w_r, 'b_r': b_r, 'w_eg': w_exp_gate, 'w_eu': w_exp_up, 'w_ed': w_exp_down}

    Bp, Tp = x_prompt.shape[0], x_prompt.shape[1]
    Bs = x_sample.shape[0]
    zero_states = (jnp.zeros((depth, Bp, H, d, d), F32), jnp.zeros((depth, Bp, H, d, d), F32),
                   jnp.zeros((depth, Bp, H, d, d), F32), jnp.zeros((depth, Bp, H, d), F32),
                   jnp.zeros((depth, Bp, 1, H), F32), jnp.zeros((depth, Bp, CONV_W - 1, W), F32))
    pos_prompt = jnp.arange(Tp, dtype=jnp.int32)
    pos_sample = PAST_LEN + jnp.arange(x_sample.shape[1], dtype=jnp.int32)
    sample_states = (state_hgrn, state_ret, state_mlstm_C, state_mlstm_n,
                     state_mlstm_m.reshape(depth, Bs, 1, H), state_mlstm_conv)
    (y_prompt, ps), (y_sample, ss) = _trunks([x_prompt, x_sample], [pos_prompt, pos_sample],
                                             [zero_states, sample_states], p)
    return (y_prompt, y_sample) + ps + ss
```

```python
import functools

import jax
import jax.numpy as jnp
import numpy as np
from jax import lax
from jax.experimental import pallas as pl
from jax.experimental.pallas import tpu as pltpu

F32 = jnp.float32
BF16 = jnp.bfloat16
HIGHEST = lax.Precision.HIGHEST

HEAD_DIM = 128
N_HEADS = 4
MIX_WIDTH = HEAD_DIM * N_HEADS
N_BRANCH = 3
CONV_W = 4
ROPE_BASE = 10000.0
N_GROUPS = 4
EXPERTS_PER_GROUP = 8
N_EXPERTS = N_GROUPS * EXPERTS_PER_GROUP
TOP_K = 2
PAST_LEN = 16384
NORM_EPS = 1e-6
HEAD_NORM_EPS = 1e-5
SCAN_COLS = 10 * MIX_WIDTH

LANES = 128
SUBLANES = 8
VMEM_LIMIT = 48 * 1024 * 1024
MOE_ROWS = 256
ISSUE_UNROLL = 8
NEG_INF = float("-inf")


def _cparams(sem):
    return pltpu.CompilerParams(dimension_semantics=sem, vmem_limit_bytes=VMEM_LIMIT)


def _pick(n, cands):
    for c in cands:
        if n % c == 0:
            return c
    return n


def _rms(x, w):
    return x * lax.rsqrt(jnp.mean(x * x, -1, keepdims=True) + NORM_EPS) * w


def _sigmoid(x):
    return 1.0 / (1.0 + jnp.exp(-x))


def _silu(x):
    return x * _sigmoid(x)


def _log_sigmoid(x):
    return jnp.minimum(x, 0.0) - jnp.log(1.0 + jnp.exp(-jnp.abs(x)))


def _dot(a, b):
    return jnp.dot(a.astype(BF16), b.astype(BF16), preferred_element_type=F32)


def _dot_nt(a, b):
    return lax.dot_general(a.astype(BF16), b.astype(BF16), (((1,), (1,)), ((), ())), preferred_element_type=F32)


def _dot_tn(a, b):
    return jnp.dot(a.T.astype(BF16), b.astype(BF16), preferred_element_type=F32)


def _split3(x):
    hi = x.astype(BF16)
    r = x - hi.astype(F32)
    mid = r.astype(BF16)
    lo = (r - mid.astype(F32)).astype(BF16)
    return hi, mid, lo


def _mask_dot(mask_bf, x):
    return sum(jnp.dot(mask_bf, part, preferred_element_type=F32) for part in _split3(x))


def _dot_mask(x, mask_bf):
    return sum(jnp.dot(part, mask_bf, preferred_element_type=F32) for part in _split3(x))


def _head_rmsnorm(o, w):
    return o * lax.rsqrt(jnp.mean(o * o, -1, keepdims=True) + HEAD_NORM_EPS) * w


def _head_layernorm(o, w):
    c = o - jnp.mean(o, -1, keepdims=True)
    return c * lax.rsqrt(jnp.mean(c * c, -1, keepdims=True) + HEAD_NORM_EPS) * w


def _level_masks(c):
    t = np.arange(c)[:, None]
    s = np.arange(c)[None, :]
    masks = [t == s]
    blk = 2
    while blk <= c:
        masks.append((t // blk == s // blk) & (t % blk >= blk // 2) & (s % blk < blk // 2))
        blk *= 2
    return np.stack(masks).astype(np.float32)


def _retention_tables(c):
    idx = np.arange(c, dtype=np.float64)
    lg = np.log(1.0 - 2.0 ** (-5.0 - np.arange(N_HEADS, dtype=np.float64)))[:, None, None]
    rel = idx[:, None] - idx[None, :]
    dm = np.where(rel >= 0, np.exp(np.maximum(rel, 0.0)[None] * lg), 0.0)
    q_in = np.broadcast_to(np.exp((idx + 1.0)[None, :, None] * lg), (N_HEADS, c, HEAD_DIM))
    k_out = np.broadcast_to(np.exp((c - 1.0 - idx)[None, :, None] * lg), (N_HEADS, c, HEAD_DIM))
    g_chunk = tuple(float(g) for g in np.exp(c * lg[:, 0, 0]))
    return dm.astype(np.float32), q_in.astype(np.float32), k_out.astype(np.float32), g_chunk


def _level_ref(G, blk, row):
    c, d = G.shape
    if blk == 2:
        return jnp.where((row & 1) == 1, pltpu.roll(G, 1, 0), G)
    if blk == 4:
        r = row & 3
        return jnp.where(r == 0, pltpu.roll(G, c - 1, 0),
                         jnp.where(r == 1, G, jnp.where(r == 2, pltpu.roll(G, 1, 0), pltpu.roll(G, 2, 0))))
    mid = blk // 2 - 1
    G3 = G.reshape(c // blk, blk, d)
    return jnp.broadcast_to(G3[:, mid:mid + 1, :], (c // blk, blk, d)).reshape(c, d)


def _seq_tables(seq, n_seq):
    rows = seq * n_seq
    t = np.arange(rows)[:, None]
    s = np.arange(rows)[None, :]
    same = (t // seq) == (s // seq)
    tri = np.stack([same & (t >= s), same & (t <= s)]).astype(np.float32)
    levels = _level_masks(seq)
    lvl = np.zeros((levels.shape[0], rows, rows), np.float32)
    dm1, q_in1, k_out1, g_chunk = _retention_tables(seq)
    dm = np.zeros((N_HEADS, rows, rows), np.float32)
    for b in range(n_seq):
        sl = slice(b * seq, (b + 1) * seq)
        lvl[:, sl, sl] = levels
        dm[:, sl, sl] = dm1
    q_in = np.tile(q_in1, (1, n_seq, 1))
    k_out = np.tile(k_out1, (1, n_seq, 1))
    return [lvl, tri, dm, q_in, k_out], g_chunk


def _mixer_kernel(*refs, tt, seq, n_seq, n_alias, g_chunk):
    (x_ref, nw_ref, w_ref, cos_ref, sin_ref, lvl_ref, tri_ref, dm_ref, qin_ref, kout_ref,
     sh0_ref, sr0_ref, c0_ref, n0_ref, m0_ref, cb0_ref,
     prm_ref, convw_ref, mq_ref, mk_ref, mv_ref, wg_ref, wgt_ref, bg_ref, bgt_ref) = refs[:25]
    o_ref, sh_ref, sr_ref, c_ref, n_ref, m_ref, cb_ref = refs[25 + n_alias:32 + n_alias]
    z_refs = refs[32 + n_alias:-1]
    ext_ref = refs[-1]
    ti = pl.program_id(1)
    d, W, H = HEAD_DIM, MIX_WIDTH, N_HEADS
    bb = sh_ref.shape[0]
    n_lvl = lvl_ref.shape[0]
    R = seq * n_seq

    @pl.when(ti == 0)
    def _():
        sh_ref[...] = sh0_ref[...]
        sr_ref[...] = sr0_ref[...]
        c_ref[...] = c0_ref[...]
        n_ref[...] = n0_ref[...]
        m_ref[...] = m0_ref[...]
        cb_ref[...] = cb0_ref[...]

    hn = _rms(x_ref[...], nw_ref[...]).astype(BF16)
    for j, zj_ref in enumerate(z_refs):
        zj_ref[...] = jnp.dot(hn, w_ref[:, j * W:(j + 1) * W], preferred_element_type=F32)

    def prm(i, h):
        return prm_ref[i:i + 1, h * d:(h + 1) * d]

    tri_l = tri_ref[0].astype(BF16)
    tri_u = tri_ref[1].astype(BF16)
    causal = tri_ref[0] > 0.0
    row = lax.broadcasted_iota(jnp.int32, (R, d), 0)

    def sl(b):
        return slice(b * seq, (b + 1) * seq)

    def per_seq(fn):
        parts = [fn(b) for b in range(n_seq)]
        return parts[0] if n_seq == 1 else jnp.concatenate(parts, axis=0)

    def last_rows(a):
        n = a.shape[1]
        a3 = a.reshape(n_seq, seq, n)
        return jnp.broadcast_to(a3[:, seq - 1:seq, :], (n_seq, seq, n)).reshape(R, n)

    def chunk(c0):
        b0 = 0
        t0 = c0 if n_seq == 1 else 0
        rows = slice(c0, c0 + R)
        cos = cos_ref[rows, :]
        sin = sin_ref[rows, :]

        def rotary(a_):
            return a_ * cos + pltpu.roll(a_, d // 2, 1) * sin

        def zc(j, h):
            return z_refs[j][rows, h * d:(h + 1) * d]

        def put(j, h, val):
            cols = slice(j * W + h * d, j * W + (h + 1) * d)
            if n_seq == 1:
                o_ref[0, t0:t0 + seq, cols] = val
            else:
                for b in range(n_seq):
                    o_ref[b, :, cols] = val[sl(b)]

        fpre = z_refs[1][rows, :]
        e = jnp.exp(-jnp.abs(fpre))
        a = prm_ref[0:1, :]
        b_ = prm_ref[1:2, :] + (jnp.minimum(fpre, 0.0) - jnp.log(1.0 + e))
        logf = jnp.maximum(a, b_) + jnp.log(1.0 + jnp.exp(-jnp.abs(a - b_)))
        k_all = prm_ref[2:3, :] * (jnp.where(fpre >= 0.0, e, 1.0) / (1.0 + e))
        G_all = _mask_dot(tri_l, logf)
        for h in range(H):
            hs = slice(h * d, (h + 1) * d)
            q, v, gate = zc(0, h), zc(2, h), zc(3, h)
            k, G = k_all[:, hs], G_all[:, hs]
            q_bf, k_bf = q.astype(BF16), k.astype(BF16)
            A = lvl_ref[0] * _dot_nt(q_bf, k_bf)
            for j in range(1, n_lvl):
                E = jnp.exp(-jnp.abs(G - _level_ref(G, 1 << j, row))).astype(BF16)
                A = A + lvl_ref[j] * _dot_nt(q_bf * E, k_bf * E)
            qg = q * jnp.exp(G)
            G_last = last_rows(G)
            kd = k * jnp.exp(G_last - G)
            o_h = _dot(A, v) + per_seq(lambda b: _dot(qg[sl(b)], sh_ref[b0 + b, h]))
            for b in range(n_seq):
                decay = jnp.exp(G_last[b * seq:b * seq + SUBLANES]).T[:, 0:1]
                sh_ref[b0 + b, h] = decay * sh_ref[b0 + b, h] + _dot_tn(kd[sl(b)], v[sl(b)])
            put(0, h, _head_rmsnorm(o_h, prm(3, h)) * _silu(gate))

        for h in range(H):
            q = rotary(zc(4, h))
            k = rotary(zc(5, h)) * (d ** -0.5)
            v, gate = zc(6, h), zc(7, h)
            A = _dot_nt(q, k) * dm_ref[h]
            o_h = _dot(A, v) + qin_ref[h] * per_seq(lambda b: _dot(q[sl(b)], sr_ref[b0 + b, h]))
            ko = k * kout_ref[h]
            for b in range(n_seq):
                sr_ref[b0 + b, h] = g_chunk[h] * sr_ref[b0 + b, h] + _dot_tn(ko[sl(b)], v[sl(b)])
            put(1, h, _head_layernorm(o_h, prm(4, h)) * _silu(gate))

        u = z_refs[8][rows, :]
        ext_rows = SUBLANES + seq
        for b in range(n_seq):
            base = b * ext_rows
            ext_ref[base + SUBLANES - (CONV_W - 1):base + SUBLANES, :] = cb_ref[b0 + b]
            ext_ref[base + SUBLANES:base + ext_rows, :] = u[sl(b)]
        conv = jnp.zeros((R, W), F32)
        for j in range(CONV_W):
            off = SUBLANES - (CONV_W - 1) + j
            conv = conv + convw_ref[j:j + 1, :] * per_seq(
                lambda b: ext_ref[b * ext_rows + off:b * ext_rows + off + seq, :])
        for b in range(n_seq):
            cb_ref[b0 + b] = ext_ref[(b + 1) * ext_rows - (CONV_W - 1):(b + 1) * ext_rows, :]
        uc = _silu(conv + prm_ref[7:8, :])

        qs, ks, vs = [], [], []
        for h in range(H):
            uch = uc[:, h * d:(h + 1) * d].astype(BF16)
            qs.append(jnp.dot(uch, mq_ref[h], preferred_element_type=F32))
            ks.append(jnp.dot(uch, mk_ref[h], preferred_element_type=F32))
            vs.append(jnp.dot(u[:, h * d:(h + 1) * d].astype(BF16), mv_ref[h], preferred_element_type=F32))
        qkv = jnp.concatenate(qs + ks + vs, axis=1).astype(BF16)
        g_col = jnp.dot(qkv, wg_ref[...], preferred_element_type=F32) + bg_ref[...]
        g_row = lax.dot_general(wgt_ref[...], qkv, (((1,), (1,)), ((), ())),
                                preferred_element_type=F32) + bgt_ref[...]
        i_cols = g_col[:, :H]
        f_cols = _mask_dot(tri_l, _log_sigmoid(g_col[:, H:]))
        i_rows = g_row[:H]
        f_rows = _dot_mask(_log_sigmoid(g_row[H:]), tri_u)
        m_prev_rows = per_seq(lambda b: jnp.broadcast_to(m_ref[b0 + b], (seq, H)))
        F_last = last_rows(f_cols)
        for h in range(H):
            q, k, v = qs[h], ks[h] * (d ** -0.5), vs[h]
            i_col, F_col = i_cols[:, h:h + 1], f_cols[:, h:h + 1]
            a_row = i_rows[h:h + 1] - f_rows[h:h + 1]
            m_prev = m_prev_rows[:, h:h + 1]
            cm = jnp.max(jnp.where(causal, a_row, NEG_INF), -1, keepdims=True)
            m_col = F_col + jnp.maximum(m_prev, cm)
            logd = (F_col - m_col) + a_row
            dmat = jnp.where(causal, jnp.exp(jnp.where(causal, logd, 0.0)), 0.0)
            Sc = _dot_nt(q, k) * dmat
            inter = jnp.exp(F_col + m_prev - m_col)
            n_rows = jnp.broadcast_to(n_ref[b0:b0 + n_seq, h:h + 1, :], (n_seq, seq, d)).reshape(R, d)
            num = _dot(Sc, v) + inter * per_seq(lambda b: _dot(q[sl(b)], c_ref[b0 + b, h]))
            den = jnp.sum(Sc, -1, keepdims=True) + inter * jnp.sum(q * n_rows, -1, keepdims=True)
            hout = num / jnp.maximum(jnp.abs(den), jnp.exp(-m_col))
            mL = last_rows(jnp.broadcast_to(m_col, (R, d)))
            FL = jnp.broadcast_to(F_last[:, h:h + 1], (R, d))
            kw = k * jnp.exp((FL - mL) + (i_col - F_col))
            decay_rows = jnp.exp((FL - mL) + m_prev)
            for b in range(n_seq):
                decay = decay_rows[b * seq:b * seq + 1]
                c_ref[b0 + b, h] = decay * c_ref[b0 + b, h] + _dot_tn(kw[sl(b)], v[sl(b)])
                n_ref[b0 + b, h:h + 1, :] = decay * n_ref[b0 + b, h:h + 1, :] + jnp.sum(kw[sl(b)], 0, keepdims=True)
                m_ref[b0 + b, :, h:h + 1] = mL[b * seq:b * seq + 1, 0:1]
            y = _head_layernorm(hout, prm(5, h)) + prm(6, h) * uc[:, h * d:(h + 1) * d]
            put(2, h, y * _silu(zc(9, h)))

    for c in range(bb * tt // R):
        chunk(c * R)


def _mixers(x, x_row0, B, T, cos, sin, states_in, l, prev_out, p):
    D = x.shape[1]
    depth = states_in[0].shape[0]
    H, d, W = N_HEADS, HEAD_DIM, MIX_WIDTH
    tt = _pick(T, (256, 128, 64, 32, 16, 8))
    n_t = T // tt
    if tt >= 128:
        bb, seq, n_seq = 1, 128, 1
    else:
        assert n_t == 1
        bb = _pick(B, tuple(c for c in (8, 4, 2) if c * tt <= 128) + (1,))
        seq, n_seq = tt, bb
        cos, sin = jnp.tile(cos, (bb, 1)), jnp.tile(sin, (bb, 1))
    assert x_row0 % (bb * tt) == 0
    xb = x_row0 // (bb * tt)
    R = seq * n_seq
    tables, g_chunk = _seq_tables(seq, n_seq)
    consts = [jnp.asarray(t) for t in tables]
    st_spec = pl.BlockSpec((None, bb, H, d, d), lambda b, t: (l, b, 0, 0, 0))
    n_spec = pl.BlockSpec((None, bb, H, d), lambda b, t: (l, b, 0, 0))
    m_spec = pl.BlockSpec((None, bb, 1, H), lambda b, t: (l, b, 0, 0))
    cb_spec = pl.BlockSpec((None, bb, CONV_W - 1, W), lambda b, t: (l, b, 0, 0))
    state_specs = [st_spec, st_spec, st_spec, n_spec, m_spec, cb_spec]

    def full(a):
        nd = a.ndim
        return pl.BlockSpec(a.shape, lambda b, t: (0,) * nd)

    def layer(a):
        nd = a.ndim - 1
        return pl.BlockSpec((None,) + a.shape[1:], lambda b, t: (l,) + (0,) * nd)

    weights = [p['prm'], p['conv_w'], p['mq'], p['mk'], p['mv'], p['m_wg'], p['m_wgt'], p['m_bg'], p['m_bgt']]
    alias_in = list(prev_out) if prev_out is not None else []
    n_fixed = 5 + len(consts) + 6 + len(weights)
    state_shapes = [jax.ShapeDtypeStruct((depth, B, H, d, d), F32)] * 3 + [
        jax.ShapeDtypeStruct((depth, B, H, d), F32),
        jax.ShapeDtypeStruct((depth, B, 1, H), F32),
        jax.ShapeDtypeStruct((depth, B, CONV_W - 1, W), F32)]
    rows_spec = pl.BlockSpec((bb * tt, d), lambda b, t: (t, 0))
    outs = pl.pallas_call(
        functools.partial(_mixer_kernel, tt=tt, seq=seq, n_seq=n_seq, n_alias=len(alias_in),
                          g_chunk=g_chunk),
        grid=(B // bb, n_t),
        in_specs=[pl.BlockSpec((bb * tt, D), lambda b, t: (xb + b * n_t + t, 0)),
                  pl.BlockSpec((None, 1, D), lambda b, t: (l, 0, 0)),
                  pl.BlockSpec((None, D, SCAN_COLS), lambda b, t: (l, 0, 0), pipeline_mode=pl.Buffered(1)),
                  rows_spec, rows_spec]
                 + [full(c) for c in consts] + state_specs + [layer(w) for w in weights]
                 + [pl.BlockSpec(memory_space=pl.ANY)] * len(alias_in),
        out_specs=[pl.BlockSpec((bb, tt, N_BRANCH * W), lambda b, t: (b, t, 0))] + state_specs,
        out_shape=[jax.ShapeDtypeStruct((B, T, N_BRANCH * W), F32)] + state_shapes,
        input_output_aliases={n_fixed + i: 1 + i for i in range(len(alias_in))},
        scratch_shapes=[pltpu.VMEM((bb * tt, W), F32)] * (SCAN_COLS // W)
                       + [pltpu.VMEM((n_seq * (SUBLANES + seq), W), F32)],
        name="mixers",
        compiler_params=_cparams(("parallel", "arbitrary")),
    )(x, p['norm_mix'], p['w_in'], cos, sin, *consts, *states_in, *weights, *alias_in)
    return outs[0], tuple(outs[1:])


def _merge_kernel(x_ref, o_ref, nw_ref, wg0_ref, wg1_ref, wg2_ref, wbr_ref, wout_ref, *rest):
    y_ref = rest[-1]
    x = x_ref[...]
    W = MIX_WIDTH
    hn = _rms(x, nw_ref[...]).astype(BF16)
    merged = jnp.zeros(x.shape, F32)
    for n, wg_ref in enumerate((wg0_ref, wg1_ref, wg2_ref)):
        gz = jnp.dot(hn, wg_ref[...], preferred_element_type=F32)
        proj = jnp.dot(o_ref[:, n * W:(n + 1) * W].astype(BF16), wbr_ref[n], preferred_element_type=F32)
        merged = merged + _sigmoid(gz) * proj
    y_ref[...] = x + jnp.dot(merged.astype(BF16), wout_ref[...], preferred_element_type=F32)


def _merge(x, x_row0, o, nw, w_in, wbr, wout, l, out, out_row0, out_rows):
    n = o.shape[0]
    d = x.shape[1]
    tm = _pick(n, (512, 256, 128))
    assert x_row0 % tm == 0 and out_row0 % tm == 0
    xb, ob = x_row0 // tm, out_row0 // tm
    g0 = SCAN_COLS // d

    def gate_spec(k):
        return pl.BlockSpec((None, d, d), lambda i: (l, 0, g0 + k))

    alias_in = [] if out is None else [out]
    return pl.pallas_call(
        _merge_kernel,
        grid=(n // tm,),
        in_specs=[pl.BlockSpec((tm, d), lambda i: (xb + i, 0)),
                  pl.BlockSpec((tm, N_BRANCH * MIX_WIDTH), lambda i: (i, 0)),
                  pl.BlockSpec((None, 1, d), lambda i: (l, 0, 0)),
                  gate_spec(0), gate_spec(1), gate_spec(2),
                  pl.BlockSpec((None,) + wbr.shape[1:], lambda i: (l, 0, 0, 0)),
                  pl.BlockSpec((None,) + wout.shape[1:], lambda i: (l, 0, 0))]
                 + [pl.BlockSpec(memory_space=pl.ANY)] * len(alias_in),
        out_specs=pl.BlockSpec((tm, d), lambda i: (ob + i, 0)),
        out_shape=jax.ShapeDtypeStruct((out_rows, d), F32),
        input_output_aliases={8: 0} if alias_in else {},
        name="merge",
        compiler_params=_cparams(("parallel",)),
    )(x, o, nw, w_in, w_in, w_in, wbr, wout, *alias_in)


def _router_kernel(x_ref, nw_ref, wr_ref, br_ref, hn_ref, mi_ref, mf_ref, cnt_ref, carry_ref):
    i = pl.program_id(0)
    tm = x_ref.shape[0]
    n_sub = x_ref.shape[1] // LANES

    @pl.when(i == 0)
    def _():
        carry_ref[...] = jnp.zeros(carry_ref.shape, F32)

    hn = _rms(x_ref[...], nw_ref[...])
    for s in range(n_sub):
        hn_ref[pl.ds(s, tm, stride=n_sub), :] = hn[:, s * LANES:(s + 1) * LANES]
    logits = _dot(hn, wr_ref[...]) + br_ref[...]
    lane = lax.broadcasted_iota(jnp.int32, logits.shape, 1)
    big = jnp.int32(1 << 20)

    def first_max(mask):
        vmax = jnp.max(jnp.where(mask, logits, NEG_INF), -1, keepdims=True)
        imax = jnp.min(jnp.where(mask, jnp.where(logits == vmax, lane, big), big), -1, keepdims=True)
        return vmax, imax

    in_groups = lane < N_GROUPS
    g_max, g_top = first_max(in_groups)
    g_w = 1.0 / jnp.sum(jnp.where(in_groups, jnp.exp(logits - g_max), 0.0), -1, keepdims=True)
    e_lo = N_GROUPS + EXPERTS_PER_GROUP * g_top
    in_e = jnp.logical_and(lane >= e_lo, lane < e_lo + EXPERTS_PER_GROUP)
    v1, i1 = first_max(in_e)
    v2, i2 = first_max(jnp.logical_and(in_e, lane != i1))
    e2 = jnp.exp(v2 - v1)
    w1 = g_w / (1.0 + e2)
    w2 = g_w * e2 / (1.0 + e2)

    hit1 = lane == i1
    hit2 = lane == i2
    onehot = jnp.where(hit1, 1.0, 0.0) + jnp.where(hit2, 1.0, 0.0)
    strict = (lax.broadcasted_iota(jnp.int32, (tm, tm), 0) > lax.broadcasted_iota(jnp.int32, (tm, tm), 1))
    before = _dot(strict.astype(F32), onehot) + carry_ref[...]
    r1 = jnp.sum(jnp.where(hit1, before, 0.0), -1, keepdims=True).astype(jnp.int32)
    r2 = jnp.sum(jnp.where(hit2, before, 0.0), -1, keepdims=True).astype(jnp.int32)
    carry_ref[...] = carry_ref[...] + jnp.sum(onehot, 0, keepdims=True)
    cnt_ref[...] = carry_ref[...]

    meta = jnp.where(lane == 0, i1 - N_GROUPS,
                     jnp.where(lane == 1, i2 - N_GROUPS,
                               jnp.where(lane == 2, r1, jnp.where(lane == 3, r2, 0))))
    mi_ref[...] = meta.T[:SUBLANES]
    mf_ref[...] = jnp.where(lane == 0, w1, jnp.where(lane == 1, w2, 0.0))


def _router(x, nw, wr, br, l):
    n, d = x.shape
    n_sub = d // LANES
    tm = _pick(n, (512, 256, 128))
    return pl.pallas_call(
        _router_kernel,
        grid=(n // tm,),
        in_specs=[pl.BlockSpec((tm, d), lambda i: (i, 0)),
                  pl.BlockSpec((None, 1, d), lambda i: (l, 0, 0)),
                  pl.BlockSpec((None, d, LANES), lambda i: (l, 0, 0)),
                  pl.BlockSpec((None, 1, LANES), lambda i: (l, 0, 0))],
        out_specs=[pl.BlockSpec((tm * n_sub, LANES), lambda i: (i, 0)),
                   pl.BlockSpec((SUBLANES, tm), lambda i: (0, i)),
                   pl.BlockSpec((tm, LANES), lambda i: (i, 0)),
                   pl.BlockSpec((1, LANES), lambda i: (0, 0))],
        out_shape=[jax.ShapeDtypeStruct((n * n_sub, LANES), F32),
                   jax.ShapeDtypeStruct((SUBLANES, n), jnp.int32),
                   jax.ShapeDtypeStruct((n, LANES), F32),
                   jax.ShapeDtypeStruct((1, LANES), F32)],
        scratch_shapes=[pltpu.VMEM((1, LANES), F32)],
        name="router",
        compiler_params=_cparams(("arbitrary",)),
    )(x, nw, wr, br)


def _slots_kernel(d0_ref, d1_ref, pair_ref):
    def body(t, carry):
        pair_ref[d0_ref[t]] = TOP_K * t
        pair_ref[d1_ref[t]] = TOP_K * t + 1
        return carry

    lax.fori_loop(0, d0_ref.shape[0], body, 0, unroll=16)


def _slots(d0, d1, cap):
    return pl.pallas_call(
        _slots_kernel,
        grid_spec=pltpu.PrefetchScalarGridSpec(
            num_scalar_prefetch=2,
            grid=(1,),
            in_specs=[],
            out_specs=pl.BlockSpec(memory_space=pltpu.SMEM)),
        out_shape=jax.ShapeDtypeStruct((cap,), jnp.int32),
        name="slots",
        compiler_params=pltpu.CompilerParams(dimension_semantics=("arbitrary",)),
    )(d0, d1)


def _experts_kernel(be_ref, nv_ref, pair_ref, hn_ref, wg_ref, wu_ref, wd_ref, y2_ref,
                    xbuf, ybuf, gsem, ssem, wg_bf, wu_bf, wd_bf, *, n_sub):
    i = pl.program_id(0)
    n_i = pl.num_programs(0)
    slot = i % 2

    def row(r):
        return pl.ds(pl.multiple_of(r * n_sub, n_sub), n_sub)

    def gather(blk, sl, r):
        tok = jnp.right_shift(pair_ref[blk * MOE_ROWS + r], TOP_K.bit_length() - 1)
        return pltpu.make_async_copy(hn_ref.at[row(tok)], xbuf.at[sl, row(r)], gsem.at[sl])

    def scatter(blk, sl, r):
        return pltpu.make_async_copy(ybuf.at[sl, row(r)], y2_ref.at[row(pair_ref[blk * MOE_ROWS + r])], ssem.at[sl])

    def start_valid(copy, blk, sl):
        n_valid = nv_ref[blk]
        n_full = n_valid // ISSUE_UNROLL

        def body(j, carry):
            for k in range(ISSUE_UNROLL):
                copy(blk, sl, j * ISSUE_UNROLL + k).start()
            return carry

        def tail(r, carry):
            copy(blk, sl, r).start()
            return carry

        lax.fori_loop(0, n_full, body, 0)
        lax.fori_loop(n_full * ISSUE_UNROLL, n_valid, tail, 0)

    def wait_valid(buf, blk, sl):
        n_valid = nv_ref[blk]

        @pl.when(n_valid > 0)
        def _():
            rows = pl.ds(0, n_valid * n_sub)
            if buf is xbuf:
                pltpu.make_async_copy(hn_ref.at[rows], xbuf.at[sl, rows], gsem.at[sl]).wait()
            else:
                pltpu.make_async_copy(ybuf.at[sl, rows], y2_ref.at[rows], ssem.at[sl]).wait()

    @pl.when(i == 0)
    def _():
        xbuf[...] = jnp.zeros(xbuf.shape, F32)
        start_valid(gather, 0, 0)

    @pl.when(i + 1 < n_i)
    def _():
        start_valid(gather, i + 1, 1 - slot)

    wait_valid(xbuf, i, slot)

    @pl.when(i >= 2)
    def _():
        wait_valid(ybuf, i - 2, slot)

    @pl.when(nv_ref[i] > 0)
    def _():
        changed = jnp.logical_or(i == 0, be_ref[i] != be_ref[jnp.maximum(i - 1, 0)])

        @pl.when(changed)
        def _():
            wg_bf[...] = wg_ref[...].astype(BF16)
            wu_bf[...] = wu_ref[...].astype(BF16)
            wd_bf[...] = wd_ref[...].astype(BF16)

        x = jnp.concatenate([xbuf[slot, pl.ds(s, MOE_ROWS, stride=n_sub), :] for s in range(n_sub)], axis=1)
        x = x.astype(BF16)
        g = jnp.dot(x, wg_bf[...], preferred_element_type=F32)
        u = jnp.dot(x, wu_bf[...], preferred_element_type=F32)
        y = jnp.dot((_silu(g) * u).astype(BF16), wd_bf[...], preferred_element_type=F32)
        for s in range(n_sub):
            ybuf[slot, pl.ds(s, MOE_ROWS, stride=n_sub), :] = y[:, s * LANES:(s + 1) * LANES]

    start_valid(scatter, i, slot)

    @pl.when(i == n_i - 1)
    def _():
        @pl.when(i >= 1)
        def _():
            wait_valid(ybuf, i - 1, 1 - slot)
        wait_valid(ybuf, i, slot)


def _experts(block_e, n_valid, pairs, hn, w_g, w_u, w_d, l):
    _, _, d, f = w_g.shape
    n_sub = d // LANES
    n = hn.shape[0] // n_sub
    n_blocks = block_e.shape[0]
    buf = pltpu.VMEM((2, MOE_ROWS * n_sub, LANES), F32)
    return pl.pallas_call(
        functools.partial(_experts_kernel, n_sub=n_sub),
        grid_spec=pltpu.PrefetchScalarGridSpec(
            num_scalar_prefetch=3,
            grid=(n_blocks,),
            in_specs=[pl.BlockSpec(memory_space=pl.ANY),
                      pl.BlockSpec((None, None, d, f), lambda i, be, nv, pr: (l, be[i], 0, 0)),
                      pl.BlockSpec((None, None, d, f), lambda i, be, nv, pr: (l, be[i], 0, 0)),
                      pl.BlockSpec((None, None, f, d), lambda i, be, nv, pr: (l, be[i], 0, 0))],
            out_specs=pl.BlockSpec(memory_space=pl.ANY),
            scratch_shapes=[buf, buf, pltpu.SemaphoreType.DMA((2,)), pltpu.SemaphoreType.DMA((2,)),
                            pltpu.VMEM((d, f), BF16), pltpu.VMEM((d, f), BF16), pltpu.VMEM((f, d), BF16)]),
        out_shape=jax.ShapeDtypeStruct((n * TOP_K * n_sub, LANES), F32),
        name="experts",
        compiler_params=pltpu.CompilerParams(dimension_semantics=("arbitrary",), vmem_limit_bytes=VMEM_LIMIT,
                                             has_side_effects=True),
    )(block_e, n_valid, pairs, hn, w_g, w_u, w_d)


def _combine_kernel(x_ref, mf_ref, nw_ref, y2_ref, y_ref, *, final_norm):
    tm = x_ref.shape[0]
    n_sub = x_ref.shape[1] // LANES
    w = mf_ref[...]
    stride = TOP_K * n_sub
    y0 = jnp.concatenate([y2_ref[pl.ds(s, tm, stride=stride), :] for s in range(n_sub)], axis=1)
    y1 = jnp.concatenate([y2_ref[pl.ds(n_sub + s, tm, stride=stride), :] for s in range(n_sub)], axis=1)
    out = x_ref[...] + (y0 * w[:, 0:1] + y1 * w[:, 1:2])
    if final_norm:
        out = _rms(out, nw_ref[...])
    y_ref[...] = out


def _combine(x, mf, nw, y2, final_norm, row0, n):
    d = x.shape[1]
    n_sub = d // LANES
    tm = _pick(n, (256, 128))
    assert row0 % tm == 0
    rb = row0 // tm
    return pl.pallas_call(
        functools.partial(_combine_kernel, final_norm=final_norm),
        grid=(n // tm,),
        in_specs=[pl.BlockSpec((tm, d), lambda i: (rb + i, 0)),
                  pl.BlockSpec((tm, LANES), lambda i: (rb + i, 0)),
                  pl.BlockSpec((1, d), lambda i: (0, 0)),
                  pl.BlockSpec((tm * TOP_K * n_sub, LANES), lambda i: (rb + i, 0))],
        out_specs=pl.BlockSpec((tm, d), lambda i: (i, 0)),
        out_shape=jax.ShapeDtypeStruct((n, d), F32),
        name="combine",
        compiler_params=_cparams(("parallel",)),
    )(x, mf, nw, y2)


def _moe(x, l, p, final_nw, out_ranges):
    n, d = x.shape
    n_sub = d // LANES
    hn, mi, mf, cnt = _router(x, p['norm_ffn'], p['w_r'], p['b_r'], l)
    counts = cnt[0, N_GROUPS:N_GROUPS + N_EXPERTS].astype(jnp.int32)
    padded = (counts + MOE_ROWS - 1) // MOE_ROWS * MOE_ROWS
    pad_end = jnp.cumsum(padded)
    pad_start = pad_end - padded
    experts = jnp.arange(N_EXPERTS, dtype=jnp.int32)

    def lookup(table, e):
        return jnp.sum(jnp.where(e[:, None] == experts[None, :], table[None, :], 0), axis=1)

    d0 = mi[2] + lookup(pad_start, mi[0])
    d1 = mi[3] + lookup(pad_start, mi[1])
    n_blocks = -(-(n * TOP_K) // MOE_ROWS) + N_EXPERTS
    block_row = jnp.arange(n_blocks, dtype=jnp.int32) * MOE_ROWS
    block_e = jnp.minimum(jnp.sum((pad_end[None, :] <= block_row[:, None]).astype(jnp.int32), axis=1), N_EXPERTS - 1)
    used_end = lookup(pad_start + counts, block_e)
    n_valid = jnp.clip(used_end - block_row, 0, MOE_ROWS).astype(jnp.int32)
    pairs = _slots(d0, d1, n_blocks * MOE_ROWS)
    y2 = _experts(block_e.astype(jnp.int32), n_valid, pairs, hn, p['w_eg'], p['w_eu'], p['w_ed'], l)
    nw = (final_nw if final_nw is not None else p['norm_ffn'][l, 0])[None]
    return [_combine(x, mf, nw, y2, final_nw is not None, row0, rows) for row0, rows in out_ranges]


def _rotary_tables(pos):
    half = HEAD_DIM // 2
    inv = ROPE_BASE ** (-jnp.arange(half, dtype=F32) / half)
    ang = pos.astype(F32)[:, None] * inv[None, :]
    cos = jnp.concatenate([jnp.cos(ang), jnp.cos(ang)], -1)
    sin = jnp.concatenate([-jnp.sin(ang), jnp.sin(ang)], -1)
    return cos, sin


def _trunks(xs, positions, states, p):
    depth = p['w_in'].shape[0]
    D = xs[0].shape[-1]
    shapes = [x.shape[:2] for x in xs]
    n_rows = [B * T for B, T in shapes]
    row0 = [sum(n_rows[:i]) for i in range(len(xs))]
    total = sum(n_rows)
    tables = [_rotary_tables(pos) for pos in positions]
    x_arrays = [x.reshape(n, D) for x, n in zip(xs, n_rows)]
    x_row0 = [0] * len(xs)
    new_states = [None] * len(xs)
    for l in range(depth):
        xm = None
        for i, (B, T) in enumerate(shapes):
            o, new_states[i] = _mixers(x_arrays[i], x_row0[i], B, T, *tables[i], states[i], l, new_states[i], p)
            xm = _merge(x_arrays[i], x_row0[i], o.reshape(n_rows[i], N_BRANCH * MIX_WIDTH), p['norm_mix'], p['w_in'],
                        p['w_branch'], p['w_out'], l, xm, row0[i], total)
        if l == depth - 1:
            outs = _moe(xm, l, p, p['norm_final'], list(zip(row0, n_rows)))
        else:
            x_all, = _moe(xm, l, p, None, [(0, total)])
            x_arrays, x_row0 = [x_all] * len(xs), row0
    results = []
    for y, (B, T), st in zip(outs, shapes, new_states):
        sh, sr, c, n, m, cb = st
        results.append((y.reshape(B, T, D), (sh, sr, c, n, m.reshape(depth, B, N_HEADS), cb)))
    return results


def kernel(x_prompt, x_sample, state_hgrn, state_ret, state_mlstm_C, state_mlstm_n, state_mlstm_m, state_mlstm_conv,
           norm_mix, norm_ffn, norm_final, w_in, hgrn_lb, hgrn_norm, ret_norm, mlstm_conv_w, mlstm_conv_b,
           mlstm_wq, mlstm_wk, mlstm_wv, mlstm_w_gates, mlstm_b_gates, mlstm_norm, mlstm_skip, w_branch, w_out,
           w_router_group, b_router_group, w_router_expert, b_router_expert, w_exp_gate, w_exp_up, w_exp_down):
    depth, D = norm_mix.shape
    H, d, W = N_HEADS, HEAD_DIM, MIX_WIDTH
    lb = jnp.cumsum(jax.nn.softmax(hgrn_lb.astype(F32), axis=0), axis=0)
    lb = lb - lb[0:1]
    prm = jnp.stack([jnp.log(lb), jnp.log1p(-lb), 1.0 - lb, hgrn_norm.astype(F32), ret_norm.astype(F32),
                     mlstm_norm.astype(F32), mlstm_skip.astype(F32), mlstm_conv_b.astype(F32)], axis=1)
    pad = LANES - N_GROUPS - N_EXPERTS
    w_r = jnp.concatenate([w_router_group, w_router_expert, jnp.zeros((depth, D, pad), F32)], -1)
    b_r = jnp.concatenate([b_router_group, b_router_expert, jnp.zeros((depth, pad), F32)], -1)[:, None, :]
    wg_bf = mlstm_w_gates.astype(BF16)
    p = {'norm_mix': norm_mix[:, None, :], 'norm_ffn': norm_ffn[:, None, :], 'norm_final': norm_final,
         'w_in': w_in.astype(BF16), 'prm': prm, 'conv_w': mlstm_conv_w.astype(F32),
         'mq': mlstm_wq.astype(BF16), 'mk': mlstm_wk.astype(BF16), 'mv': mlstm_wv.astype(BF16),
         'm_wg': wg_bf, 'm_wgt': jnp.swapaxes(wg_bf, 1, 2),
         'm_bg': mlstm_b_gates[:, None, :], 'm_bgt': mlstm_b_gates[:, :, None],
         'w_branch': w_branch.astype(BF16), 'w_out': w_out.astype(BF16),
         'w_r': w_r, 'b_r': b_r, 'w_eg': w_exp_gate, 'w_eu': w_exp_up, 'w_ed': w_exp_down}

    Bp, Tp = x_prompt.shape[0], x_prompt.shape[1]
    Bs = x_sample.shape[0]
    zero_states = (jnp.zeros((depth, Bp, H, d, d), F32), jnp.zeros((depth, Bp, H, d, d), F32),
                   jnp.zeros((depth, Bp, H, d, d), F32), jnp.zeros((depth, Bp, H, d), F32),
                   jnp.zeros((depth, Bp, 1, H), F32), jnp.zeros((depth, Bp, CONV_W - 1, W), F32))
    pos_prompt = jnp.arange(Tp, dtype=jnp.int32)
    pos_sample = PAST_LEN + jnp.arange(x_sample.shape[1], dtype=jnp.int32)
    sample_states = (state_hgrn, state_ret, state_mlstm_C, state_mlstm_n,
                     state_mlstm_m.reshape(depth, Bs, 1, H), state_mlstm_conv)
    (y_prompt, ps), (y_sample, ss) = _trunks([x_prompt, x_sample], [pos_prompt, pos_sample],
                                             [zero_states, sample_states], p)
    return (y_prompt, y_sample) + ps + ss
```

```python
import functools

import jax
import jax.numpy as jnp
import numpy as np
from jax import lax
from jax.experimental import pallas as pl
from jax.experimental.pallas import tpu as pltpu

F32 = jnp.float32
BF16 = jnp.bfloat16

HEAD_DIM = 128
N_HEADS = 4
MIX_WIDTH = HEAD_DIM * N_HEADS
N_BRANCH = 3
CONV_W = 4
ROPE_BASE = 10000.0
N_GROUPS = 4
EXPERTS_PER_GROUP = 8
N_EXPERTS = N_GROUPS * EXPERTS_PER_GROUP
TOP_K = 2
PAST_LEN = 16384
NORM_EPS = 1e-6
HEAD_NORM_EPS = 1e-5
SCAN_COLS = 10 * MIX_WIDTH

LANES = 128
SUBLANES = 8
VMEM_LIMIT = 48 * 1024 * 1024
MOE_ROWS = 256
ISSUE_UNROLL = 8
NEG_INF = float("-inf")


def _cparams(sem):
    return pltpu.CompilerParams(dimension_semantics=sem, vmem_limit_bytes=VMEM_LIMIT)


def _pick(n, cands):
    for c in cands:
        if n % c == 0:
            return c
    return n


def _rms(x, w):
    return x * lax.rsqrt(jnp.mean(x * x, -1, keepdims=True) + NORM_EPS) * w


def _sigmoid(x):
    return 1.0 / (1.0 + jnp.exp(-x))


def _silu(x):
    return x * _sigmoid(x)


def _log_sigmoid(x):
    return jnp.minimum(x, 0.0) - jnp.log(1.0 + jnp.exp(-jnp.abs(x)))


def _dot(a, b):
    return jnp.dot(a.astype(BF16), b.astype(BF16), preferred_element_type=F32)


def _dot_nt(a, b):
    return lax.dot_general(a.astype(BF16), b.astype(BF16), (((1,), (1,)), ((), ())), preferred_element_type=F32)


def _dot_tn(a, b):
    return jnp.dot(a.T.astype(BF16), b.astype(BF16), preferred_element_type=F32)


def _split3(x):
    hi = x.astype(BF16)
    r = x - hi.astype(F32)
    mid = r.astype(BF16)
    lo = (r - mid.astype(F32)).astype(BF16)
    return hi, mid, lo


def _mask_dot(mask_bf, x):
    return sum(jnp.dot(mask_bf, part, preferred_element_type=F32) for part in _split3(x))


def _dot_mask(x, mask_bf):
    return sum(jnp.dot(part, mask_bf, preferred_element_type=F32) for part in _split3(x))


def _head_rmsnorm(o, w):
    return o * lax.rsqrt(jnp.mean(o * o, -1, keepdims=True) + HEAD_NORM_EPS) * w


def _head_layernorm(o, w):
    c = o - jnp.mean(o, -1, keepdims=True)
    return c * lax.rsqrt(jnp.mean(c * c, -1, keepdims=True) + HEAD_NORM_EPS) * w


def _level_masks(c):
    t = np.arange(c)[:, None]
    s = np.arange(c)[None, :]
    masks = [t == s]
    blk = 2
    while blk <= c:
        masks.append((t // blk == s // blk) & (t % blk >= blk // 2) & (s % blk < blk // 2))
        blk *= 2
    return np.stack(masks).astype(np.float32)


def _retention_tables(c):
    idx = np.arange(c, dtype=np.float64)
    lg = np.log(1.0 - 2.0 ** (-5.0 - np.arange(N_HEADS, dtype=np.float64)))[:, None, None]
    rel = idx[:, None] - idx[None, :]
    dm = np.where(rel >= 0, np.exp(np.maximum(rel, 0.0)[None] * lg), 0.0)
    q_in = np.broadcast_to(np.exp((idx + 1.0)[None, :, None] * lg), (N_HEADS, c, HEAD_DIM))
    k_out = np.broadcast_to(np.exp((c - 1.0 - idx)[None, :, None] * lg), (N_HEADS, c, HEAD_DIM))
    g_chunk = tuple(float(g) for g in np.exp(c * lg[:, 0, 0]))
    return dm.astype(np.float32), q_in.astype(np.float32), k_out.astype(np.float32), g_chunk


def _level_ref(G, blk, row):
    c, d = G.shape
    if blk == 2:
        return jnp.where((row & 1) == 1, pltpu.roll(G, 1, 0), G)
    if blk == 4:
        r = row & 3
        return jnp.where(r == 0, pltpu.roll(G, c - 1, 0),
                         jnp.where(r == 1, G, jnp.where(r == 2, pltpu.roll(G, 1, 0), pltpu.roll(G, 2, 0))))
    mid = blk // 2 - 1
    G3 = G.reshape(c // blk, blk, d)
    return jnp.broadcast_to(G3[:, mid:mid + 1, :], (c // blk, blk, d)).reshape(c, d)


def _seq_tables(seq, n_seq):
    rows = seq * n_seq
    t = np.arange(rows)[:, None]
    s = np.arange(rows)[None, :]
    same = (t // seq) == (s // seq)
    tri = np.stack([same & (t >= s), same & (t <= s)]).astype(np.float32)
    levels = _level_masks(seq)
    lvl = np.zeros((levels.shape[0], rows, rows), np.float32)
    dm1, q_in1, k_out1, g_chunk = _retention_tables(seq)
    dm = np.zeros((N_HEADS, rows, rows), np.float32)
    for b in range(n_seq):
        sl = slice(b * seq, (b + 1) * seq)
        lvl[:, sl, sl] = levels
        dm[:, sl, sl] = dm1
    q_in = np.tile(q_in1, (1, n_seq, 1))
    k_out = np.tile(k_out1, (1, n_seq, 1))
    return [lvl, tri, dm, q_in, k_out], g_chunk


def _mixer_kernel(*refs, tt, seq, n_seq, n_alias, g_chunk):
    (x_ref, nw_ref, w_ref, cos_ref, sin_ref, lvl_ref, tri_ref, dm_ref, qin_ref, kout_ref,
     sh0_ref, sr0_ref, c0_ref, n0_ref, m0_ref, cb0_ref,
     prm_ref, convw_ref, mq_ref, mk_ref, mv_ref, wg_ref, wgt_ref, bg_ref, bgt_ref) = refs[:25]
    o_ref, sh_ref, sr_ref, c_ref, n_ref, m_ref, cb_ref = refs[25 + n_alias:32 + n_alias]
    z_refs = refs[32 + n_alias:-1]
    ext_ref = refs[-1]
    ti = pl.program_id(1)
    d, W, H = HEAD_DIM, MIX_WIDTH, N_HEADS
    bb = sh_ref.shape[0]
    n_lvl = lvl_ref.shape[0]
    R = seq * n_seq

    @pl.when(ti == 0)
    def _():
        sh_ref[...] = sh0_ref[...]
        sr_ref[...] = sr0_ref[...]
        c_ref[...] = c0_ref[...]
        n_ref[...] = n0_ref[...]
        m_ref[...] = m0_ref[...]
        cb_ref[...] = cb0_ref[...]

    hn = _rms(x_ref[...], nw_ref[...]).astype(BF16)
    for j, zj_ref in enumerate(z_refs):
        zj_ref[...] = jnp.dot(hn, w_ref[:, j * W:(j + 1) * W], preferred_element_type=F32)

    def prm(i, h):
        return prm_ref[i:i + 1, h * d:(h + 1) * d]

    tri_l = tri_ref[0].astype(BF16)
    tri_u = tri_ref[1].astype(BF16)
    causal = tri_ref[0] > 0.0
    row = lax.broadcasted_iota(jnp.int32, (R, d), 0)

    def sl(b):
        return slice(b * seq, (b + 1) * seq)

    def per_seq(fn):
        parts = [fn(b) for b in range(n_seq)]
        return parts[0] if n_seq == 1 else jnp.concatenate(parts, axis=0)

    def last_rows(a):
        n = a.shape[1]
        a3 = a.reshape(n_seq, seq, n)
        return jnp.broadcast_to(a3[:, seq - 1:seq, :], (n_seq, seq, n)).reshape(R, n)

    def chunk(c0):
        t0 = c0 if n_seq == 1 else 0
        rows = slice(c0, c0 + R)
        cos = cos_ref[rows, :]
        sin = sin_ref[rows, :]

        def rotary(a_):
            return a_ * cos + pltpu.roll(a_, d // 2, 1) * sin

        def zc(j, h):
            return z_refs[j][rows, h * d:(h + 1) * d]

        def put(j, h, val):
            cols = slice(j * W + h * d, j * W + (h + 1) * d)
            if n_seq == 1:
                o_ref[0, t0:t0 + seq, cols] = val
            else:
                for b in range(n_seq):
                    o_ref[b, :, cols] = val[sl(b)]

        fpre = z_refs[1][rows, :]
        e = jnp.exp(-jnp.abs(fpre))
        a = prm_ref[0:1, :]
        b_ = prm_ref[1:2, :] + (jnp.minimum(fpre, 0.0) - jnp.log(1.0 + e))
        logf = jnp.maximum(a, b_) + jnp.log(1.0 + jnp.exp(-jnp.abs(a - b_)))
        k_all = prm_ref[2:3, :] * (jnp.where(fpre >= 0.0, e, 1.0) / (1.0 + e))
        G_all = _mask_dot(tri_l, logf)
        for h in range(H):
            hs = slice(h * d, (h + 1) * d)
            q, v, gate = zc(0, h), zc(2, h), zc(3, h)
            k, G = k_all[:, hs], G_all[:, hs]
            q_bf, k_bf = q.astype(BF16), k.astype(BF16)
            A = lvl_ref[0] * _dot_nt(q_bf, k_bf)
            for j in range(1, n_lvl):
                E = jnp.exp(-jnp.abs(G - _level_ref(G, 1 << j, row))).astype(BF16)
                A = A + lvl_ref[j] * _dot_nt(q_bf * E, k_bf * E)
            qg = q * jnp.exp(G)
            G_last = last_rows(G)
            kd = k * jnp.exp(G_last - G)
            o_h = _dot(A, v) + per_seq(lambda b: _dot(qg[sl(b)], sh_ref[b, h]))
            for b in range(n_seq):
                decay = jnp.exp(G_last[b * seq:b * seq + SUBLANES]).T[:, 0:1]
                sh_ref[b, h] = decay * sh_ref[b, h] + _dot_tn(kd[sl(b)], v[sl(b)])
            put(0, h, _head_rmsnorm(o_h, prm(3, h)) * _silu(gate))

        for h in range(H):
            q = rotary(zc(4, h))
            k = rotary(zc(5, h)) * (d ** -0.5)
            v, gate = zc(6, h), zc(7, h)
            A = _dot_nt(q, k) * dm_ref[h]
            o_h = _dot(A, v) + qin_ref[h] * per_seq(lambda b: _dot(q[sl(b)], sr_ref[b, h]))
            ko = k * kout_ref[h]
            for b in range(n_seq):
                sr_ref[b, h] = g_chunk[h] * sr_ref[b, h] + _dot_tn(ko[sl(b)], v[sl(b)])
            put(1, h, _head_layernorm(o_h, prm(4, h)) * _silu(gate))

        u = z_refs[8][rows, :]
        ext_rows = SUBLANES + seq
        for b in range(n_seq):
            base = b * ext_rows
            ext_ref[base + SUBLANES - (CONV_W - 1):base + SUBLANES, :] = cb_ref[b]
            ext_ref[base + SUBLANES:base + ext_rows, :] = u[sl(b)]
        conv = jnp.zeros((R, W), F32)
        for j in range(CONV_W):
            off = SUBLANES - (CONV_W - 1) + j
            conv = conv + convw_ref[j:j + 1, :] * per_seq(
                lambda b: ext_ref[b * ext_rows + off:b * ext_rows + off + seq, :])
        for b in range(n_seq):
            cb_ref[b] = ext_ref[(b + 1) * ext_rows - (CONV_W - 1):(b + 1) * ext_rows, :]
        uc = _silu(conv + prm_ref[7:8, :])

        qs, ks, vs = [], [], []
        for h in range(H):
            uch = uc[:, h * d:(h + 1) * d].astype(BF16)
            qs.append(jnp.dot(uch, mq_ref[h], preferred_element_type=F32))
            ks.append(jnp.dot(uch, mk_ref[h], preferred_element_type=F32))
            vs.append(jnp.dot(u[:, h * d:(h + 1) * d].astype(BF16), mv_ref[h], preferred_element_type=F32))
        qkv = jnp.concatenate(qs + ks + vs, axis=1).astype(BF16)
        g_col = jnp.dot(qkv, wg_ref[...], preferred_element_type=F32) + bg_ref[...]
        g_row = lax.dot_general(wgt_ref[...], qkv, (((1,), (1,)), ((), ())),
                                preferred_element_type=F32) + bgt_ref[...]
        i_cols = g_col[:, :H]
        f_cols = _mask_dot(tri_l, _log_sigmoid(g_col[:, H:]))
        i_rows = g_row[:H]
        f_rows = _dot_mask(_log_sigmoid(g_row[H:]), tri_u)
        m_prev_rows = per_seq(lambda b: jnp.broadcast_to(m_ref[b], (seq, H)))
        F_last = last_rows(f_cols)
        for h in range(H):
            q, k, v = qs[h], ks[h] * (d ** -0.5), vs[h]
            i_col, F_col = i_cols[:, h:h + 1], f_cols[:, h:h + 1]
            a_row = i_rows[h:h + 1] - f_rows[h:h + 1]
            m_prev = m_prev_rows[:, h:h + 1]
            cm = jnp.max(jnp.where(causal, a_row, NEG_INF), -1, keepdims=True)
            m_col = F_col + jnp.maximum(m_prev, cm)
            logd = (F_col - m_col) + a_row
            dmat = jnp.where(causal, jnp.exp(jnp.where(causal, logd, 0.0)), 0.0)
            Sc = _dot_nt(q, k) * dmat
            inter = jnp.exp(F_col + m_prev - m_col)
            n_rows = jnp.broadcast_to(n_ref[:, h:h + 1, :], (n_seq, seq, d)).reshape(R, d)
            num = _dot(Sc, v) + inter * per_seq(lambda b: _dot(q[sl(b)], c_ref[b, h]))
            den = jnp.sum(Sc, -1, keepdims=True) + inter * jnp.sum(q * n_rows, -1, keepdims=True)
            hout = num / jnp.maximum(jnp.abs(den), jnp.exp(-m_col))
            mL = last_rows(jnp.broadcast_to(m_col, (R, d)))
            FL = jnp.broadcast_to(F_last[:, h:h + 1], (R, d))
            kw = k * jnp.exp((FL - mL) + (i_col - F_col))
            decay_rows = jnp.exp((FL - mL) + m_prev)
            for b in range(n_seq):
                decay = decay_rows[b * seq:b * seq + 1]
                c_ref[b, h] = decay * c_ref[b, h] + _dot_tn(kw[sl(b)], v[sl(b)])
                n_ref[b, h:h + 1, :] = decay * n_ref[b, h:h + 1, :] + jnp.sum(kw[sl(b)], 0, keepdims=True)
                m_ref[b, :, h:h + 1] = mL[b * seq:b * seq + 1, 0:1]
            y = _head_layernorm(hout, prm(5, h)) + prm(6, h) * uc[:, h * d:(h + 1) * d]
            put(2, h, y * _silu(zc(9, h)))

    for c in range(bb * tt // R):
        chunk(c * R)


def _mixers(x, x_row0, B, T, cos, sin, states_in, l, prev_out, p):
    D = x.shape[1]
    depth = states_in[0].shape[0]
    H, d, W = N_HEADS, HEAD_DIM, MIX_WIDTH
    tt = _pick(T, (256, 128, 64, 32, 16, 8))
    n_t = T // tt
    if tt >= 128:
        bb, seq, n_seq = 1, 128, 1
    else:
        assert n_t == 1
        bb = _pick(B, tuple(c for c in (8, 4, 2) if c * tt <= 128) + (1,))
        seq, n_seq = tt, bb
        cos, sin = jnp.tile(cos, (bb, 1)), jnp.tile(sin, (bb, 1))
    assert x_row0 % (bb * tt) == 0
    xb = x_row0 // (bb * tt)
    R = seq * n_seq
    tables, g_chunk = _seq_tables(seq, n_seq)
    consts = [jnp.asarray(t) for t in tables]
    st_spec = pl.BlockSpec((None, bb, H, d, d), lambda b, t: (l, b, 0, 0, 0))
    n_spec = pl.BlockSpec((None, bb, H, d), lambda b, t: (l, b, 0, 0))
    m_spec = pl.BlockSpec((None, bb, 1, H), lambda b, t: (l, b, 0, 0))
    cb_spec = pl.BlockSpec((None, bb, CONV_W - 1, W), lambda b, t: (l, b, 0, 0))
    state_specs = [st_spec, st_spec, st_spec, n_spec, m_spec, cb_spec]

    def full(a):
        nd = a.ndim
        return pl.BlockSpec(a.shape, lambda b, t: (0,) * nd)

    def layer(a):
        nd = a.ndim - 1
        return pl.BlockSpec((None,) + a.shape[1:], lambda b, t: (l,) + (0,) * nd)

    weights = [p['prm'], p['conv_w'], p['mq'], p['mk'], p['mv'], p['m_wg'], p['m_wgt'], p['m_bg'], p['m_bgt']]
    alias_in = list(prev_out) if prev_out is not None else []
    n_fixed = 5 + len(consts) + 6 + len(weights)
    state_shapes = [jax.ShapeDtypeStruct((depth, B, H, d, d), F32)] * 3 + [
        jax.ShapeDtypeStruct((depth, B, H, d), F32),
        jax.ShapeDtypeStruct((depth, B, 1, H), F32),
        jax.ShapeDtypeStruct((depth, B, CONV_W - 1, W), F32)]
    rows_spec = pl.BlockSpec((bb * tt, d), lambda b, t: (t, 0))
    outs = pl.pallas_call(
        functools.partial(_mixer_kernel, tt=tt, seq=seq, n_seq=n_seq, n_alias=len(alias_in),
                          g_chunk=g_chunk),
        grid=(B // bb, n_t),
        in_specs=[pl.BlockSpec((bb * tt, D), lambda b, t: (xb + b * n_t + t, 0)),
                  pl.BlockSpec((None, 1, D), lambda b, t: (l, 0, 0)),
                  pl.BlockSpec((None, D, SCAN_COLS), lambda b, t: (l, 0, 0), pipeline_mode=pl.Buffered(1)),
                  rows_spec, rows_spec]
                 + [full(c) for c in consts] + state_specs + [layer(w) for w in weights]
                 + [pl.BlockSpec(memory_space=pl.ANY)] * len(alias_in),
        out_specs=[pl.BlockSpec((bb, tt, N_BRANCH * W), lambda b, t: (b, t, 0))] + state_specs,
        out_shape=[jax.ShapeDtypeStruct((B, T, N_BRANCH * W), F32)] + state_shapes,
        input_output_aliases={n_fixed + i: 1 + i for i in range(len(alias_in))},
        scratch_shapes=[pltpu.VMEM((bb * tt, W), F32)] * (SCAN_COLS // W)
                       + [pltpu.VMEM((n_seq * (SUBLANES + seq), W), F32)],
        name="mixers",
        compiler_params=_cparams(("parallel", "arbitrary")),
    )(x, p['norm_mix'], p['w_in'], cos, sin, *consts, *states_in, *weights, *alias_in)
    return outs[0], tuple(outs[1:])


def _merge_kernel(x_ref, o_ref, nw_ref, wg0_ref, wg1_ref, wg2_ref, wbr_ref, wout_ref, *rest):
    y_ref = rest[-1]
    x = x_ref[...]
    W = MIX_WIDTH
    hn = _rms(x, nw_ref[...]).astype(BF16)
    merged = jnp.zeros(x.shape, F32)
    for n, wg_ref in enumerate((wg0_ref, wg1_ref, wg2_ref)):
        gz = jnp.dot(hn, wg_ref[...], preferred_element_type=F32)
        proj = jnp.dot(o_ref[:, n * W:(n + 1) * W].astype(BF16), wbr_ref[n], preferred_element_type=F32)
        merged = merged + _sigmoid(gz) * proj
    y_ref[...] = x + jnp.dot(merged.astype(BF16), wout_ref[...], preferred_element_type=F32)


def _merge(x, x_row0, o, nw, w_in, wbr, wout, l, out, out_row0, out_rows):
    n = o.shape[0]
    d = x.shape[1]
    tm = _pick(n, (512, 256, 128))
    assert x_row0 % tm == 0 and out_row0 % tm == 0
    xb, ob = x_row0 // tm, out_row0 // tm
    g0 = SCAN_COLS // d

    def gate_spec(k):
        return pl.BlockSpec((None, d, d), lambda i: (l, 0, g0 + k))

    alias_in = [] if out is None else [out]
    return pl.pallas_call(
        _merge_kernel,
        grid=(n // tm,),
        in_specs=[pl.BlockSpec((tm, d), lambda i: (xb + i, 0)),
                  pl.BlockSpec((tm, N_BRANCH * MIX_WIDTH), lambda i: (i, 0)),
                  pl.BlockSpec((None, 1, d), lambda i: (l, 0, 0)),
                  gate_spec(0), gate_spec(1), gate_spec(2),
                  pl.BlockSpec((None,) + wbr.shape[1:], lambda i: (l, 0, 0, 0)),
                  pl.BlockSpec((None,) + wout.shape[1:], lambda i: (l, 0, 0))]
                 + [pl.BlockSpec(memory_space=pl.ANY)] * len(alias_in),
        out_specs=pl.BlockSpec((tm, d), lambda i: (ob + i, 0)),
        out_shape=jax.ShapeDtypeStruct((out_rows, d), F32),
        input_output_aliases={8: 0} if alias_in else {},
        name="merge",
        compiler_params=_cparams(("parallel",)),
    )(x, o, nw, w_in, w_in, w_in, wbr, wout, *alias_in)


def _router_kernel(x_ref, nw_ref, wr_ref, br_ref, hn_ref, mi_ref, mf_ref, cnt_ref, carry_ref):
    i = pl.program_id(0)
    tm = x_ref.shape[0]
    n_sub = x_ref.shape[1] // LANES

    @pl.when(i == 0)
    def _():
        carry_ref[...] = jnp.zeros(carry_ref.shape, F32)

    hn = _rms(x_ref[...], nw_ref[...])
    for s in range(n_sub):
        hn_ref[pl.ds(s, tm, stride=n_sub), :] = hn[:, s * LANES:(s + 1) * LANES]
    logits = _dot(hn, wr_ref[...]) + br_ref[...]
    lane = lax.broadcasted_iota(jnp.int32, logits.shape, 1)
    big = jnp.int32(1 << 20)

    def first_max(mask):
        vmax = jnp.max(jnp.where(mask, logits, NEG_INF), -1, keepdims=True)
        imax = jnp.min(jnp.where(mask, jnp.where(logits == vmax, lane, big), big), -1, keepdims=True)
        return vmax, imax

    in_groups = lane < N_GROUPS
    g_max, g_top = first_max(in_groups)
    g_w = 1.0 / jnp.sum(jnp.where(in_groups, jnp.exp(logits - g_max), 0.0), -1, keepdims=True)
    e_lo = N_GROUPS + EXPERTS_PER_GROUP * g_top
    in_e = jnp.logical_and(lane >= e_lo, lane < e_lo + EXPERTS_PER_GROUP)
    v1, i1 = first_max(in_e)
    v2, i2 = first_max(jnp.logical_and(in_e, lane != i1))
    e2 = jnp.exp(v2 - v1)
    w1 = g_w / (1.0 + e2)
    w2 = g_w * e2 / (1.0 + e2)

    hit1 = lane == i1
    hit2 = lane == i2
    onehot = jnp.where(hit1, 1.0, 0.0) + jnp.where(hit2, 1.0, 0.0)
    strict = (lax.broadcasted_iota(jnp.int32, (tm, tm), 0) > lax.broadcasted_iota(jnp.int32, (tm, tm), 1))
    before = _dot(strict.astype(F32), onehot) + carry_ref[...]
    r1 = jnp.sum(jnp.where(hit1, before, 0.0), -1, keepdims=True).astype(jnp.int32)
    r2 = jnp.sum(jnp.where(hit2, before, 0.0), -1, keepdims=True).astype(jnp.int32)
    carry_ref[...] = carry_ref[...] + jnp.sum(onehot, 0, keepdims=True)
    cnt_ref[...] = carry_ref[...]

    meta = jnp.where(lane == 0, i1 - N_GROUPS,
                     jnp.where(lane == 1, i2 - N_GROUPS,
                               jnp.where(lane == 2, r1, jnp.where(lane == 3, r2, 0))))
    mi_ref[...] = meta.T[:SUBLANES]
    mf_ref[...] = jnp.where(lane == 0, w1, jnp.where(lane == 1, w2, 0.0))


def _router(x, nw, wr, br, l):
    n, d = x.shape
    n_sub = d // LANES
    tm = _pick(n, (512, 256, 128))
    return pl.pallas_call(
        _router_kernel,
        grid=(n // tm,),
        in_specs=[pl.BlockSpec((tm, d), lambda i: (i, 0)),
                  pl.BlockSpec((None, 1, d), lambda i: (l, 0, 0)),
                  pl.BlockSpec((None, d, LANES), lambda i: (l, 0, 0)),
                  pl.BlockSpec((None, 1, LANES), lambda i: (l, 0, 0))],
        out_specs=[pl.BlockSpec((tm * n_sub, LANES), lambda i: (i, 0)),
                   pl.BlockSpec((SUBLANES, tm), lambda i: (0, i)),
                   pl.BlockSpec((tm, LANES), lambda i: (i, 0)),
                   pl.BlockSpec((1, LANES), lambda i: (0, 0))],
        out_shape=[jax.ShapeDtypeStruct((n * n_sub, LANES), F32),
                   jax.ShapeDtypeStruct((SUBLANES, n), jnp.int32),
                   jax.ShapeDtypeStruct((n, LANES), F32),
                   jax.ShapeDtypeStruct((1, LANES), F32)],
        scratch_shapes=[pltpu.VMEM((1, LANES), F32)],
        name="router",
        compiler_params=_cparams(("arbitrary",)),
    )(x, nw, wr, br)


def _slots_kernel(d0_ref, d1_ref, pair_ref):
    def body(t, carry):
        pair_ref[d0_ref[t]] = TOP_K * t
        pair_ref[d1_ref[t]] = TOP_K * t + 1
        return carry

    lax.fori_loop(0, d0_ref.shape[0], body, 0, unroll=16)


def _slots(d0, d1, cap):
    return pl.pallas_call(
        _slots_kernel,
        grid_spec=pltpu.PrefetchScalarGridSpec(
            num_scalar_prefetch=2,
            grid=(1,),
            in_specs=[],
            out_specs=pl.BlockSpec(memory_space=pltpu.SMEM)),
        out_shape=jax.ShapeDtypeStruct((cap,), jnp.int32),
        name="slots",
        compiler_params=pltpu.CompilerParams(dimension_semantics=("arbitrary",)),
    )(d0, d1)


def _experts_kernel(be_ref, nv_ref, pair_ref, hn_ref, wg_ref, wu_ref, wd_ref, y2_ref,
                    xbuf, ybuf, gsem, ssem, wg_bf, wu_bf, wd_bf, *, n_sub):
    i = pl.program_id(0)
    n_i = pl.num_programs(0)
    slot = i % 2

    def row(r):
        return pl.ds(pl.multiple_of(r * n_sub, n_sub), n_sub)

    def gather(blk, sl, r):
        tok = jnp.right_shift(pair_ref[blk * MOE_ROWS + r], TOP_K.bit_length() - 1)
        return pltpu.make_async_copy(hn_ref.at[row(tok)], xbuf.at[sl, row(r)], gsem.at[sl])

    def scatter(blk, sl, r):
        return pltpu.make_async_copy(ybuf.at[sl, row(r)], y2_ref.at[row(pair_ref[blk * MOE_ROWS + r])], ssem.at[sl])

    def start_valid(copy, blk, sl):
        n_valid = nv_ref[blk]
        n_full = n_valid // ISSUE_UNROLL

        def body(j, carry):
            for k in range(ISSUE_UNROLL):
                copy(blk, sl, j * ISSUE_UNROLL + k).start()
            return carry

        def tail(r, carry):
            copy(blk, sl, r).start()
            return carry

        lax.fori_loop(0, n_full, body, 0)
        lax.fori_loop(n_full * ISSUE_UNROLL, n_valid, tail, 0)

    def wait_valid(buf, blk, sl):
        n_valid = nv_ref[blk]

        @pl.when(n_valid > 0)
        def _():
            rows = pl.ds(0, n_valid * n_sub)
            if buf is xbuf:
                pltpu.make_async_copy(hn_ref.at[rows], xbuf.at[sl, rows], gsem.at[sl]).wait()
            else:
                pltpu.make_async_copy(ybuf.at[sl, rows], y2_ref.at[rows], ssem.at[sl]).wait()

    @pl.when(i == 0)
    def _():
        xbuf[...] = jnp.zeros(xbuf.shape, F32)
        start_valid(gather, 0, 0)

    @pl.when(i + 1 < n_i)
    def _():
        start_valid(gather, i + 1, 1 - slot)

    wait_valid(xbuf, i, slot)

    @pl.when(i >= 2)
    def _():
        wait_valid(ybuf, i - 2, slot)

    @pl.when(nv_ref[i] > 0)
    def _():
        changed = jnp.logical_or(i == 0, be_ref[i] != be_ref[jnp.maximum(i - 1, 0)])

        @pl.when(changed)
        def _():
            wg_bf[...] = wg_ref[...].astype(BF16)
            wu_bf[...] = wu_ref[...].astype(BF16)
            wd_bf[...] = wd_ref[...].astype(BF16)

        x = jnp.concatenate([xbuf[slot, pl.ds(s, MOE_ROWS, stride=n_sub), :] for s in range(n_sub)], axis=1)
        x = x.astype(BF16)
        g = jnp.dot(x, wg_bf[...], preferred_element_type=F32)
        u = jnp.dot(x, wu_bf[...], preferred_element_type=F32)
        y = jnp.dot((_silu(g) * u).astype(BF16), wd_bf[...], preferred_element_type=F32)
        for s in range(n_sub):
            ybuf[slot, pl.ds(s, MOE_ROWS, stride=n_sub), :] = y[:, s * LANES:(s + 1) * LANES]

    start_valid(scatter, i, slot)

    @pl.when(i == n_i - 1)
    def _():
        @pl.when(i >= 1)
        def _():
            wait_valid(ybuf, i - 1, 1 - slot)
        wait_valid(ybuf, i, slot)


def _experts(block_e, n_valid, pairs, hn, w_g, w_u, w_d, l):
    _, _, d, f = w_g.shape
    n_sub = d // LANES
    n = hn.shape[0] // n_sub
    n_blocks = block_e.shape[0]
    buf = pltpu.VMEM((2, MOE_ROWS * n_sub, LANES), F32)
    return pl.pallas_call(
        functools.partial(_experts_kernel, n_sub=n_sub),
        grid_spec=pltpu.PrefetchScalarGridSpec(
            num_scalar_prefetch=3,
            grid=(n_blocks,),
            in_specs=[pl.BlockSpec(memory_space=pl.ANY),
                      pl.BlockSpec((None, None, d, f), lambda i, be, nv, pr: (l, be[i], 0, 0)),
                      pl.BlockSpec((None, None, d, f), lambda i, be, nv, pr: (l, be[i], 0, 0)),
                      pl.BlockSpec((None, None, f, d), lambda i, be, nv, pr: (l, be[i], 0, 0))],
            out_specs=pl.BlockSpec(memory_space=pl.ANY),
            scratch_shapes=[buf, buf, pltpu.SemaphoreType.DMA((2,)), pltpu.SemaphoreType.DMA((2,)),
                            pltpu.VMEM((d, f), BF16), pltpu.VMEM((d, f), BF16), pltpu.VMEM((f, d), BF16)]),
        out_shape=jax.ShapeDtypeStruct((n * TOP_K * n_sub, LANES), F32),
        name="experts",
        compiler_params=pltpu.CompilerParams(dimension_semantics=("arbitrary",), vmem_limit_bytes=VMEM_LIMIT,
                                             has_side_effects=True),
    )(block_e, n_valid, pairs, hn, w_g, w_u, w_d)


def _combine_kernel(x_ref, mf_ref, nw_ref, y2_ref, y_ref, *, final_norm):
    tm = x_ref.shape[0]
    n_sub = x_ref.shape[1] // LANES
    w = mf_ref[...]
    stride = TOP_K * n_sub
    y0 = jnp.concatenate([y2_ref[pl.ds(s, tm, stride=stride), :] for s in range(n_sub)], axis=1)
    y1 = jnp.concatenate([y2_ref[pl.ds(n_sub + s, tm, stride=stride), :] for s in range(n_sub)], axis=1)
    out = x_ref[...] + (y0 * w[:, 0:1] + y1 * w[:, 1:2])
    if final_norm:
        out = _rms(out, nw_ref[...])
    y_ref[...] = out


def _combine(x, mf, nw, y2, final_norm, row0, n):
    d = x.shape[1]
    n_sub = d // LANES
    tm = _pick(n, (256, 128))
    assert row0 % tm == 0
    rb = row0 // tm
    return pl.pallas_call(
        functools.partial(_combine_kernel, final_norm=final_norm),
        grid=(n // tm,),
        in_specs=[pl.BlockSpec((tm, d), lambda i: (rb + i, 0)),
                  pl.BlockSpec((tm, LANES), lambda i: (rb + i, 0)),
                  pl.BlockSpec((1, d), lambda i: (0, 0)),
                  pl.BlockSpec((tm * TOP_K * n_sub, LANES), lambda i: (rb + i, 0))],
        out_specs=pl.BlockSpec((tm, d), lambda i: (i, 0)),
        out_shape=jax.ShapeDtypeStruct((n, d), F32),
        name="combine",
        compiler_params=_cparams(("parallel",)),
    )(x, mf, nw, y2)


def _moe(x, l, p, final_nw, out_ranges):
    n, d = x.shape
    n_sub = d // LANES
    hn, mi, mf, cnt = _router(x, p['norm_ffn'], p['w_r'], p['b_r'], l)
    counts = cnt[0, N_GROUPS:N_GROUPS + N_EXPERTS].astype(jnp.int32)
    padded = (counts + MOE_ROWS - 1) // MOE_ROWS * MOE_ROWS
    pad_end = jnp.cumsum(padded)
    pad_start = pad_end - padded
    experts = jnp.arange(N_EXPERTS, dtype=jnp.int32)

    def lookup(table, e):
        return jnp.sum(jnp.where(e[:, None] == experts[None, :], table[None, :], 0), axis=1)

    d0 = mi[2] + lookup(pad_start, mi[0])
    d1 = mi[3] + lookup(pad_start, mi[1])
    n_blocks = -(-(n * TOP_K) // MOE_ROWS) + N_EXPERTS
    block_row = jnp.arange(n_blocks, dtype=jnp.int32) * MOE_ROWS
    block_e = jnp.minimum(jnp.sum((pad_end[None, :] <= block_row[:, None]).astype(jnp.int32), axis=1), N_EXPERTS - 1)
    used_end = lookup(pad_start + counts, block_e)
    n_valid = jnp.clip(used_end - block_row, 0, MOE_ROWS).astype(jnp.int32)
    pairs = _slots(d0, d1, n_blocks * MOE_ROWS)
    y2 = _experts(block_e.astype(jnp.int32), n_valid, pairs, hn, p['w_eg'], p['w_eu'], p['w_ed'], l)
    nw = (final_nw if final_nw is not None else p['norm_ffn'][l, 0])[None]
    return [_combine(x, mf, nw, y2, final_nw is not None, row0, rows) for row0, rows in out_ranges]


def _rotary_tables(pos):
    half = HEAD_DIM // 2
    inv = ROPE_BASE ** (-jnp.arange(half, dtype=F32) / half)
    ang = pos.astype(F32)[:, None] * inv[None, :]
    cos = jnp.concatenate([jnp.cos(ang), jnp.cos(ang)], -1)
    sin = jnp.concatenate([-jnp.sin(ang), jnp.sin(ang)], -1)
    return cos, sin


def _trunks(xs, positions, states, p):
    depth = p['w_in'].shape[0]
    D = xs[0].shape[-1]
    shapes = [x.shape[:2] for x in xs]
    n_rows = [B * T for B, T in shapes]
    row0 = [sum(n_rows[:i]) for i in range(len(xs))]
    total = sum(n_rows)
    tables = [_rotary_tables(pos) for pos in positions]
    x_arrays = [x.reshape(n, D) for x, n in zip(xs, n_rows)]
    x_row0 = [0] * len(xs)
    new_states = [None] * len(xs)
    for l in range(depth):
        xm = None
        for i, (B, T) in enumerate(shapes):
            o, new_states[i] = _mixers(x_arrays[i], x_row0[i], B, T, *tables[i], states[i], l, new_states[i], p)
            xm = _merge(x_arrays[i], x_row0[i], o.reshape(n_rows[i], N_BRANCH * MIX_WIDTH), p['norm_mix'], p['w_in'],
                        p['w_branch'], p['w_out'], l, xm, row0[i], total)
        if l == depth - 1:
            outs = _moe(xm, l, p, p['norm_final'], list(zip(row0, n_rows)))
        else:
            x_all, = _moe(xm, l, p, None, [(0, total)])
            x_arrays, x_row0 = [x_all] * len(xs), row0
    results = []
    for y, (B, T), st in zip(outs, shapes, new_states):
        sh, sr, c, n, m, cb = st
        results.append((y.reshape(B, T, D), (sh, sr, c, n, m.reshape(depth, B, N_HEADS), cb)))
    return results


def kernel(x_prompt, x_sample, state_hgrn, state_ret, state_mlstm_C, state_mlstm_n, state_mlstm_m, state_mlstm_conv,
           norm_mix, norm_ffn, norm_final, w_in, hgrn_lb, hgrn_norm, ret_norm, mlstm_conv_w, mlstm_conv_b,
           mlstm_wq, mlstm_wk, mlstm_wv, mlstm_w_gates, mlstm_b_gates, mlstm_norm, mlstm_skip, w_branch, w_out,
           w_router_group, b_router_group, w_router_expert, b_router_expert, w_exp_gate, w_exp_up, w_exp_down):
    depth, D = norm_mix.shape
    H, d, W = N_HEADS, HEAD_DIM, MIX_WIDTH
    lb = jnp.cumsum(jax.nn.softmax(hgrn_lb.astype(F32), axis=0), axis=0)
    lb = lb - lb[0:1]
    prm = jnp.stack([jnp.log(lb), jnp.log1p(-lb), 1.0 - lb, hgrn_norm.astype(F32), ret_norm.astype(F32),
                     mlstm_norm.astype(F32), mlstm_skip.astype(F32), mlstm_conv_b.astype(F32)], axis=1)
    pad = LANES - N_GROUPS - N_EXPERTS
    w_r = jnp.concatenate([w_router_group, w_router_expert, jnp.zeros((depth, D, pad), F32)], -1)
    b_r = jnp.concatenate([b_router_group, b_router_expert, jnp.zeros((depth, pad), F32)], -1)[:, None, :]
    wg_bf = mlstm_w_gates.astype(BF16)
    p = {'norm_mix': norm_mix[:, None, :], 'norm_ffn': norm_ffn[:, None, :], 'norm_final': norm_final,
         'w_in': w_in.astype(BF16), 'prm': prm, 'conv_w': mlstm_conv_w.astype(F32),
         'mq': mlstm_wq.astype(BF16), 'mk': mlstm_wk.astype(BF16), 'mv': mlstm_wv.astype(BF16),
         'm_wg': wg_bf, 'm_wgt': jnp.swapaxes(wg_bf, 1, 2),
         'm_bg': mlstm_b_gates[:, None, :], 'm_bgt': mlstm_b_gates[:, :, None],
         'w_branch': w_branch.astype(BF16), 'w_out': w_out.astype(BF16),
         'w_r': w_r, 'b_r': b_r, 'w_eg': w_exp_gate, 'w_eu': w_exp_up, 'w_ed': w_exp_down}

    Bp, Tp = x_prompt.shape[0], x_prompt.shape[1]
    Bs = x_sample.shape[0]
    zero_states = (jnp.zeros((depth, Bp, H, d, d), F32), jnp.zeros((depth, Bp, H, d, d), F32),
                   jnp.zeros((depth, Bp, H, d, d), F32), jnp.zeros((depth, Bp, H, d), F32),
                   jnp.zeros((depth, Bp, 1, H), F32), jnp.zeros((depth, Bp, CONV_W - 1, W), F32))
    pos_prompt = jnp.arange(Tp, dtype=jnp.int32)
    pos_sample = PAST_LEN + jnp.arange(x_sample.shape[1], dtype=jnp.int32)
    sample_states = (state_hgrn, state_ret, state_mlstm_C, state_mlstm_n,
                     state_mlstm_m.reshape(depth, Bs, 1, H), state_mlstm_conv)
    (y_prompt, ps), (y_sample, ss) = _trunks([x_prompt, x_sample], [pos_prompt, pos_sample],
                                             [zero_states, sample_states], p)
    return (y_prompt, y_sample) + ps + ss
```

```python
import functools

import jax
import jax.numpy as jnp
import numpy as np
from jax import lax
from jax.experimental import pallas as pl
from jax.experimental.pallas import tpu as pltpu

F32 = jnp.float32
BF16 = jnp.bfloat16

HEAD_DIM = 128
N_HEADS = 4
MIX_WIDTH = HEAD_DIM * N_HEADS
N_BRANCH = 3
CONV_W = 4
ROPE_BASE = 10000.0
N_GROUPS = 4
EXPERTS_PER_GROUP = 8
N_EXPERTS = N_GROUPS * EXPERTS_PER_GROUP
TOP_K = 2
PAST_LEN = 16384
NORM_EPS = 1e-6
HEAD_NORM_EPS = 1e-5
SCAN_COLS = 10 * MIX_WIDTH

LANES = 128
SUBLANES = 8
VMEM_LIMIT = 48 * 1024 * 1024
MOE_ROWS = 256
ISSUE_UNROLL = 8
NEG_INF = float("-inf")


def _cparams(sem):
    return pltpu.CompilerParams(dimension_semantics=sem, vmem_limit_bytes=VMEM_LIMIT)


def _pick(n, cands):
    for c in cands:
        if n % c == 0:
            return c
    return n


def _rms(x, w):
    return x * lax.rsqrt(jnp.mean(x * x, -1, keepdims=True) + NORM_EPS) * w


def _sigmoid(x):
    return 1.0 / (1.0 + jnp.exp(-x))


def _silu(x):
    return x * _sigmoid(x)


def _log_sigmoid(x):
    return jnp.minimum(x, 0.0) - jnp.log(1.0 + jnp.exp(-jnp.abs(x)))


def _dot(a, b):
    return jnp.dot(a.astype(BF16), b.astype(BF16), preferred_element_type=F32)


def _dot_nt(a, b):
    return lax.dot_general(a.astype(BF16), b.astype(BF16), (((1,), (1,)), ((), ())), preferred_element_type=F32)


def _dot_tn(a, b):
    return jnp.dot(a.T.astype(BF16), b.astype(BF16), preferred_element_type=F32)


def _split3(x):
    hi = x.astype(BF16)
    r = x - hi.astype(F32)
    mid = r.astype(BF16)
    lo = (r - mid.astype(F32)).astype(BF16)
    return hi, mid, lo


def _mask_dot(mask_bf, x):
    return sum(jnp.dot(mask_bf, part, preferred_element_type=F32) for part in _split3(x))


def _dot_mask(x, mask_bf):
    return sum(jnp.dot(part, mask_bf, preferred_element_type=F32) for part in _split3(x))


def _head_rmsnorm(o, w):
    return o * lax.rsqrt(jnp.mean(o * o, -1, keepdims=True) + HEAD_NORM_EPS) * w


def _head_layernorm(o, w):
    c = o - jnp.mean(o, -1, keepdims=True)
    return c * lax.rsqrt(jnp.mean(c * c, -1, keepdims=True) + HEAD_NORM_EPS) * w


def _level_masks(c):
    t = np.arange(c)[:, None]
    s = np.arange(c)[None, :]
    masks = [t == s]
    blk = 2
    while blk <= c:
        masks.append((t // blk == s // blk) & (t % blk >= blk // 2) & (s % blk < blk // 2))
        blk *= 2
    return np.stack(masks).astype(np.float32)


def _retention_tables(c):
    idx = np.arange(c, dtype=np.float64)
    lg = np.log(1.0 - 2.0 ** (-5.0 - np.arange(N_HEADS, dtype=np.float64)))[:, None, None]
    rel = idx[:, None] - idx[None, :]
    dm = np.where(rel >= 0, np.exp(np.maximum(rel, 0.0)[None] * lg), 0.0)
    q_in = np.broadcast_to(np.exp((idx + 1.0)[None, :, None] * lg), (N_HEADS, c, HEAD_DIM))
    k_out = np.broadcast_to(np.exp((c - 1.0 - idx)[None, :, None] * lg), (N_HEADS, c, HEAD_DIM))
    g_chunk = tuple(float(g) for g in np.exp(c * lg[:, 0, 0]))
    return dm.astype(np.float32), q_in.astype(np.float32), k_out.astype(np.float32), g_chunk


def _level_ref(G, blk, row):
    c, d = G.shape
    if blk == 2:
        return jnp.where((row & 1) == 1, pltpu.roll(G, 1, 0), G)
    if blk == 4:
        r = row & 3
        return jnp.where(r == 0, pltpu.roll(G, c - 1, 0),
                         jnp.where(r == 1, G, jnp.where(r == 2, pltpu.roll(G, 1, 0), pltpu.roll(G, 2, 0))))
    mid = blk // 2 - 1
    G3 = G.reshape(c // blk, blk, d)
    return jnp.broadcast_to(G3[:, mid:mid + 1, :], (c // blk, blk, d)).reshape(c, d)


def _seq_tables(seq, n_seq):
    rows = seq * n_seq
    t = np.arange(rows)[:, None]
    s = np.arange(rows)[None, :]
    same = (t // seq) == (s // seq)
    tri = np.stack([same & (t >= s), same & (t <= s)]).astype(np.float32)
    levels = _level_masks(seq)
    lvl = np.zeros((levels.shape[0], rows, rows), np.float32)
    dm1, q_in1, k_out1, g_chunk = _retention_tables(seq)
    dm = np.zeros((N_HEADS, rows, rows), np.float32)
    for b in range(n_seq):
        sl = slice(b * seq, (b + 1) * seq)
        lvl[:, sl, sl] = levels
        dm[:, sl, sl] = dm1
    q_in = np.tile(q_in1, (1, n_seq, 1))
    k_out = np.tile(k_out1, (1, n_seq, 1))
    return [lvl, tri, dm, q_in, k_out], g_chunk


def _mixer_kernel(*refs, tt, seq, n_seq, n_alias, g_chunk):
    (x_ref, nw_ref, w_ref, cos_ref, sin_ref, lvl_ref, tri_ref, dm_ref, qin_ref, kout_ref,
     sh0_ref, sr0_ref, c0_ref, n0_ref, m0_ref, cb0_ref,
     prm_ref, convw_ref, mq_ref, mk_ref, mv_ref, wg_ref, wgt_ref, bg_ref, bgt_ref) = refs[:25]
    o_ref, sh_ref, sr_ref, c_ref, n_ref, m_ref, cb_ref = refs[25 + n_alias:32 + n_alias]
    z_refs = refs[32 + n_alias:-1]
    ext_ref = refs[-1]
    ti = pl.program_id(1)
    d, W, H = HEAD_DIM, MIX_WIDTH, N_HEADS
    bb = sh_ref.shape[0]
    n_lvl = lvl_ref.shape[0]
    R = seq * n_seq

    @pl.when(ti == 0)
    def _():
        sh_ref[...] = sh0_ref[...]
        sr_ref[...] = sr0_ref[...]
        c_ref[...] = c0_ref[...]
        n_ref[...] = n0_ref[...]
        m_ref[...] = m0_ref[...]
        cb_ref[...] = cb0_ref[...]

    hn = _rms(x_ref[...], nw_ref[...]).astype(BF16)
    for j, zj_ref in enumerate(z_refs):
        zj_ref[...] = jnp.dot(hn, w_ref[:, j * W:(j + 1) * W], preferred_element_type=F32)

    def prm(i, h):
        return prm_ref[i:i + 1, h * d:(h + 1) * d]

    tri_l = tri_ref[0].astype(BF16)
    tri_u = tri_ref[1].astype(BF16)
    causal = tri_ref[0] > 0.0
    row = lax.broadcasted_iota(jnp.int32, (R, d), 0)

    def sl(b):
        return slice(b * seq, (b + 1) * seq)

    def per_seq(fn):
        parts = [fn(b) for b in range(n_seq)]
        return parts[0] if n_seq == 1 else jnp.concatenate(parts, axis=0)

    def last_rows(a):
        n = a.shape[1]
        a3 = a.reshape(n_seq, seq, n)
        return jnp.broadcast_to(a3[:, seq - 1:seq, :], (n_seq, seq, n)).reshape(R, n)

    def chunk(c0):
        t0 = c0 if n_seq == 1 else 0
        rows = slice(c0, c0 + R)
        cos = cos_ref[rows, :]
        sin = sin_ref[rows, :]

        def rotary(a_):
            return a_ * cos + pltpu.roll(a_, d // 2, 1) * sin

        def zc(j, h):
            return z_refs[j][rows, h * d:(h + 1) * d]

        def put(j, h, val):
            cols = slice(j * W + h * d, j * W + (h + 1) * d)
            if n_seq == 1:
                o_ref[0, t0:t0 + seq, cols] = val
            else:
                for b in range(n_seq):
                    o_ref[b, :, cols] = val[sl(b)]

        fpre = z_refs[1][rows, :]
        e = jnp.exp(-jnp.abs(fpre))
        a = prm_ref[0:1, :]
        b_ = prm_ref[1:2, :] + (jnp.minimum(fpre, 0.0) - jnp.log(1.0 + e))
        logf = jnp.maximum(a, b_) + jnp.log(1.0 + jnp.exp(-jnp.abs(a - b_)))
        k_all = prm_ref[2:3, :] * (jnp.where(fpre >= 0.0, e, 1.0) / (1.0 + e))
        G_all = _mask_dot(tri_l, logf)
        for h in range(H):
            hs = slice(h * d, (h + 1) * d)
            q, v, gate = zc(0, h), zc(2, h), zc(3, h)
            k, G = k_all[:, hs], G_all[:, hs]
            q_bf, k_bf = q.astype(BF16), k.astype(BF16)
            A = lvl_ref[0] * _dot_nt(q_bf, k_bf)
            for j in range(1, n_lvl):
                E = jnp.exp(-jnp.abs(G - _level_ref(G, 1 << j, row))).astype(BF16)
                A = A + lvl_ref[j] * _dot_nt(q_bf * E, k_bf * E)
            qg = q * jnp.exp(G)
            G_last = last_rows(G)
            kd = k * jnp.exp(G_last - G)
            o_h = _dot(A, v) + per_seq(lambda b: _dot(qg[sl(b)], sh_ref[b, h]))
            for b in range(n_seq):
                decay = jnp.exp(G_last[b * seq:b * seq + SUBLANES]).T[:, 0:1]
                sh_ref[b, h] = decay * sh_ref[b, h] + _dot_tn(kd[sl(b)], v[sl(b)])
            put(0, h, _head_rmsnorm(o_h, prm(3, h)) * _silu(gate))

        for h in range(H):
            q = rotary(zc(4, h))
            k = rotary(zc(5, h)) * (d ** -0.5)
            v, gate = zc(6, h), zc(7, h)
            A = _dot_nt(q, k) * dm_ref[h]
            o_h = _dot(A, v) + qin_ref[h] * per_seq(lambda b: _dot(q[sl(b)], sr_ref[b, h]))
            ko = k * kout_ref[h]
            for b in range(n_seq):
                sr_ref[b, h] = g_chunk[h] * sr_ref[b, h] + _dot_tn(ko[sl(b)], v[sl(b)])
            put(1, h, _head_layernorm(o_h, prm(4, h)) * _silu(gate))

        u = z_refs[8][rows, :]
        ext_rows = SUBLANES + seq
        for b in range(n_seq):
            base = b * ext_rows
            ext_ref[base + SUBLANES - (CONV_W - 1):base + SUBLANES, :] = cb_ref[b]
            ext_ref[base + SUBLANES:base + ext_rows, :] = u[sl(b)]
        conv = jnp.zeros((R, W), F32)
        for j in range(CONV_W):
            off = SUBLANES - (CONV_W - 1) + j
            conv = conv + convw_ref[j:j + 1, :] * per_seq(
                lambda b: ext_ref[b * ext_rows + off:b * ext_rows + off + seq, :])
        for b in range(n_seq):
            cb_ref[b] = ext_ref[(b + 1) * ext_rows - (CONV_W - 1):(b + 1) * ext_rows, :]
        uc = _silu(conv + prm_ref[7:8, :])

        qs, ks, vs = [], [], []
        for h in range(H):
            uch = uc[:, h * d:(h + 1) * d].astype(BF16)
            qs.append(jnp.dot(uch, mq_ref[h], preferred_element_type=F32))
            ks.append(jnp.dot(uch, mk_ref[h], preferred_element_type=F32))
            vs.append(jnp.dot(u[:, h * d:(h + 1) * d].astype(BF16), mv_ref[h], preferred_element_type=F32))
        qkv = jnp.concatenate(qs + ks + vs, axis=1).astype(BF16)
        g_col = jnp.dot(qkv, wg_ref[...], preferred_element_type=F32) + bg_ref[...]
        g_row = lax.dot_general(wgt_ref[...], qkv, (((1,), (1,)), ((), ())),
                                preferred_element_type=F32) + bgt_ref[...]
        i_cols = g_col[:, :H]
        f_cols = _mask_dot(tri_l, _log_sigmoid(g_col[:, H:]))
        i_rows = g_row[:H]
        f_rows = _dot_mask(_log_sigmoid(g_row[H:]), tri_u)
        m_prev_rows = per_seq(lambda b: jnp.broadcast_to(m_ref[b], (seq, H)))
        F_last = last_rows(f_cols)
        for h in range(H):
            q, k, v = qs[h], ks[h] * (d ** -0.5), vs[h]
            i_col, F_col = i_cols[:, h:h + 1], f_cols[:, h:h + 1]
            a_row = i_rows[h:h + 1] - f_rows[h:h + 1]
            m_prev = m_prev_rows[:, h:h + 1]
            cm = jnp.max(jnp.where(causal, a_row, NEG_INF), -1, keepdims=True)
            m_col = F_col + jnp.maximum(m_prev, cm)
            logd = (F_col - m_col) + a_row
            dmat = jnp.where(causal, jnp.exp(jnp.where(causal, logd, 0.0)), 0.0)
            Sc = _dot_nt(q, k) * dmat
            inter = jnp.exp(F_col + m_prev - m_col)
            n_rows = jnp.broadcast_to(n_ref[:, h:h + 1, :], (n_seq, seq, d)).reshape(R, d)
            num = _dot(Sc, v) + inter * per_seq(lambda b: _dot(q[sl(b)], c_ref[b, h]))
            den = jnp.sum(Sc, -1, keepdims=True) + inter * jnp.sum(q * n_rows, -1, keepdims=True)
            hout = num / jnp.maximum(jnp.abs(den), jnp.exp(-m_col))
            mL = last_rows(jnp.broadcast_to(m_col, (R, d)))
            FL = jnp.broadcast_to(F_last[:, h:h + 1], (R, d))
            kw = k * jnp.exp((FL - mL) + (i_col - F_col))
            decay_rows = jnp.exp((FL - mL) + m_prev)
            for b in range(n_seq):
                decay = decay_rows[b * seq:b * seq + 1]
                c_ref[b, h] = decay * c_ref[b, h] + _dot_tn(kw[sl(b)], v[sl(b)])
                n_ref[b, h:h + 1, :] = decay * n_ref[b, h:h + 1, :] + jnp.sum(kw[sl(b)], 0, keepdims=True)
                m_ref[b, :, h:h + 1] = mL[b * seq:b * seq + 1, 0:1]
            y = _head_layernorm(hout, prm(5, h)) + prm(6, h) * uc[:, h * d:(h + 1) * d]
            put(2, h, y * _silu(zc(9, h)))

    for c in range(bb * tt // R):
        chunk(c * R)


def _mixers(x, x_row0, B, T, cos, sin, states_in, l, prev_out, p):
    D = x.shape[1]
    depth = states_in[0].shape[0]
    H, d, W = N_HEADS, HEAD_DIM, MIX_WIDTH
    tt = _pick(T, (256, 128, 64, 32, 16, 8))
    n_t = T // tt
    if tt >= 128:
        bb, seq, n_seq = 1, 128, 1
    else:
        assert n_t == 1
        bb = _pick(B, tuple(c for c in (8, 4, 2) if c * tt <= 128) + (1,))
        seq, n_seq = tt, bb
        cos, sin = jnp.tile(cos, (bb, 1)), jnp.tile(sin, (bb, 1))
    assert x_row0 % (bb * tt) == 0
    xb = x_row0 // (bb * tt)
    R = seq * n_seq
    tables, g_chunk = _seq_tables(seq, n_seq)
    consts = [jnp.asarray(t) for t in tables]
    st_spec = pl.BlockSpec((None, bb, H, d, d), lambda b, t: (l, b, 0, 0, 0))
    n_spec = pl.BlockSpec((None, bb, H, d), lambda b, t: (l, b, 0, 0))
    m_spec = pl.BlockSpec((None, bb, 1, H), lambda b, t: (l, b, 0, 0))
    cb_spec = pl.BlockSpec((None, bb, CONV_W - 1, W), lambda b, t: (l, b, 0, 0))
    state_specs = [st_spec, st_spec, st_spec, n_spec, m_spec, cb_spec]

    def full(a):
        nd = a.ndim
        return pl.BlockSpec(a.shape, lambda b, t: (0,) * nd)

    def layer(a):
        nd = a.ndim - 1
        return pl.BlockSpec((None,) + a.shape[1:], lambda b, t: (l,) + (0,) * nd)

    weights = [p['prm'], p['conv_w'], p['mq'], p['mk'], p['mv'], p['m_wg'], p['m_wgt'], p['m_bg'], p['m_bgt']]
    alias_in = list(prev_out) if prev_out is not None else []
    n_fixed = 5 + len(consts) + 6 + len(weights)
    state_shapes = [jax.ShapeDtypeStruct((depth, B, H, d, d), F32)] * 3 + [
        jax.ShapeDtypeStruct((depth, B, H, d), F32),
        jax.ShapeDtypeStruct((depth, B, 1, H), F32),
        jax.ShapeDtypeStruct((depth, B, CONV_W - 1, W), F32)]
    rows_spec = pl.BlockSpec((bb * tt, d), lambda b, t: (t, 0))
    outs = pl.pallas_call(
        functools.partial(_mixer_kernel, tt=tt, seq=seq, n_seq=n_seq, n_alias=len(alias_in),
                          g_chunk=g_chunk),
        grid=(B // bb, n_t),
        in_specs=[pl.BlockSpec((bb * tt, D), lambda b, t: (xb + b * n_t + t, 0)),
                  pl.BlockSpec((None, 1, D), lambda b, t: (l, 0, 0)),
                  pl.BlockSpec((None, D, SCAN_COLS), lambda b, t: (l, 0, 0), pipeline_mode=pl.Buffered(1)),
                  rows_spec, rows_spec]
                 + [full(c) for c in consts] + state_specs + [layer(w) for w in weights]
                 + [pl.BlockSpec(memory_space=pl.ANY)] * len(alias_in),
        out_specs=[pl.BlockSpec((bb, tt, N_BRANCH * W), lambda b, t: (b, t, 0))] + state_specs,
        out_shape=[jax.ShapeDtypeStruct((B, T, N_BRANCH * W), F32)] + state_shapes,
        input_output_aliases={n_fixed + i: 1 + i for i in range(len(alias_in))},
        scratch_shapes=[pltpu.VMEM((bb * tt, W), F32)] * (SCAN_COLS // W)
                       + [pltpu.VMEM((n_seq * (SUBLANES + seq), W), F32)],
        name="mixers",
        compiler_params=_cparams(("parallel", "arbitrary")),
    )(x, p['norm_mix'], p['w_in'], cos, sin, *consts, *states_in, *weights, *alias_in)
    return outs[0], tuple(outs[1:])


def _merge_kernel(x_ref, o_ref, nw_ref, wg0_ref, wg1_ref, wg2_ref, wbr_ref, wout_ref, *rest):
    y_ref = rest[-1]
    x = x_ref[...]
    W = MIX_WIDTH
    hn = _rms(x, nw_ref[...]).astype(BF16)
    merged = jnp.zeros(x.shape, F32)
    for n, wg_ref in enumerate((wg0_ref, wg1_ref, wg2_ref)):
        gz = jnp.dot(hn, wg_ref[...], preferred_element_type=F32)
        proj = jnp.dot(o_ref[:, n * W:(n + 1) * W].astype(BF16), wbr_ref[n], preferred_element_type=F32)
        merged = merged + _sigmoid(gz) * proj
    y_ref[...] = x + jnp.dot(merged.astype(BF16), wout_ref[...], preferred_element_type=F32)


def _merge(x, x_row0, o, nw, w_in, wbr, wout, l, out, out_row0, out_rows):
    n = o.shape[0]
    d = x.shape[1]
    tm = _pick(n, (512, 256, 128))
    assert x_row0 % tm == 0 and out_row0 % tm == 0
    xb, ob = x_row0 // tm, out_row0 // tm
    g0 = SCAN_COLS // d

    def gate_spec(k):
        return pl.BlockSpec((None, d, d), lambda i: (l, 0, g0 + k))

    alias_in = [] if out is None else [out]
    return pl.pallas_call(
        _merge_kernel,
        grid=(n // tm,),
        in_specs=[pl.BlockSpec((tm, d), lambda i: (xb + i, 0)),
                  pl.BlockSpec((tm, N_BRANCH * MIX_WIDTH), lambda i: (i, 0)),
                  pl.BlockSpec((None, 1, d), lambda i: (l, 0, 0)),
                  gate_spec(0), gate_spec(1), gate_spec(2),
                  pl.BlockSpec((None,) + wbr.shape[1:], lambda i: (l, 0, 0, 0)),
                  pl.BlockSpec((None,) + wout.shape[1:], lambda i: (l, 0, 0))]
                 + [pl.BlockSpec(memory_space=pl.ANY)] * len(alias_in),
        out_specs=pl.BlockSpec((tm, d), lambda i: (ob + i, 0)),
        out_shape=jax.ShapeDtypeStruct((out_rows, d), F32),
        input_output_aliases={8: 0} if alias_in else {},
        name="merge",
        compiler_params=_cparams(("parallel",)),
    )(x, o, nw, w_in, w_in, w_in, wbr, wout, *alias_in)


def _router_kernel(x_ref, nw_ref, wr_ref, br_ref, hn_ref, mi_ref, mf_ref, cnt_ref, carry_ref):
    i = pl.program_id(0)
    tm = x_ref.shape[0]
    n_sub = x_ref.shape[1] // LANES

    @pl.when(i == 0)
    def _():
        carry_ref[...] = jnp.zeros(carry_ref.shape, F32)

    hn = _rms(x_ref[...], nw_ref[...])
    for s in range(n_sub):
        hn_ref[pl.ds(s, tm, stride=n_sub), :] = hn[:, s * LANES:(s + 1) * LANES]
    logits = _dot(hn, wr_ref[...]) + br_ref[...]
    lane = lax.broadcasted_iota(jnp.int32, logits.shape, 1)
    big = jnp.int32(1 << 20)

    def first_max(mask):
        vmax = jnp.max(jnp.where(mask, logits, NEG_INF), -1, keepdims=True)
        imax = jnp.min(jnp.where(mask, jnp.where(logits == vmax, lane, big), big), -1, keepdims=True)
        return vmax, imax

    in_groups = lane < N_GROUPS
    g_max, g_top = first_max(in_groups)
    g_w = 1.0 / jnp.sum(jnp.where(in_groups, jnp.exp(logits - g_max), 0.0), -1, keepdims=True)
    e_lo = N_GROUPS + EXPERTS_PER_GROUP * g_top
    in_e = jnp.logical_and(lane >= e_lo, lane < e_lo + EXPERTS_PER_GROUP)
    v1, i1 = first_max(in_e)
    v2, i2 = first_max(jnp.logical_and(in_e, lane != i1))
    e2 = jnp.exp(v2 - v1)
    w1 = g_w / (1.0 + e2)
    w2 = g_w * e2 / (1.0 + e2)

    hit1 = lane == i1
    hit2 = lane == i2
    onehot = jnp.where(hit1, 1.0, 0.0) + jnp.where(hit2, 1.0, 0.0)
    strict = (lax.broadcasted_iota(jnp.int32, (tm, tm), 0) > lax.broadcasted_iota(jnp.int32, (tm, tm), 1))
    before = _dot(strict.astype(F32), onehot) + carry_ref[...]
    r1 = jnp.sum(jnp.where(hit1, before, 0.0), -1, keepdims=True).astype(jnp.int32)
    r2 = jnp.sum(jnp.where(hit2, before, 0.0), -1, keepdims=True).astype(jnp.int32)
    carry_ref[...] = carry_ref[...] + jnp.sum(onehot, 0, keepdims=True)
    cnt_ref[...] = carry_ref[...]

    meta = jnp.where(lane == 0, i1 - N_GROUPS,
                     jnp.where(lane == 1, i2 - N_GROUPS,
                               jnp.where(lane == 2, r1, jnp.where(lane == 3, r2, 0))))
    mi_ref[...] = meta.T[:SUBLANES]
    mf_ref[...] = jnp.where(lane == 0, w1, jnp.where(lane == 1, w2, 0.0))


def _router(x, nw, wr, br, l):
    n, d = x.shape
    n_sub = d // LANES
    tm = _pick(n, (512, 256, 128))
    return pl.pallas_call(
        _router_kernel,
        grid=(n // tm,),
        in_specs=[pl.BlockSpec((tm, d), lambda i: (i, 0)),
                  pl.BlockSpec((None, 1, d), lambda i: (l, 0, 0)),
                  pl.BlockSpec((None, d, LANES), lambda i: (l, 0, 0)),
                  pl.BlockSpec((None, 1, LANES), lambda i: (l, 0, 0))],
        out_specs=[pl.BlockSpec((tm * n_sub, LANES), lambda i: (i, 0)),
                   pl.BlockSpec((SUBLANES, tm), lambda i: (0, i)),
                   pl.BlockSpec((tm, LANES), lambda i: (i, 0)),
                   pl.BlockSpec((1, LANES), lambda i: (0, 0))],
        out_shape=[jax.ShapeDtypeStruct((n * n_sub, LANES), F32),
                   jax.ShapeDtypeStruct((SUBLANES, n), jnp.int32),
                   jax.ShapeDtypeStruct((n, LANES), F32),
                   jax.ShapeDtypeStruct((1, LANES), F32)],
        scratch_shapes=[pltpu.VMEM((1, LANES), F32)],
        name="router",
        compiler_params=_cparams(("arbitrary",)),
    )(x, nw, wr, br)


def _slots_kernel(d0_ref, d1_ref, pair_ref):
    def body(t, carry):
        pair_ref[d0_ref[t]] = TOP_K * t
        pair_ref[d1_ref[t]] = TOP_K * t + 1
        return carry

    lax.fori_loop(0, d0_ref.shape[0], body, 0, unroll=16)


def _slots(d0, d1, cap):
    return pl.pallas_call(
        _slots_kernel,
        grid_spec=pltpu.PrefetchScalarGridSpec(
            num_scalar_prefetch=2,
            grid=(1,),
            in_specs=[],
            out_specs=pl.BlockSpec(memory_space=pltpu.SMEM)),
        out_shape=jax.ShapeDtypeStruct((cap,), jnp.int32),
        name="slots",
        compiler_params=pltpu.CompilerParams(dimension_semantics=("arbitrary",)),
    )(d0, d1)


def _experts_kernel(be_ref, nv_ref, pair_ref, hn_ref, wg_ref, wu_ref, wd_ref, y2_ref,
                    xbuf, ybuf, gsem, ssem, wg_bf, wu_bf, wd_bf, *, n_sub):
    i = pl.program_id(0)
    n_i = pl.num_programs(0)
    slot = i % 2

    def row(r):
        return pl.ds(pl.multiple_of(r * n_sub, n_sub), n_sub)

    def gather(blk, sl, r):
        tok = jnp.right_shift(pair_ref[blk * MOE_ROWS + r], TOP_K.bit_length() - 1)
        return pltpu.make_async_copy(hn_ref.at[row(tok)], xbuf.at[sl, row(r)], gsem.at[sl])

    def scatter(blk, sl, r):
        return pltpu.make_async_copy(ybuf.at[sl, row(r)], y2_ref.at[row(pair_ref[blk * MOE_ROWS + r])], ssem.at[sl])

    def start_valid(copy, blk, sl):
        n_valid = nv_ref[blk]
        n_full = n_valid // ISSUE_UNROLL

        def body(j, carry):
            for k in range(ISSUE_UNROLL):
                copy(blk, sl, j * ISSUE_UNROLL + k).start()
            return carry

        def tail(r, carry):
            copy(blk, sl, r).start()
            return carry

        lax.fori_loop(0, n_full, body, 0)
        lax.fori_loop(n_full * ISSUE_UNROLL, n_valid, tail, 0)

    def wait_valid(buf, blk, sl):
        n_valid = nv_ref[blk]

        @pl.when(n_valid > 0)
        def _():
            rows = pl.ds(0, n_valid * n_sub)
            if buf is xbuf:
                pltpu.make_async_copy(hn_ref.at[rows], xbuf.at[sl, rows], gsem.at[sl]).wait()
            else:
                pltpu.make_async_copy(ybuf.at[sl, rows], y2_ref.at[rows], ssem.at[sl]).wait()

    @pl.when(i == 0)
    def _():
        xbuf[...] = jnp.zeros(xbuf.shape, F32)
        start_valid(gather, 0, 0)

    @pl.when(i + 1 < n_i)
    def _():
        start_valid(gather, i + 1, 1 - slot)

    wait_valid(xbuf, i, slot)

    @pl.when(i >= 2)
    def _():
        wait_valid(ybuf, i - 2, slot)

    @pl.when(nv_ref[i] > 0)
    def _():
        changed = jnp.logical_or(i == 0, be_ref[i] != be_ref[jnp.maximum(i - 1, 0)])

        @pl.when(changed)
        def _():
            wg_bf[...] = wg_ref[...].astype(BF16)
            wu_bf[...] = wu_ref[...].astype(BF16)
            wd_bf[...] = wd_ref[...].astype(BF16)

        x = jnp.concatenate([xbuf[slot, pl.ds(s, MOE_ROWS, stride=n_sub), :] for s in range(n_sub)], axis=1)
        x = x.astype(BF16)
        g = jnp.dot(x, wg_bf[...], preferred_element_type=F32)
        u = jnp.dot(x, wu_bf[...], preferred_element_type=F32)
        y = jnp.dot((_silu(g) * u).astype(BF16), wd_bf[...], preferred_element_type=F32)
        for s in range(n_sub):
            ybuf[slot, pl.ds(s, MOE_ROWS, stride=n_sub), :] = y[:, s * LANES:(s + 1) * LANES]

    start_valid(scatter, i, slot)

    @pl.when(i == n_i - 1)
    def _():
        @pl.when(i >= 1)
        def _():
            wait_valid(ybuf, i - 1, 1 - slot)
        wait_valid(ybuf, i, slot)


def _experts(block_e, n_valid, pairs, hn, w_g, w_u, w_d, l):
    _, _, d, f = w_g.shape
    n_sub = d // LANES
    n = hn.shape[0] // n_sub
    n_blocks = block_e.shape[0]
    buf = pltpu.VMEM((2, MOE_ROWS * n_sub, LANES), F32)
    return pl.pallas_call(
        functools.partial(_experts_kernel, n_sub=n_sub),
        grid_spec=pltpu.PrefetchScalarGridSpec(
            num_scalar_prefetch=3,
            grid=(n_blocks,),
            in_specs=[pl.BlockSpec(memory_space=pl.ANY),
                      pl.BlockSpec((None, None, d, f), lambda i, be, nv, pr: (l, be[i], 0, 0)),
                      pl.BlockSpec((None, None, d, f), lambda i, be, nv, pr: (l, be[i], 0, 0)),
                      pl.BlockSpec((None, None, f, d), lambda i, be, nv, pr: (l, be[i], 0, 0))],
            out_specs=pl.BlockSpec(memory_space=pl.ANY),
            scratch_shapes=[buf, buf, pltpu.SemaphoreType.DMA((2,)), pltpu.SemaphoreType.DMA((2,)),
                            pltpu.VMEM((d, f), BF16), pltpu.VMEM((d, f), BF16), pltpu.VMEM((f, d), BF16)]),
        out_shape=jax.ShapeDtypeStruct((n * TOP_K * n_sub, LANES), F32),
        name="experts",
        compiler_params=pltpu.CompilerParams(dimension_semantics=("arbitrary",), vmem_limit_bytes=VMEM_LIMIT,
                                             has_side_effects=True),
    )(block_e, n_valid, pairs, hn, w_g, w_u, w_d)


def _combine_kernel(x_ref, mf_ref, nw_ref, y2_ref, y_ref, *, final_norm):
    tm = x_ref.shape[0]
    n_sub = x_ref.shape[1] // LANES
    w = mf_ref[...]
    stride = TOP_K * n_sub
    y0 = jnp.concatenate([y2_ref[pl.ds(s, tm, stride=stride), :] for s in range(n_sub)], axis=1)
    y1 = jnp.concatenate([y2_ref[pl.ds(n_sub + s, tm, stride=stride), :] for s in range(n_sub)], axis=1)
    out = x_ref[...] + (y0 * w[:, 0:1] + y1 * w[:, 1:2])
    if final_norm:
        out = _rms(out, nw_ref[...])
    y_ref[...] = out


def _combine(x, mf, nw, y2, final_norm, row0, n):
    d = x.shape[1]
    n_sub = d // LANES
    tm = _pick(n, (512, 256, 128))
    assert row0 % tm == 0
    rb = row0 // tm
    return pl.pallas_call(
        functools.partial(_combine_kernel, final_norm=final_norm),
        grid=(n // tm,),
        in_specs=[pl.BlockSpec((tm, d), lambda i: (rb + i, 0)),
                  pl.BlockSpec((tm, LANES), lambda i: (rb + i, 0)),
                  pl.BlockSpec((1, d), lambda i: (0, 0)),
                  pl.BlockSpec((tm * TOP_K * n_sub, LANES), lambda i: (rb + i, 0))],
        out_specs=pl.BlockSpec((tm, d), lambda i: (i, 0)),
        out_shape=jax.ShapeDtypeStruct((n, d), F32),
        name="combine",
        compiler_params=_cparams(("parallel",)),
    )(x, mf, nw, y2)


def _moe(x, l, p, final_nw, out_ranges):
    n, d = x.shape
    n_sub = d // LANES
    hn, mi, mf, cnt = _router(x, p['norm_ffn'], p['w_r'], p['b_r'], l)
    counts = cnt[0, N_GROUPS:N_GROUPS + N_EXPERTS].astype(jnp.int32)
    padded = (counts + MOE_ROWS - 1) // MOE_ROWS * MOE_ROWS
    pad_end = jnp.cumsum(padded)
    pad_start = pad_end - padded
    experts = jnp.arange(N_EXPERTS, dtype=jnp.int32)

    def lookup(table, e):
        return jnp.sum(jnp.where(e[:, None] == experts[None, :], table[None, :], 0), axis=1)

    d0 = mi[2] + lookup(pad_start, mi[0])
    d1 = mi[3] + lookup(pad_start, mi[1])
    n_blocks = -(-(n * TOP_K) // MOE_ROWS) + N_EXPERTS
    block_row = jnp.arange(n_blocks, dtype=jnp.int32) * MOE_ROWS
    block_e = jnp.minimum(jnp.sum((pad_end[None, :] <= block_row[:, None]).astype(jnp.int32), axis=1), N_EXPERTS - 1)
    used_end = lookup(pad_start + counts, block_e)
    n_valid = jnp.clip(used_end - block_row, 0, MOE_ROWS).astype(jnp.int32)
    pairs = _slots(d0, d1, n_blocks * MOE_ROWS)
    y2 = _experts(block_e.astype(jnp.int32), n_valid, pairs, hn, p['w_eg'], p['w_eu'], p['w_ed'], l)
    nw = (final_nw if final_nw is not None else p['norm_ffn'][l, 0])[None]
    return [_combine(x, mf, nw, y2, final_nw is not None, row0, rows) for row0, rows in out_ranges]


def _rotary_tables(pos):
    half = HEAD_DIM // 2
    inv = ROPE_BASE ** (-jnp.arange(half, dtype=F32) / half)
    ang = pos.astype(F32)[:, None] * inv[None, :]
    cos = jnp.concatenate([jnp.cos(ang), jnp.cos(ang)], -1)
    sin = jnp.concatenate([-jnp.sin(ang), jnp.sin(ang)], -1)
    return cos, sin


def _trunks(xs, positions, states, p):
    depth = p['w_in'].shape[0]
    D = xs[0].shape[-1]
    shapes = [x.shape[:2] for x in xs]
    n_rows = [B * T for B, T in shapes]
    row0 = [sum(n_rows[:i]) for i in range(len(xs))]
    total = sum(n_rows)
    tables = [_rotary_tables(pos) for pos in positions]
    x_arrays = [x.reshape(n, D) for x, n in zip(xs, n_rows)]
    x_row0 = [0] * len(xs)
    new_states = [None] * len(xs)
    for l in range(depth):
        xm = None
        for i, (B, T) in enumerate(shapes):
            o, new_states[i] = _mixers(x_arrays[i], x_row0[i], B, T, *tables[i], states[i], l, new_states[i], p)
            xm = _merge(x_arrays[i], x_row0[i], o.reshape(n_rows[i], N_BRANCH * MIX_WIDTH), p['norm_mix'], p['w_in'],
                        p['w_branch'], p['w_out'], l, xm, row0[i], total)
        if l == depth - 1:
            outs = _moe(xm, l, p, p['norm_final'], list(zip(row0, n_rows)))
        else:
            x_all, = _moe(xm, l, p, None, [(0, total)])
            x_arrays, x_row0 = [x_all] * len(xs), row0
    results = []
    for y, (B, T), st in zip(outs, shapes, new_states):
        sh, sr, c, n, m, cb = st
        results.append((y.reshape(B, T, D), (sh, sr, c, n, m.reshape(depth, B, N_HEADS), cb)))
    return results


def kernel(x_prompt, x_sample, state_hgrn, state_ret, state_mlstm_C, state_mlstm_n, state_mlstm_m, state_mlstm_conv,
           norm_mix, norm_ffn, norm_final, w_in, hgrn_lb, hgrn_norm, ret_norm, mlstm_conv_w, mlstm_conv_b,
           mlstm_wq, mlstm_wk, mlstm_wv, mlstm_w_gates, mlstm_b_gates, mlstm_norm, mlstm_skip, w_branch, w_out,
           w_router_group, b_router_group, w_router_expert, b_router_expert, w_exp_gate, w_exp_up, w_exp_down):
    depth, D = norm_mix.shape
    H, d, W = N_HEADS, HEAD_DIM, MIX_WIDTH
    lb = jnp.cumsum(jax.nn.softmax(hgrn_lb.astype(F32), axis=0), axis=0)
    lb = lb - lb[0:1]
    prm = jnp.stack([jnp.log(lb), jnp.log1p(-lb), 1.0 - lb, hgrn_norm.astype(F32), ret_norm.astype(F32),
                     mlstm_norm.astype(F32), mlstm_skip.astype(F32), mlstm_conv_b.astype(F32)], axis=1)
    pad = LANES - N_GROUPS - N_EXPERTS
    w_r = jnp.concatenate([w_router_group, w_router_expert, jnp.zeros((depth, D, pad), F32)], -1)
    b_r = jnp.concatenate([b_router_group, b_router_expert, jnp.zeros((depth, pad), F32)], -1)[:, None, :]
    wg_bf = mlstm_w_gates.astype(BF16)
    p = {'norm_mix': norm_mix[:, None, :], 'norm_ffn': norm_ffn[:, None, :], 'norm_final': norm_final,
         'w_in': w_in.astype(BF16), 'prm': prm, 'conv_w': mlstm_conv_w.astype(F32),
         'mq': mlstm_wq.astype(BF16), 'mk': mlstm_wk.astype(BF16), 'mv': mlstm_wv.astype(BF16),
         'm_wg': wg_bf, 'm_wgt': jnp.swapaxes(wg_bf, 1, 2),
         'm_bg': mlstm_b_gates[:, None, :], 'm_bgt': mlstm_b_gates[:, :, None],
         'w_branch': w_branch.astype(BF16), 'w_out': w_out.astype(BF16),
         'w_r': w_r, 'b_r': b_r, 'w_eg': w_exp_gate, 'w_eu': w_exp_up, 'w_ed': w_exp_down}

    Bp, Tp = x_prompt.shape[0], x_prompt.shape[1]
    Bs = x_sample.shape[0]
    zero_states = (jnp.zeros((depth, Bp, H, d, d), F32), jnp.zeros((depth, Bp, H, d, d), F32),
                   jnp.zeros((depth, Bp, H, d, d), F32), jnp.zeros((depth, Bp, H, d), F32),
                   jnp.zeros((depth, Bp, 1, H), F32), jnp.zeros((depth, Bp, CONV_W - 1, W), F32))
    pos_prompt = jnp.arange(Tp, dtype=jnp.int32)
    pos_sample = PAST_LEN + jnp.arange(x_sample.shape[1], dtype=jnp.int32)
    sample_states = (state_hgrn, state_ret, state_mlstm_C, state_mlstm_n,
                     state_mlstm_m.reshape(depth, Bs, 1, H), state_mlstm_conv)
    (y_prompt, ps), (y_sample, ss) = _trunks([x_prompt, x_sample], [pos_prompt, pos_sample],
                                             [zero_states, sample_states], p)
    return (y_prompt, y_sample) + ps + ss
```

```python
import functools

import jax
import jax.numpy as jnp
import numpy as np
from jax import lax
from jax.experimental import pallas as pl
from jax.experimental.pallas import tpu as pltpu

F32 = jnp.float32
BF16 = jnp.bfloat16

HEAD_DIM = 128
N_HEADS = 4
MIX_WIDTH = HEAD_DIM * N_HEADS
N_BRANCH = 3
CONV_W = 4
ROPE_BASE = 10000.0
N_GROUPS = 4
EXPERTS_PER_GROUP = 8
N_EXPERTS = N_GROUPS * EXPERTS_PER_GROUP
TOP_K = 2
PAST_LEN = 16384
NORM_EPS = 1e-6
HEAD_NORM_EPS = 1e-5
SCAN_COLS = 10 * MIX_WIDTH

LANES = 128
SUBLANES = 8
VMEM_LIMIT = 48 * 1024 * 1024
MOE_ROWS = 256
ISSUE_UNROLL = 8
NEG_INF = float("-inf")


def _cparams(sem):
    return pltpu.CompilerParams(dimension_semantics=sem, vmem_limit_bytes=VMEM_LIMIT)


def _pick(n, cands):
    for c in cands:
        if n % c == 0:
            return c
    return n


def _rms(x, w):
    return x * lax.rsqrt(jnp.mean(x * x, -1, keepdims=True) + NORM_EPS) * w


def _sigmoid(x):
    return 1.0 / (1.0 + jnp.exp(-x))


def _silu(x):
    return x * _sigmoid(x)


def _log_sigmoid(x):
    return jnp.minimum(x, 0.0) - jnp.log(1.0 + jnp.exp(-jnp.abs(x)))


def _dot(a, b):
    return jnp.dot(a.astype(BF16), b.astype(BF16), preferred_element_type=F32)


def _dot_nt(a, b):
    return lax.dot_general(a.astype(BF16), b.astype(BF16), (((1,), (1,)), ((), ())), preferred_element_type=F32)


def _dot_tn(a, b):
    return jnp.dot(a.T.astype(BF16), b.astype(BF16), preferred_element_type=F32)


def _split3(x):
    hi = x.astype(BF16)
    r = x - hi.astype(F32)
    mid = r.astype(BF16)
    lo = (r - mid.astype(F32)).astype(BF16)
    return hi, mid, lo


def _mask_dot(mask_bf, x):
    return sum(jnp.dot(mask_bf, part, preferred_element_type=F32) for part in _split3(x))


def _dot_mask(x, mask_bf):
    return sum(jnp.dot(part, mask_bf, preferred_element_type=F32) for part in _split3(x))


def _head_rmsnorm(o, w):
    return o * lax.rsqrt(jnp.mean(o * o, -1, keepdims=True) + HEAD_NORM_EPS) * w


def _head_layernorm(o, w):
    c = o - jnp.mean(o, -1, keepdims=True)
    return c * lax.rsqrt(jnp.mean(c * c, -1, keepdims=True) + HEAD_NORM_EPS) * w


def _level_masks(c):
    t = np.arange(c)[:, None]
    s = np.arange(c)[None, :]
    masks = [t == s]
    blk = 2
    while blk <= c:
        masks.append((t // blk == s // blk) & (t % blk >= blk // 2) & (s % blk < blk // 2))
        blk *= 2
    return np.stack(masks).astype(np.float32)


def _retention_tables(c):
    idx = np.arange(c, dtype=np.float64)
    lg = np.log(1.0 - 2.0 ** (-5.0 - np.arange(N_HEADS, dtype=np.float64)))[:, None, None]
    rel = idx[:, None] - idx[None, :]
    dm = np.where(rel >= 0, np.exp(np.maximum(rel, 0.0)[None] * lg), 0.0)
    q_in = np.broadcast_to(np.exp((idx + 1.0)[None, :, None] * lg), (N_HEADS, c, HEAD_DIM))
    k_out = np.broadcast_to(np.exp((c - 1.0 - idx)[None, :, None] * lg), (N_HEADS, c, HEAD_DIM))
    g_chunk = tuple(float(g) for g in np.exp(c * lg[:, 0, 0]))
    return dm.astype(np.float32), q_in.astype(np.float32), k_out.astype(np.float32), g_chunk


def _level_ref(G, blk, row):
    c, d = G.shape
    if blk == 2:
        return jnp.where((row & 1) == 1, pltpu.roll(G, 1, 0), G)
    if blk == 4:
        r = row & 3
        return jnp.where(r == 0, pltpu.roll(G, c - 1, 0),
                         jnp.where(r == 1, G, jnp.where(r == 2, pltpu.roll(G, 1, 0), pltpu.roll(G, 2, 0))))
    mid = blk // 2 - 1
    G3 = G.reshape(c // blk, blk, d)
    return jnp.broadcast_to(G3[:, mid:mid + 1, :], (c // blk, blk, d)).reshape(c, d)


def _seq_tables(seq, n_seq):
    rows = seq * n_seq
    t = np.arange(rows)[:, None]
    s = np.arange(rows)[None, :]
    same = (t // seq) == (s // seq)
    tri = np.stack([same & (t >= s), same & (t <= s)]).astype(np.float32)
    levels = _level_masks(seq)
    lvl = np.zeros((levels.shape[0], rows, rows), np.float32)
    dm1, q_in1, k_out1, g_chunk = _retention_tables(seq)
    dm = np.zeros((N_HEADS, rows, rows), np.float32)
    for b in range(n_seq):
        sl = slice(b * seq, (b + 1) * seq)
        lvl[:, sl, sl] = levels
        dm[:, sl, sl] = dm1
    q_in = np.tile(q_in1, (1, n_seq, 1))
    k_out = np.tile(k_out1, (1, n_seq, 1))
    return [lvl, tri, dm, q_in, k_out], g_chunk


def _mixer_kernel(*refs, tt, seq, n_seq, n_alias, g_chunk):
    (x_ref, nw_ref, w_ref, cos_ref, sin_ref, lvl_ref, tri_ref, dm_ref, qin_ref, kout_ref,
     sh0_ref, sr0_ref, c0_ref, n0_ref, m0_ref, cb0_ref,
     prm_ref, convw_ref, mq_ref, mk_ref, mv_ref, wg_ref, wgt_ref, bg_ref, bgt_ref) = refs[:25]
    o_ref, sh_ref, sr_ref, c_ref, n_ref, m_ref, cb_ref = refs[25 + n_alias:32 + n_alias]
    z_refs = refs[32 + n_alias:-1]
    ext_ref = refs[-1]
    ti = pl.program_id(1)
    d, W, H = HEAD_DIM, MIX_WIDTH, N_HEADS
    bb = sh_ref.shape[0]
    n_lvl = lvl_ref.shape[0]
    R = seq * n_seq

    @pl.when(ti == 0)
    def _():
        sh_ref[...] = sh0_ref[...]
        sr_ref[...] = sr0_ref[...]
        c_ref[...] = c0_ref[...]
        n_ref[...] = n0_ref[...]
        m_ref[...] = m0_ref[...]
        cb_ref[...] = cb0_ref[...]

    hn = _rms(x_ref[...], nw_ref[...]).astype(BF16)
    for j, zj_ref in enumerate(z_refs):
        zj_ref[...] = jnp.dot(hn, w_ref[:, j * W:(j + 1) * W], preferred_element_type=F32)

    def prm(i, h):
        return prm_ref[i:i + 1, h * d:(h + 1) * d]

    tri_l = tri_ref[0].astype(BF16)
    tri_u = tri_ref[1].astype(BF16)
    causal = tri_ref[0] > 0.0
    row = lax.broadcasted_iota(jnp.int32, (R, d), 0)

    def sl(b):
        return slice(b * seq, (b + 1) * seq)

    def per_seq(fn):
        parts = [fn(b) for b in range(n_seq)]
        return parts[0] if n_seq == 1 else jnp.concatenate(parts, axis=0)

    def last_rows(a):
        n = a.shape[1]
        a3 = a.reshape(n_seq, seq, n)
        return jnp.broadcast_to(a3[:, seq - 1:seq, :], (n_seq, seq, n)).reshape(R, n)

    def chunk(c0):
        t0 = c0 if n_seq == 1 else 0
        rows = slice(c0, c0 + R)
        cos = cos_ref[rows, :]
        sin = sin_ref[rows, :]

        def rotary(a_):
            return a_ * cos + pltpu.roll(a_, d // 2, 1) * sin

        def zc(j, h):
            return z_refs[j][rows, h * d:(h + 1) * d]

        def put(j, h, val):
            cols = slice(j * W + h * d, j * W + (h + 1) * d)
            if n_seq == 1:
                o_ref[0, t0:t0 + seq, cols] = val
            else:
                for b in range(n_seq):
                    o_ref[b, :, cols] = val[sl(b)]

        fpre = z_refs[1][rows, :]
        e = jnp.exp(-jnp.abs(fpre))
        a = prm_ref[0:1, :]
        b_ = prm_ref[1:2, :] + (jnp.minimum(fpre, 0.0) - jnp.log(1.0 + e))
        logf = jnp.maximum(a, b_) + jnp.log(1.0 + jnp.exp(-jnp.abs(a - b_)))
        k_all = prm_ref[2:3, :] * (jnp.where(fpre >= 0.0, e, 1.0) / (1.0 + e))
        G_all = _mask_dot(tri_l, logf)
        for h in range(H):
            hs = slice(h * d, (h + 1) * d)
            q, v, gate = zc(0, h), zc(2, h), zc(3, h)
            k, G = k_all[:, hs], G_all[:, hs]
            q_bf, k_bf = q.astype(BF16), k.astype(BF16)
            A = lvl_ref[0] * _dot_nt(q_bf, k_bf)
            for j in range(1, n_lvl):
                E = jnp.exp(-jnp.abs(G - _level_ref(G, 1 << j, row))).astype(BF16)
                A = A + lvl_ref[j] * _dot_nt(q_bf * E, k_bf * E)
            qg = q * jnp.exp(G)
            G_last = last_rows(G)
            kd = k * jnp.exp(G_last - G)
            o_h = _dot(A, v) + per_seq(lambda b: _dot(qg[sl(b)], sh_ref[b, h]))
            for b in range(n_seq):
                decay = jnp.exp(G_last[b * seq:b * seq + SUBLANES]).T[:, 0:1]
                sh_ref[b, h] = decay * sh_ref[b, h] + _dot_tn(kd[sl(b)], v[sl(b)])
            put(0, h, _head_rmsnorm(o_h, prm(3, h)) * _silu(gate))

        for h in range(H):
            q = rotary(zc(4, h))
            k = rotary(zc(5, h)) * (d ** -0.5)
            v, gate = zc(6, h), zc(7, h)
            A = _dot_nt(q, k) * dm_ref[h]
            o_h = _dot(A, v) + qin_ref[h] * per_seq(lambda b: _dot(q[sl(b)], sr_ref[b, h]))
            ko = k * kout_ref[h]
            for b in range(n_seq):
                sr_ref[b, h] = g_chunk[h] * sr_ref[b, h] + _dot_tn(ko[sl(b)], v[sl(b)])
            put(1, h, _head_layernorm(o_h, prm(4, h)) * _silu(gate))

        u = z_refs[8][rows, :]
        ext_rows = SUBLANES + seq
        for b in range(n_seq):
            base = b * ext_rows
            ext_ref[base + SUBLANES - (CONV_W - 1):base + SUBLANES, :] = cb_ref[b]
            ext_ref[base + SUBLANES:base + ext_rows, :] = u[sl(b)]
        conv = jnp.zeros((R, W), F32)
        for j in range(CONV_W):
            off = SUBLANES - (CONV_W - 1) + j
            conv = conv + convw_ref[j:j + 1, :] * per_seq(
                lambda b: ext_ref[b * ext_rows + off:b * ext_rows + off + seq, :])
        for b in range(n_seq):
            cb_ref[b] = ext_ref[(b + 1) * ext_rows - (CONV_W - 1):(b + 1) * ext_rows, :]
        uc = _silu(conv + prm_ref[7:8, :])

        qs, ks, vs = [], [], []
        for h in range(H):
            uch = uc[:, h * d:(h + 1) * d].astype(BF16)
            qs.append(jnp.dot(uch, mq_ref[h], preferred_element_type=F32))
            ks.append(jnp.dot(uch, mk_ref[h], preferred_element_type=F32))
            vs.append(jnp.dot(u[:, h * d:(h + 1) * d].astype(BF16), mv_ref[h], preferred_element_type=F32))
        qkv = jnp.concatenate(qs + ks + vs, axis=1).astype(BF16)
        g_col = jnp.dot(qkv, wg_ref[...], preferred_element_type=F32) + bg_ref[...]
        g_row = lax.dot_general(wgt_ref[...], qkv, (((1,), (1,)), ((), ())),
                                preferred_element_type=F32) + bgt_ref[...]
        i_cols = g_col[:, :H]
        f_cols = _mask_dot(tri_l, _log_sigmoid(g_col[:, H:]))
        i_rows = g_row[:H]
        f_rows = _dot_mask(_log_sigmoid(g_row[H:]), tri_u)
        m_prev_rows = per_seq(lambda b: jnp.broadcast_to(m_ref[b], (seq, H)))
        F_last = last_rows(f_cols)
        for h in range(H):
            q, k, v = qs[h], ks[h] * (d ** -0.5), vs[h]
            i_col, F_col = i_cols[:, h:h + 1], f_cols[:, h:h + 1]
            a_row = i_rows[h:h + 1] - f_rows[h:h + 1]
            m_prev = m_prev_rows[:, h:h + 1]
            cm = jnp.max(jnp.where(causal, a_row, NEG_INF), -1, keepdims=True)
            m_col = F_col + jnp.maximum(m_prev, cm)
            logd = (F_col - m_col) + a_row
            dmat = jnp.where(causal, jnp.exp(jnp.where(causal, logd, 0.0)), 0.0)
            Sc = _dot_nt(q, k) * dmat
            inter = jnp.exp(F_col + m_prev - m_col)
            n_rows = jnp.broadcast_to(n_ref[:, h:h + 1, :], (n_seq, seq, d)).reshape(R, d)
            num = _dot(Sc, v) + inter * per_seq(lambda b: _dot(q[sl(b)], c_ref[b, h]))
            den = jnp.sum(Sc, -1, keepdims=True) + inter * jnp.sum(q * n_rows, -1, keepdims=True)
            hout = num / jnp.maximum(jnp.abs(den), jnp.exp(-m_col))
            mL = last_rows(jnp.broadcast_to(m_col, (R, d)))
            FL = jnp.broadcast_to(F_last[:, h:h + 1], (R, d))
            kw = k * jnp.exp((FL - mL) + (i_col - F_col))
            decay_rows = jnp.exp((FL - mL) + m_prev)
            for b in range(n_seq):
                decay = decay_rows[b * seq:b * seq + 1]
                c_ref[b, h] = decay * c_ref[b, h] + _dot_tn(kw[sl(b)], v[sl(b)])
                n_ref[b, h:h + 1, :] = decay * n_ref[b, h:h + 1, :] + jnp.sum(kw[sl(b)], 0, keepdims=True)
                m_ref[b, :, h:h + 1] = mL[b * seq:b * seq + 1, 0:1]
            y = _head_layernorm(hout, prm(5, h)) + prm(6, h) * uc[:, h * d:(h + 1) * d]
            put(2, h, y * _silu(zc(9, h)))

    for c in range(bb * tt // R):
        chunk(c * R)


def _mixers(x, x_row0, B, T, cos, sin, states_in, l, prev_out, p):
    D = x.shape[1]
    depth = states_in[0].shape[0]
    H, d, W = N_HEADS, HEAD_DIM, MIX_WIDTH
    tt = _pick(T, (256, 128, 64, 32, 16, 8))
    n_t = T // tt
    if tt >= 128:
        bb, seq, n_seq = 1, 128, 1
    else:
        assert n_t == 1
        bb = _pick(B, tuple(c for c in (8, 4, 2) if c * tt <= 128) + (1,))
        seq, n_seq = tt, bb
        cos, sin = jnp.tile(cos, (bb, 1)), jnp.tile(sin, (bb, 1))
    assert x_row0 % (bb * tt) == 0
    xb = x_row0 // (bb * tt)
    R = seq * n_seq
    tables, g_chunk = _seq_tables(seq, n_seq)
    consts = [jnp.asarray(t) for t in tables]
    st_spec = pl.BlockSpec((None, bb, H, d, d), lambda b, t: (l, b, 0, 0, 0))
    n_spec = pl.BlockSpec((None, bb, H, d), lambda b, t: (l, b, 0, 0))
    m_spec = pl.BlockSpec((None, bb, 1, H), lambda b, t: (l, b, 0, 0))
    cb_spec = pl.BlockSpec((None, bb, CONV_W - 1, W), lambda b, t: (l, b, 0, 0))
    state_specs = [st_spec, st_spec, st_spec, n_spec, m_spec, cb_spec]

    def full(a):
        nd = a.ndim
        return pl.BlockSpec(a.shape, lambda b, t: (0,) * nd)

    def layer(a):
        nd = a.ndim - 1
        return pl.BlockSpec((None,) + a.shape[1:], lambda b, t: (l,) + (0,) * nd)

    weights = [p['prm'], p['conv_w'], p['mq'], p['mk'], p['mv'], p['m_wg'], p['m_wgt'], p['m_bg'], p['m_bgt']]
    alias_in = list(prev_out) if prev_out is not None else []
    n_fixed = 5 + len(consts) + 6 + len(weights)
    state_shapes = [jax.ShapeDtypeStruct((depth, B, H, d, d), F32)] * 3 + [
        jax.ShapeDtypeStruct((depth, B, H, d), F32),
        jax.ShapeDtypeStruct((depth, B, 1, H), F32),
        jax.ShapeDtypeStruct((depth, B, CONV_W - 1, W), F32)]
    rows_spec = pl.BlockSpec((bb * tt, d), lambda b, t: (t, 0))
    outs = pl.pallas_call(
        functools.partial(_mixer_kernel, tt=tt, seq=seq, n_seq=n_seq, n_alias=len(alias_in),
                          g_chunk=g_chunk),
        grid=(B // bb, n_t),
        in_specs=[pl.BlockSpec((bb * tt, D), lambda b, t: (xb + b * n_t + t, 0)),
                  pl.BlockSpec((None, 1, D), lambda b, t: (l, 0, 0)),
                  pl.BlockSpec((None, D, SCAN_COLS), lambda b, t: (l, 0, 0), pipeline_mode=pl.Buffered(1)),
                  rows_spec, rows_spec]
                 + [full(c) for c in consts] + state_specs + [layer(w) for w in weights]
                 + [pl.BlockSpec(memory_space=pl.ANY)] * len(alias_in),
        out_specs=[pl.BlockSpec((bb, tt, N_BRANCH * W), lambda b, t: (b, t, 0))] + state_specs,
        out_shape=[jax.ShapeDtypeStruct((B, T, N_BRANCH * W), F32)] + state_shapes,
        input_output_aliases={n_fixed + i: 1 + i for i in range(len(alias_in))},
        scratch_shapes=[pltpu.VMEM((bb * tt, W), F32)] * (SCAN_COLS // W)
                       + [pltpu.VMEM((n_seq * (SUBLANES + seq), W), F32)],
        name="mixers",
        compiler_params=_cparams(("parallel", "arbitrary")),
    )(x, p['norm_mix'], p['w_in'], cos, sin, *consts, *states_in, *weights, *alias_in)
    return outs[0], tuple(outs[1:])


def _merge_kernel(x_ref, o_ref, nw_ref, wg0_ref, wg1_ref, wg2_ref, wbr_ref, wout_ref, *rest):
    y_ref = rest[-1]
    x = x_ref[...]
    W = MIX_WIDTH
    hn = _rms(x, nw_ref[...]).astype(BF16)
    merged = jnp.zeros(x.shape, F32)
    for n, wg_ref in enumerate((wg0_ref, wg1_ref, wg2_ref)):
        gz = jnp.dot(hn, wg_ref[...], preferred_element_type=F32)
        proj = jnp.dot(o_ref[:, n * W:(n + 1) * W].astype(BF16), wbr_ref[n], preferred_element_type=F32)
        merged = merged + _sigmoid(gz) * proj
    y_ref[...] = x + jnp.dot(merged.astype(BF16), wout_ref[...], preferred_element_type=F32)


def _merge(x, x_row0, o, nw, w_in, wbr, wout, l, out, out_row0, out_rows):
    n = o.shape[0]
    d = x.shape[1]
    tm = _pick(n, (512, 256, 128))
    assert x_row0 % tm == 0 and out_row0 % tm == 0
    xb, ob = x_row0 // tm, out_row0 // tm
    g0 = SCAN_COLS // d

    def gate_spec(k):
        return pl.BlockSpec((None, d, d), lambda i: (l, 0, g0 + k))

    alias_in = [] if out is None else [out]
    return pl.pallas_call(
        _merge_kernel,
        grid=(n // tm,),
        in_specs=[pl.BlockSpec((tm, d), lambda i: (xb + i, 0)),
                  pl.BlockSpec((tm, N_BRANCH * MIX_WIDTH), lambda i: (i, 0)),
                  pl.BlockSpec((None, 1, d), lambda i: (l, 0, 0)),
                  gate_spec(0), gate_spec(1), gate_spec(2),
                  pl.BlockSpec((None,) + wbr.shape[1:], lambda i: (l, 0, 0, 0)),
                  pl.BlockSpec((None,) + wout.shape[1:], lambda i: (l, 0, 0))]
                 + [pl.BlockSpec(memory_space=pl.ANY)] * len(alias_in),
        out_specs=pl.BlockSpec((tm, d), lambda i: (ob + i, 0)),
        out_shape=jax.ShapeDtypeStruct((out_rows, d), F32),
        input_output_aliases={8: 0} if alias_in else {},
        name="merge",
        compiler_params=_cparams(("parallel",)),
    )(x, o, nw, w_in, w_in, w_in, wbr, wout, *alias_in)


def _router_kernel(x_ref, nw_ref, wr_ref, br_ref, hn_ref, mi_ref, mf_ref, cnt_ref, carry_ref):
    i = pl.program_id(0)
    tm = x_ref.shape[0]
    n_sub = x_ref.shape[1] // LANES

    @pl.when(i == 0)
    def _():
        carry_ref[...] = jnp.zeros(carry_ref.shape, F32)

    hn = _rms(x_ref[...], nw_ref[...])
    for s in range(n_sub):
        hn_ref[pl.ds(s, tm, stride=n_sub), :] = hn[:, s * LANES:(s + 1) * LANES]
    logits = _dot(hn, wr_ref[...]) + br_ref[...]
    lane = lax.broadcasted_iota(jnp.int32, logits.shape, 1)
    big = jnp.int32(1 << 20)

    def first_max(mask):
        vmax = jnp.max(jnp.where(mask, logits, NEG_INF), -1, keepdims=True)
        imax = jnp.min(jnp.where(mask, jnp.where(logits == vmax, lane, big), big), -1, keepdims=True)
        return vmax, imax

    in_groups = lane < N_GROUPS
    g_max, g_top = first_max(in_groups)
    g_w = 1.0 / jnp.sum(jnp.where(in_groups, jnp.exp(logits - g_max), 0.0), -1, keepdims=True)
    e_lo = N_GROUPS + EXPERTS_PER_GROUP * g_top
    in_e = jnp.logical_and(lane >= e_lo, lane < e_lo + EXPERTS_PER_GROUP)
    v1, i1 = first_max(in_e)
    v2, i2 = first_max(jnp.logical_and(in_e, lane != i1))
    e2 = jnp.exp(v2 - v1)
    w1 = g_w / (1.0 + e2)
    w2 = g_w * e2 / (1.0 + e2)

    hit1 = lane == i1
    hit2 = lane == i2
    onehot = jnp.where(hit1, 1.0, 0.0) + jnp.where(hit2, 1.0, 0.0)
    strict = (lax.broadcasted_iota(jnp.int32, (tm, tm), 0) > lax.broadcasted_iota(jnp.int32, (tm, tm), 1))
    before = _dot(strict.astype(F32), onehot) + carry_ref[...]
    r1 = jnp.sum(jnp.where(hit1, before, 0.0), -1, keepdims=True).astype(jnp.int32)
    r2 = jnp.sum(jnp.where(hit2, before, 0.0), -1, keepdims=True).astype(jnp.int32)
    carry_ref[...] = carry_ref[...] + jnp.sum(onehot, 0, keepdims=True)
    cnt_ref[...] = carry_ref[...]

    meta = jnp.where(lane == 0, i1 - N_GROUPS,
                     jnp.where(lane == 1, i2 - N_GROUPS,
                               jnp.where(lane == 2, r1, jnp.where(lane == 3, r2, 0))))
    mi_ref[...] = meta.T[:SUBLANES]
    mf_ref[...] = jnp.where(lane == 0, w1, jnp.where(lane == 1, w2, 0.0))


def _router(x, nw, wr, br, l):
    n, d = x.shape
    n_sub = d // LANES
    tm = _pick(n, (512, 256, 128))
    return pl.pallas_call(
        _router_kernel,
        grid=(n // tm,),
        in_specs=[pl.BlockSpec((tm, d), lambda i: (i, 0)),
                  pl.BlockSpec((None, 1, d), lambda i: (l, 0, 0)),
                  pl.BlockSpec((None, d, LANES), lambda i: (l, 0, 0)),
                  pl.BlockSpec((None, 1, LANES), lambda i: (l, 0, 0))],
        out_specs=[pl.BlockSpec((tm * n_sub, LANES), lambda i: (i, 0)),
                   pl.BlockSpec((SUBLANES, tm), lambda i: (0, i)),
                   pl.BlockSpec((tm, LANES), lambda i: (i, 0)),
                   pl.BlockSpec((1, LANES), lambda i: (0, 0))],
        out_shape=[jax.ShapeDtypeStruct((n * n_sub, LANES), F32),
                   jax.ShapeDtypeStruct((SUBLANES, n), jnp.int32),
                   jax.ShapeDtypeStruct((n, LANES), F32),
                   jax.ShapeDtypeStruct((1, LANES), F32)],
        scratch_shapes=[pltpu.VMEM((1, LANES), F32)],
        name="router",
        compiler_params=_cparams(("arbitrary",)),
    )(x, nw, wr, br)


def _slots_kernel(d0_ref, d1_ref, pair_ref):
    def body(t, carry):
        pair_ref[d0_ref[t]] = TOP_K * t
        pair_ref[d1_ref[t]] = TOP_K * t + 1
        return carry

    lax.fori_loop(0, d0_ref.shape[0], body, 0, unroll=16)


def _slots(d0, d1, cap):
    return pl.pallas_call(
        _slots_kernel,
        grid_spec=pltpu.PrefetchScalarGridSpec(
            num_scalar_prefetch=2,
            grid=(1,),
            in_specs=[],
            out_specs=pl.BlockSpec(memory_space=pltpu.SMEM)),
        out_shape=jax.ShapeDtypeStruct((cap,), jnp.int32),
        name="slots",
        compiler_params=pltpu.CompilerParams(dimension_semantics=("arbitrary",)),
    )(d0, d1)


def _experts_kernel(be_ref, nv_ref, pair_ref, hn_ref, wg_ref, wu_ref, wd_ref, y2_ref,
                    xbuf, ybuf, gsem, ssem, wg_bf, wu_bf, wd_bf, *, n_sub):
    i = pl.program_id(0)
    n_i = pl.num_programs(0)
    slot = i % 2

    def row(r):
        return pl.ds(pl.multiple_of(r * n_sub, n_sub), n_sub)

    def gather(blk, sl, r):
        tok = jnp.right_shift(pair_ref[blk * MOE_ROWS + r], TOP_K.bit_length() - 1)
        return pltpu.make_async_copy(hn_ref.at[row(tok)], xbuf.at[sl, row(r)], gsem.at[sl])

    def scatter(blk, sl, r):
        return pltpu.make_async_copy(ybuf.at[sl, row(r)], y2_ref.at[row(pair_ref[blk * MOE_ROWS + r])], ssem.at[sl])

    def start_valid(copy, blk, sl):
        n_valid = nv_ref[blk]
        n_full = n_valid // ISSUE_UNROLL

        def body(j, carry):
            for k in range(ISSUE_UNROLL):
                copy(blk, sl, j * ISSUE_UNROLL + k).start()
            return carry

        def tail(r, carry):
            copy(blk, sl, r).start()
            return carry

        lax.fori_loop(0, n_full, body, 0)
        lax.fori_loop(n_full * ISSUE_UNROLL, n_valid, tail, 0)

    def wait_valid(buf, blk, sl):
        n_valid = nv_ref[blk]

        @pl.when(n_valid > 0)
        def _():
            rows = pl.ds(0, n_valid * n_sub)
            if buf is xbuf:
                pltpu.make_async_copy(hn_ref.at[rows], xbuf.at[sl, rows], gsem.at[sl]).wait()
            else:
                pltpu.make_async_copy(ybuf.at[sl, rows], y2_ref.at[rows], ssem.at[sl]).wait()

    @pl.when(i == 0)
    def _():
        xbuf[...] = jnp.zeros(xbuf.shape, F32)
        start_valid(gather, 0, 0)

    @pl.when(i + 1 < n_i)
    def _():
        start_valid(gather, i + 1, 1 - slot)

    wait_valid(xbuf, i, slot)

    @pl.when(i >= 2)
    def _():
        wait_valid(ybuf, i - 2, slot)

    @pl.when(nv_ref[i] > 0)
    def _():
        changed = jnp.logical_or(i == 0, be_ref[i] != be_ref[jnp.maximum(i - 1, 0)])

        @pl.when(changed)
        def _():
            wg_bf[...] = wg_ref[...].astype(BF16)
            wu_bf[...] = wu_ref[...].astype(BF16)
            wd_bf[...] = wd_ref[...].astype(BF16)

        x = jnp.concatenate([xbuf[slot, pl.ds(s, MOE_ROWS, stride=n_sub), :] for s in range(n_sub)], axis=1)
        x = x.astype(BF16)
        g = jnp.dot(x, wg_bf[...], preferred_element_type=F32)
        u = jnp.dot(x, wu_bf[...], preferred_element_type=F32)
        y = jnp.dot((_silu(g) * u).astype(BF16), wd_bf[...], preferred_element_type=F32)
        for s in range(n_sub):
            ybuf[slot, pl.ds(s, MOE_ROWS, stride=n_sub), :] = y[:, s * LANES:(s + 1) * LANES]

    start_valid(scatter, i, slot)

    @pl.when(i == n_i - 1)
    def _():
        @pl.when(i >= 1)
        def _():
            wait_valid(ybuf, i - 1, 1 - slot)
        wait_valid(ybuf, i, slot)


def _experts(block_e, n_valid, pairs, hn, w_g, w_u, w_d, l):
    _, _, d, f = w_g.shape
    n_sub = d // LANES
    n = hn.shape[0] // n_sub
    n_blocks = block_e.shape[0]
    buf = pltpu.VMEM((2, MOE_ROWS * n_sub, LANES), F32)
    return pl.pallas_call(
        functools.partial(_experts_kernel, n_sub=n_sub),
        grid_spec=pltpu.PrefetchScalarGridSpec(
            num_scalar_prefetch=3,
            grid=(n_blocks,),
            in_specs=[pl.BlockSpec(memory_space=pl.ANY),
                      pl.BlockSpec((None, None, d, f), lambda i, be, nv, pr: (l, be[i], 0, 0)),
                      pl.BlockSpec((None, None, d, f), lambda i, be, nv, pr: (l, be[i], 0, 0)),
                      pl.BlockSpec((None, None, f, d), lambda i, be, nv, pr: (l, be[i], 0, 0))],
            out_specs=pl.BlockSpec(memory_space=pl.ANY),
            scratch_shapes=[buf, buf, pltpu.SemaphoreType.DMA((2,)), pltpu.SemaphoreType.DMA((2,)),
                            pltpu.VMEM((d, f), BF16), pltpu.VMEM((d, f), BF16), pltpu.VMEM((f, d), BF16)]),
        out_shape=jax.ShapeDtypeStruct((n * TOP_K * n_sub, LANES), F32),
        name="experts",
        compiler_params=pltpu.CompilerParams(dimension_semantics=("arbitrary",), vmem_limit_bytes=VMEM_LIMIT,
                                             has_side_effects=True),
    )(block_e, n_valid, pairs, hn, w_g, w_u, w_d)


def _combine_kernel(x_ref, mf_ref, nw_ref, y2_ref, y_ref, *, final_norm):
    tm = x_ref.shape[0]
    n_sub = x_ref.shape[1] // LANES
    w = mf_ref[...]
    stride = TOP_K * n_sub
    y0 = jnp.concatenate([y2_ref[pl.ds(s, tm, stride=stride), :] for s in range(n_sub)], axis=1)
    y1 = jnp.concatenate([y2_ref[pl.ds(n_sub + s, tm, stride=stride), :] for s in range(n_sub)], axis=1)
    out = x_ref[...] + (y0 * w[:, 0:1] + y1 * w[:, 1:2])
    if final_norm:
        out = _rms(out, nw_ref[...])
    y_ref[...] = out


def _combine(x, mf, nw, y2, final_norm, row0, n):
    d = x.shape[1]
    n_sub = d // LANES
    tm = _pick(n, (1024, 512, 256, 128))
    assert row0 % tm == 0
    rb = row0 // tm
    return pl.pallas_call(
        functools.partial(_combine_kernel, final_norm=final_norm),
        grid=(n // tm,),
        in_specs=[pl.BlockSpec((tm, d), lambda i: (rb + i, 0)),
                  pl.BlockSpec((tm, LANES), lambda i: (rb + i, 0)),
                  pl.BlockSpec((1, d), lambda i: (0, 0)),
                  pl.BlockSpec((tm * TOP_K * n_sub, LANES), lambda i: (rb + i, 0))],
        out_specs=pl.BlockSpec((tm, d), lambda i: (i, 0)),
        out_shape=jax.ShapeDtypeStruct((n, d), F32),
        name="combine",
        compiler_params=_cparams(("parallel",)),
    )(x, mf, nw, y2)


def _moe(x, l, p, final_nw, out_ranges):
    n, d = x.shape
    n_sub = d // LANES
    hn, mi, mf, cnt = _router(x, p['norm_ffn'], p['w_r'], p['b_r'], l)
    counts = cnt[0, N_GROUPS:N_GROUPS + N_EXPERTS].astype(jnp.int32)
    padded = (counts + MOE_ROWS - 1) // MOE_ROWS * MOE_ROWS
    pad_end = jnp.cumsum(padded)
    pad_start = pad_end - padded
    experts = jnp.arange(N_EXPERTS, dtype=jnp.int32)

    def lookup(table, e):
        return jnp.sum(jnp.where(e[:, None] == experts[None, :], table[None, :], 0), axis=1)

    d0 = mi[2] + lookup(pad_start, mi[0])
    d1 = mi[3] + lookup(pad_start, mi[1])
    n_blocks = -(-(n * TOP_K) // MOE_ROWS) + N_EXPERTS
    block_row = jnp.arange(n_blocks, dtype=jnp.int32) * MOE_ROWS
    block_e = jnp.minimum(jnp.sum((pad_end[None, :] <= block_row[:, None]).astype(jnp.int32), axis=1), N_EXPERTS - 1)
    used_end = lookup(pad_start + counts, block_e)
    n_valid = jnp.clip(used_end - block_row, 0, MOE_ROWS).astype(jnp.int32)
    pairs = _slots(d0, d1, n_blocks * MOE_ROWS)
    y2 = _experts(block_e.astype(jnp.int32), n_valid, pairs, hn, p['w_eg'], p['w_eu'], p['w_ed'], l)
    nw = (final_nw if final_nw is not None else p['norm_ffn'][l, 0])[None]
    return [_combine(x, mf, nw, y2, final_nw is not None, row0, rows) for row0, rows in out_ranges]


def _rotary_tables(pos):
    half = HEAD_DIM // 2
    inv = ROPE_BASE ** (-jnp.arange(half, dtype=F32) / half)
    ang = pos.astype(F32)[:, None] * inv[None, :]
    cos = jnp.concatenate([jnp.cos(ang), jnp.cos(ang)], -1)
    sin = jnp.concatenate([-jnp.sin(ang), jnp.sin(ang)], -1)
    return cos, sin


def _trunks(xs, positions, states, p):
    depth = p['w_in'].shape[0]
    D = xs[0].shape[-1]
    shapes = [x.shape[:2] for x in xs]
    n_rows = [B * T for B, T in shapes]
    row0 = [sum(n_rows[:i]) for i in range(len(xs))]
    total = sum(n_rows)
    tables = [_rotary_tables(pos) for pos in positions]
    x_arrays = [x.reshape(n, D) for x, n in zip(xs, n_rows)]
    x_row0 = [0] * len(xs)
    new_states = [None] * len(xs)
    for l in range(depth):
        xm = None
        for i, (B, T) in enumerate(shapes):
            o, new_states[i] = _mixers(x_arrays[i], x_row0[i], B, T, *tables[i], states[i], l, new_states[i], p)
            xm = _merge(x_arrays[i], x_row0[i], o.reshape(n_rows[i], N_BRANCH * MIX_WIDTH), p['norm_mix'], p['w_in'],
                        p['w_branch'], p['w_out'], l, xm, row0[i], total)
        if l == depth - 1:
            outs = _moe(xm, l, p, p['norm_final'], list(zip(row0, n_rows)))
        else:
            x_all, = _moe(xm, l, p, None, [(0, total)])
            x_arrays, x_row0 = [x_all] * len(xs), row0
    results = []
    for y, (B, T), st in zip(outs, shapes, new_states):
        sh, sr, c, n, m, cb = st
        results.append((y.reshape(B, T, D), (sh, sr, c, n, m.reshape(depth, B, N_HEADS), cb)))
    return results


def kernel(x_prompt, x_sample, state_hgrn, state_ret, state_mlstm_C, state_mlstm_n, state_mlstm_m, state_mlstm_conv,
           norm_mix, norm_ffn, norm_final, w_in, hgrn_lb, hgrn_norm, ret_norm, mlstm_conv_w, mlstm_conv_b,
           mlstm_wq, mlstm_wk, mlstm_wv, mlstm_w_gates, mlstm_b_gates, mlstm_norm, mlstm_skip, w_branch, w_out,
           w_router_group, b_router_group, w_router_expert, b_router_expert, w_exp_gate, w_exp_up, w_exp_down):
    depth, D = norm_mix.shape
    H, d, W = N_HEADS, HEAD_DIM, MIX_WIDTH
    lb = jnp.cumsum(jax.nn.softmax(hgrn_lb.astype(F32), axis=0), axis=0)
    lb = lb - lb[0:1]
    prm = jnp.stack([jnp.log(lb), jnp.log1p(-lb), 1.0 - lb, hgrn_norm.astype(F32), ret_norm.astype(F32),
                     mlstm_norm.astype(F32), mlstm_skip.astype(F32), mlstm_conv_b.astype(F32)], axis=1)
    pad = LANES - N_GROUPS - N_EXPERTS
    w_r = jnp.concatenate([w_router_group, w_router_expert, jnp.zeros((depth, D, pad), F32)], -1)
    b_r = jnp.concatenate([b_router_group, b_router_expert, jnp.zeros((depth, pad), F32)], -1)[:, None, :]
    wg_bf = mlstm_w_gates.astype(BF16)
    p = {'norm_mix': norm_mix[:, None, :], 'norm_ffn': norm_ffn[:, None, :], 'norm_final': norm_final,
         'w_in': w_in.astype(BF16), 'prm': prm, 'conv_w': mlstm_conv_w.astype(F32),
         'mq': mlstm_wq.astype(BF16), 'mk': mlstm_wk.astype(BF16), 'mv': mlstm_wv.astype(BF16),
         'm_wg': wg_bf, 'm_wgt': jnp.swapaxes(wg_bf, 1, 2),
         'm_bg': mlstm_b_gates[:, None, :], 'm_bgt': mlstm_b_gates[:, :, None],
         'w_branch': w_branch.astype(BF16), 'w_out': w_out.astype(BF16),
         'w_r': w_r, 'b_r': b_r, 'w_eg': w_exp_gate, 'w_eu': w_exp_up, 'w_ed': w_exp_down}

    Bp, Tp = x_prompt.shape[0], x_prompt.shape[1]
    Bs = x_sample.shape[0]
    zero_states = (jnp.zeros((depth, Bp, H, d, d), F32), jnp.zeros((depth, Bp, H, d, d), F32),
                   jnp.zeros((depth, Bp, H, d, d), F32), jnp.zeros((depth, Bp, H, d), F32),
                   jnp.zeros((depth, Bp, 1, H), F32), jnp.zeros((depth, Bp, CONV_W - 1, W), F32))
    pos_prompt = jnp.arange(Tp, dtype=jnp.int32)
    pos_sample = PAST_LEN + jnp.arange(x_sample.shape[1], dtype=jnp.int32)
    sample_states = (state_hgrn, state_ret, state_mlstm_C, state_mlstm_n,
                     state_mlstm_m.reshape(depth, Bs, 1, H), state_mlstm_conv)
    (y_prompt, ps), (y_sample, ss) = _trunks([x_prompt, x_sample], [pos_prompt, pos_sample],
                                             [zero_states, sample_states], p)
    return (y_prompt, y_sample) + ps + ss
```
